```python
import math
import jax, jax.numpy as jnp
from jax import lax
import numpy as np

D_MODEL = 1024
BATCH = 8
SEQ = 4096
DEPTH = 2

N_MIXERS = 2
EPS = 1e-6
MEM_LEN = 256

MIX_WIDTH = D_MODEL
MEM_HEADS = 4
MEM_HEAD_DIM = 64
MEM_WIDTH = MEM_HEADS * MEM_HEAD_DIM
SEQ_WIDTH = MIX_WIDTH - MEM_WIDTH

A_HEAD_DIM = 64
A_HEADS = SEQ_WIDTH // A_HEAD_DIM
A_KV_RANK = 256
IDX_HEADS = 8
IDX_DIM = 64
IDX_TOPK_MAX = 256
Q_BLOCK = 128

REL_BUCKETS = 32
REL_MAX_DIST = 128

SSM_HEAD_DIM = 64
SSM_HEADS = SEQ_WIDTH // SSM_HEAD_DIM
SSM_GROUPS = 2
HEADS_PER_GROUP = SSM_HEADS // SSM_GROUPS
SSM_STATE = 128
CONV_WIDTH = 4
SSD_CHUNK = 128
CONV_DIM = SEQ_WIDTH + 2 * SSM_GROUPS * SSM_STATE

PEER_HEADS = 8
PEER_KEYS = 128
PEER_EXPERTS = PEER_KEYS * PEER_KEYS
PEER_QDIM = 256
PEER_TOPK = 16
TOKEN_BLOCK = 128

N_A_LAYERS = (DEPTH + 1) // 2
N_B_LAYERS = DEPTH // 2


def _split_points(widths):
    pts, acc = [], 0
    for w in widths[:-1]:
        acc += w
        pts.append(acc)
    return pts


A_WIDTHS = [SEQ_WIDTH, A_KV_RANK, IDX_HEADS * IDX_DIM, IDX_DIM, IDX_HEADS, MEM_WIDTH]
B_WIDTHS = [SEQ_WIDTH, CONV_DIM, SSM_HEADS, MEM_WIDTH]
A_IN = sum(A_WIDTHS)
B_IN = sum(B_WIDTHS)
A_SPLITS = _split_points(A_WIDTHS)
B_SPLITS = _split_points(B_WIDTHS)

kernel_name = "hybrid_dsa_ssd_peer_trunk"


def rms_norm(x, g):
    xf = x.astype(jnp.float32)
    y = xf * lax.rsqrt(jnp.mean(xf * xf, axis=-1, keepdims=True) + EPS)
    return (y * g.astype(jnp.float32)).astype(x.dtype)


def t5_causal_bucket(dist):
    n = jnp.maximum(dist, 0)
    max_exact = REL_BUCKETS // 2
    nf = jnp.maximum(n, max_exact).astype(jnp.float32)
    large = max_exact + (jnp.log(nf / max_exact) / math.log(REL_MAX_DIST / max_exact)
                         * (REL_BUCKETS - max_exact)).astype(jnp.int32)
    large = jnp.minimum(large, REL_BUCKETS - 1)
    return jnp.where(n < max_exact, n, large)


def memory_attention(q, mem_n, w_kv):
    b, s, _ = q.shape
    k, v = jnp.split(mem_n @ w_kv, 2, axis=-1)
    q = q.reshape(b, s, MEM_HEADS, MEM_HEAD_DIM)
    k = k.reshape(b, -1, MEM_HEADS, MEM_HEAD_DIM)
    v = v.reshape(b, -1, MEM_HEADS, MEM_HEAD_DIM)
    logits = jnp.einsum('bqhd,bmhd->bhqm', q, k).astype(jnp.float32) * (MEM_HEAD_DIM ** -0.5)
    p = jax.nn.softmax(logits, axis=-1).astype(v.dtype)
    o = jnp.einsum('bhqm,bmhd->bqhd', p, v)
    return o.reshape(b, s, MEM_WIDTH)


def dsa_attention(q, c_kv, iq, ik, iw, w_uk, w_uv, rel_bias):
    b, s = c_kv.shape[:2]
    topk = min(IDX_TOPK_MAX, s // 4)
    n_blk = s // Q_BLOCK
    key_pos = jnp.arange(s, dtype=jnp.int32)

    def block(i):
        t0 = i * Q_BLOCK
        qb = lax.dynamic_slice_in_dim(q, t0, Q_BLOCK, axis=1)
        iqb = lax.dynamic_slice_in_dim(iq, t0, Q_BLOCK, axis=1)
        iwb = lax.dynamic_slice_in_dim(iw, t0, Q_BLOCK, axis=1)
        q_pos = t0 + jnp.arange(Q_BLOCK, dtype=jnp.int32)
        rel = jax.nn.relu(jnp.einsum('bqhd,bsd->bqhs', iqb, ik).astype(jnp.float32) * (IDX_DIM ** -0.5))
        score = jnp.einsum('bqhs,bqh->bqs', rel, iwb.astype(jnp.float32) * (IDX_HEADS ** -0.5))
        causal = key_pos[None, :] <= q_pos[:, None]
        score = jnp.where(causal[None], score, -jnp.inf)
        _, sel = lax.top_k(score, topk)
        c_sel = jax.vmap(lambda c, idx: c[idx])(c_kv, sel)
        q_lat = jnp.einsum('bqhd,hcd->bqhc', qb, w_uk)
        logits = jnp.einsum('bqhc,bqkc->bhqk', q_lat, c_sel).astype(jnp.float32) * (A_HEAD_DIM ** -0.5)
        dist = q_pos[None, :, None] - sel
        bias = rel_bias[t5_causal_bucket(dist)]
        logits = logits + jnp.moveaxis(bias, -1, 1).astype(jnp.float32)
        logits = jnp.where((dist >= 0)[:, None], logits, -jnp.inf)
        p = jax.nn.softmax(logits, axis=-1).astype(c_sel.dtype)
        o_lat = jnp.einsum('bhqk,bqkc->bqhc', p, c_sel)
        o = jnp.einsum('bqhc,hcd->bqhd', o_lat, w_uv)
        return o.reshape(b, Q_BLOCK, SEQ_WIDTH)

    out = lax.map(block, jnp.arange(n_blk))
    return jnp.moveaxis(out, 0, 1).reshape(b, s, SEQ_WIDTH)


def causal_depthwise_conv(x, w, bias):
    c = x.shape[-1]
    y = lax.conv_general_dilated(x, w[:, None, :].astype(x.dtype), window_strides=(1,),
                                 padding=[(CONV_WIDTH - 1, 0)],
                                 dimension_numbers=('NWC', 'WIO', 'NWC'),
                                 feature_group_count=c)
    return y + bias.astype(x.dtype)


def ssd_scan(xh, dt, a, bm, cm):
    b, s, g, r, p = xh.shape
    n = bm.shape[-1]
    l = SSD_CHUNK
    c = s // l
    f32 = jnp.float32
    x = (xh.astype(f32) * dt[..., None]).reshape(b, c, l, g, r, p)
    da = (dt * a).reshape(b, c, l, g, r)
    bc = bm.astype(f32).reshape(b, c, l, g, n)
    cc = cm.astype(f32).reshape(b, c, l, g, n)
    a_cs = jnp.cumsum(da, axis=2)
    seg = a_cs[:, :, :, None] - a_cs[:, :, None, :]
    tri = jnp.tril(jnp.ones((l, l), dtype=bool))[None, None, :, :, None, None]
    decay = jnp.exp(jnp.where(tri, seg, -jnp.inf))
    cb = jnp.einsum('bclgn,bcsgn->bclsg', cc, bc)
    y_diag = jnp.einsum('bclsg,bclsgr,bcsgrp->bclgrp', cb, decay, x)
    decay_to_end = jnp.exp(a_cs[:, :, -1:] - a_cs)
    states = jnp.einsum('bclgn,bclgr,bclgrp->bcgrpn', bc, decay_to_end, x)
    chunk_decay = jnp.exp(a_cs[:, :, -1])

    def step(h, inp):
        st, dec = inp
        return h * dec[..., None, None] + st, h

    h0 = jnp.zeros((b, g, r, p, n), f32)
    _, h_prev = lax.scan(step, h0, (jnp.moveaxis(states, 1, 0), jnp.moveaxis(chunk_decay, 1, 0)))
    h_prev = jnp.moveaxis(h_prev, 0, 1)
    y_off = jnp.einsum('bclgn,bcgrpn,bclgr->bclgrp', cc, h_prev, jnp.exp(a_cs))
    return (y_diag + y_off).reshape(b, s, g, r, p)


def mamba2_mixer(z, xbc, dt_raw, conv_w, conv_b, dt_bias, a_log, d_skip, out_norm):
    b, s, _ = z.shape
    xbc = jax.nn.silu(causal_depthwise_conv(xbc, conv_w, conv_b))
    xs, bm, cm = jnp.split(xbc, [SEQ_WIDTH, SEQ_WIDTH + SSM_GROUPS * SSM_STATE], axis=-1)
    xh = xs.reshape(b, s, SSM_GROUPS, HEADS_PER_GROUP, SSM_HEAD_DIM)
    bm = bm.reshape(b, s, SSM_GROUPS, SSM_STATE)
    cm = cm.reshape(b, s, SSM_GROUPS, SSM_STATE)
    dt = jax.nn.softplus(dt_raw.astype(jnp.float32) + dt_bias.astype(jnp.float32))
    dt = dt.reshape(b, s, SSM_GROUPS, HEADS_PER_GROUP)
    a = -jnp.exp(a_log.astype(jnp.float32)).reshape(SSM_GROUPS, HEADS_PER_GROUP)
    d = d_skip.astype(jnp.float32).reshape(SSM_GROUPS, HEADS_PER_GROUP)
    y = ssd_scan(xh, dt, a, bm, cm) + xh.astype(jnp.float32) * d[..., None]
    y = y.reshape(b, s, SEQ_WIDTH) * jax.nn.silu(z.astype(jnp.float32))
    return rms_norm(y, out_norm).astype(z.dtype)


def peer_ffn(h, w_q, sub_keys, u, v):
    b, s, d = h.shape
    q = (h @ w_q).reshape(b, s, PEER_HEADS, 2, PEER_QDIM // 2)
    scores = jnp.einsum('bshic,ikc->bshik', q, sub_keys).astype(jnp.float32)
    top_s, top_i = lax.top_k(scores, PEER_TOPK)
    cand_s = top_s[..., 0, :, None] + top_s[..., 1, None, :]
    cand_i = top_i[..., 0, :, None] * PEER_KEYS + top_i[..., 1, None, :]
    cand_s = cand_s.reshape(b, s, PEER_HEADS, PEER_TOPK * PEER_TOPK)
    cand_i = cand_i.reshape(b, s, PEER_HEADS, PEER_TOPK * PEER_TOPK)
    best_s, pos = lax.top_k(cand_s, PEER_TOPK)
    expert = jnp.take_along_axis(cand_i, pos, axis=-1)
    gate = jax.nn.softmax(best_s, axis=-1).astype(h.dtype)
    nb = (b * s) // TOKEN_BLOCK
    hb = h.reshape(nb, TOKEN_BLOCK, d)
    eb = expert.reshape(nb, TOKEN_BLOCK, PEER_HEADS, PEER_TOPK)
    gb = gate.reshape(nb, TOKEN_BLOCK, PEER_HEADS, PEER_TOPK)

    def block(args):
        xt, et, gt = args
        act = jax.nn.gelu(jnp.einsum('td,thkd->thk', xt, u[et]), approximate=False)
        return jnp.einsum('thk,thkd->td', act * gt, v[et])

    out = lax.map(block, (hb, eb, gb))
    return out.reshape(b, s, d)


def setup_inputs(seed: int = 0) -> dict:
    key = jax.random.key(seed)
    ks = jax.random.split(key, 26)
    f32 = jnp.float32
    nA, nB = N_A_LAYERS, N_B_LAYERS

    def nrm(k, shape, scale):
        return jax.random.normal(k, shape, f32) * scale

    def gain(k, shape):
        return 1.0 + 0.05 * jax.random.normal(k, shape, f32)

    dt0 = jnp.exp(jax.random.uniform(ks[14], (nB, SSM_HEADS), f32, math.log(1e-3), math.log(1e-1)))
    return {
        "x": jax.random.normal(ks[0], (BATCH, SEQ, D_MODEL), f32),
        "mem": jax.random.normal(ks[1], (BATCH, MEM_LEN, D_MODEL), f32),
        "mem_norm": gain(ks[2], (D_MODEL,)),
        "rel_bias": nrm(ks[3], (REL_BUCKETS, A_HEADS), 0.5),
        "mix_norm": gain(ks[4], (DEPTH, D_MODEL)),
        "ffn_norm": gain(ks[5], (DEPTH, D_MODEL)),
        "final_norm": gain(ks[6], (D_MODEL,)),
        "w_o": nrm(ks[7], (DEPTH, MIX_WIDTH, D_MODEL), MIX_WIDTH ** -0.5),
        "w_mem_kv": nrm(ks[8], (DEPTH, D_MODEL, 2 * MEM_WIDTH), D_MODEL ** -0.5),
        "a_w_in": nrm(ks[9], (nA, D_MODEL, A_IN), D_MODEL ** -0.5),
        "a_kv_norm": gain(ks[10], (nA, A_KV_RANK)),
        "a_w_uk": nrm(ks[11], (nA, A_HEADS, A_KV_RANK, A_HEAD_DIM), A_KV_RANK ** -0.5),
        "a_w_uv": nrm(ks[12], (nA, A_HEADS, A_KV_RANK, A_HEAD_DIM), A_KV_RANK ** -0.5),
        "b_w_in": nrm(ks[13], (nB, D_MODEL, B_IN), D_MODEL ** -0.5),
        "b_conv_w": nrm(ks[15], (nB, CONV_WIDTH, CONV_DIM), CONV_WIDTH ** -0.5),
        "b_conv_b": nrm(ks[16], (nB, CONV_DIM), 0.02),
        "b_dt_bias": dt0 + jnp.log(-jnp.expm1(-dt0)),
        "b_a_log": jnp.log(jax.random.uniform(ks[17], (nB, SSM_HEADS), f32, 1.0, 16.0)),
        "b_d_skip": 1.0 + 0.1 * jax.random.normal(ks[18], (nB, SSM_HEADS), f32),
        "b_out_norm": gain(ks[19], (nB, SEQ_WIDTH)),
        "peer_w_q": nrm(ks[20], (DEPTH, D_MODEL, PEER_HEADS * PEER_QDIM), D_MODEL ** -0.5),
        "peer_sub_keys": nrm(ks[21], (DEPTH, 2, PEER_KEYS, PEER_QDIM // 2), (PEER_QDIM // 2) ** -0.5),
        "peer_u": nrm(ks[22], (DEPTH, PEER_EXPERTS, D_MODEL), D_MODEL ** -0.5),
        "peer_v": nrm(ks[23], (DEPTH, PEER_EXPERTS, D_MODEL), D_MODEL ** -0.5),
    }


def reference(x, mem, mem_norm, rel_bias, mix_norm, ffn_norm, final_norm, w_o, w_mem_kv,
              a_w_in, a_kv_norm, a_w_uk, a_w_uv,
              b_w_in, b_conv_w, b_conv_b, b_dt_bias, b_a_log, b_d_skip, b_out_norm,
              peer_w_q, peer_sub_keys, peer_u, peer_v):
    b, s, _ = x.shape
    mem_n = rms_norm(mem, mem_norm)
    for i in range(DEPTH):
        h = rms_norm(x, mix_norm[i])
        j = i // N_MIXERS
        if i % N_MIXERS == 0:
            q, c_kv, iq, ik, iw, q_mem = jnp.split(h @ a_w_in[j], A_SPLITS, axis=-1)
            c_kv = rms_norm(c_kv, a_kv_norm[j])
            seq_out = dsa_attention(q.reshape(b, s, A_HEADS, A_HEAD_DIM), c_kv,
                                    iq.reshape(b, s, IDX_HEADS, IDX_DIM), ik, iw,
                                    a_w_uk[j], a_w_uv[j], rel_bias)
        else:
            z, xbc, dt_raw, q_mem = jnp.split(h @ b_w_in[j], B_SPLITS, axis=-1)
            seq_out = mamba2_mixer(z, xbc, dt_raw, b_conv_w[j], b_conv_b[j], b_dt_bias[j],
                                   b_a_log[j], b_d_skip[j], b_out_norm[j])
        mem_out = memory_attention(q_mem, mem_n, w_mem_kv[i])
        x = x + jnp.concatenate([seq_out, mem_out], axis=-1) @ w_o[i]
        x = x + peer_ffn(rms_norm(x, ffn_norm[i]), peer_w_q[i], peer_sub_keys[i], peer_u[i], peer_v[i])
    return rms_norm(x, final_norm)
```

```python
import functools
import math

import jax
import jax.numpy as jnp
from jax import lax
from jax.experimental import pallas as pl
from jax.experimental.pallas import tpu as pltpu

F32 = jnp.float32
BF16 = jnp.bfloat16
I32 = jnp.int32

EPS = 1e-6
MEM_HEADS = 4
MEM_HEAD_DIM = 64
A_HEAD_DIM = 64
A_KV_RANK = 256
IDX_HEADS = 8
IDX_DIM = 64
IDX_TOPK = 256
REL_BUCKETS = 32
REL_MAX_DIST = 128
SSM_HEAD_DIM = 64
SSM_GROUPS = 2
SSM_STATE = 128
CONV_WIDTH = 4
SSD_CHUNK = 128
PEER_HEADS = 8
PEER_KEYS = 128
PEER_TOPK = 16

LANE = 128
SUBLANE = 8
INT_MIN = -(2 ** 31)
NEG_BIG = -1e30

PROJ_TOKENS = 512
ATT_Q = 256
ROUTE_TOKENS = 256
GATHER_TOKENS = 64
GATHER_SLOTS = 16
GATHER_LOOKAHEAD = 12
VMEM_LIMIT = 56 * 1024 * 1024


def _cparams(sem):
    return pltpu.CompilerParams(dimension_semantics=sem, vmem_limit_bytes=VMEM_LIMIT)


def _fold_rows(x, op):
    parts = [x[k * SUBLANE:(k + 1) * SUBLANE] for k in range(x.shape[0] // SUBLANE)]
    while len(parts) > 1:
        nxt = [op(parts[k], parts[k + 1]) for k in range(0, len(parts) - 1, 2)]
        if len(parts) % 2:
            nxt.append(parts[-1])
        parts = nxt
    return parts[0]


def _rms(x, g):
    return x * lax.rsqrt(jnp.mean(x * x, axis=-1, keepdims=True) + EPS) * g


def _split2(a):
    hi = a.astype(BF16)
    lo = (a - hi.astype(F32)).astype(BF16)
    return hi, lo


def _dot(a, b):
    return jnp.dot(a, b, preferred_element_type=F32)


def _dot_nt(a, b):
    return lax.dot_general(a, b, (((1,), (1,)), ((), ())), preferred_element_type=F32)


def _mem_kv_kernel(mem_ref, g_ref, w_ref, out_ref):
    y = _rms(mem_ref[0], g_ref[...])
    out_ref[0, 0] = _dot(y.astype(BF16), w_ref[0]).astype(BF16)


def _mem_kv(mem, mem_norm, w_pad):
    b, m, d = mem.shape
    depth, _, wcols = w_pad.shape
    return pl.pallas_call(
        _mem_kv_kernel,
        grid=(depth, b),
        in_specs=[
            pl.BlockSpec((1, m, d), lambda l, i: (i, 0, 0)),
            pl.BlockSpec((1, d), lambda l, i: (0, 0)),
            pl.BlockSpec((1, d, wcols), lambda l, i: (l, 0, 0)),
        ],
        out_specs=pl.BlockSpec((1, 1, m, wcols), lambda l, i: (l, i, 0, 0)),
        out_shape=jax.ShapeDtypeStruct((depth, b, m, wcols), BF16),
        compiler_params=_cparams(("arbitrary", "arbitrary")),
    )(mem, mem_norm.reshape(1, d), w_pad)


def _bias_kernel(rb_ref, out_ref):
    h = pl.program_id(0)
    max_exact = REL_BUCKETS // 2
    far = rb_ref[REL_BUCKETS - 1, h]
    krow = lax.broadcasted_iota(I32, (ATT_Q, ATT_Q), 0)
    qcol = lax.broadcasted_iota(I32, (ATT_Q, ATT_Q), 1)
    for r in range(2):
        dist = qcol - krow + ATT_Q * r
        n = jnp.maximum(dist, 0)
        nf = jnp.maximum(n, max_exact).astype(F32)
        large = max_exact + (jnp.log(nf / max_exact) / math.log(REL_MAX_DIST / max_exact)
                             * (REL_BUCKETS - max_exact)).astype(I32)
        large = jnp.minimum(large, REL_BUCKETS - 1)
        bucket = jnp.where(n < max_exact, n, large)
        acc = jnp.zeros((ATT_Q, ATT_Q), F32)
        for k in range(REL_BUCKETS):
            acc = jnp.where(bucket == k, rb_ref[k, h], acc)
        out_ref[0, r] = acc - far


def _bias_tiles(rel_bias):
    heads = rel_bias.shape[1]
    return pl.pallas_call(
        _bias_kernel,
        grid=(heads,),
        in_specs=[pl.BlockSpec(memory_space=pltpu.SMEM)],
        out_specs=pl.BlockSpec((1, 2, ATT_Q, ATT_Q), lambda h: (h, 0, 0, 0)),
        out_shape=jax.ShapeDtypeStruct((heads, 2, ATT_Q, ATT_Q), F32),
        compiler_params=_cparams(("arbitrary",)),
    )(rel_bias)


def _in_a_kernel(x_ref, g_ref, wq_ref, wc_ref, wm_ref, wih_ref, wil_ref, kvn_ref,
                 q_ref, c_ref, ct_ref, iq_ref, ka_ref, iw_ref, qm_ref):
    h = _rms(x_ref[...], g_ref[...])
    h_hi, h_lo = _split2(h)
    q_ref[...] = _dot(h_hi, wq_ref[...]).astype(BF16)
    qm_ref[...] = _dot(h_hi, wm_ref[...]).astype(BF16)
    c = _rms(_dot(h_hi, wc_ref[...]), kvn_ref[...])
    c_ref[...] = c.astype(BF16)
    for j in range(PROJ_TOKENS // ATT_Q):
        ct_ref[j] = c[j * ATT_Q:(j + 1) * ATT_Q].T.astype(BF16)
    wih = wih_ref[...]
    ii = _dot(h_hi, wih) + _dot(h_lo, wih) + _dot(h_hi, wil_ref[...])
    iq_ref[...] = ii[:, :IDX_HEADS * IDX_DIM]
    kk = ii[:, IDX_HEADS * IDX_DIM:IDX_HEADS * IDX_DIM + LANE]
    kk_hi, kk_lo = _split2(kk)
    lane = lax.broadcasted_iota(I32, kk.shape, 1)
    half = jnp.where(lane < IDX_DIM, kk_hi, kk_lo)
    ka_ref[...] = jnp.concatenate([half, half], axis=1)
    iw_ref[...] = ii[:, IDX_HEADS * IDX_DIM + LANE:]


def _in_a(x2, g, wq, wc, wm, wih, wil, kvn):
    n, d = x2.shape
    tt = PROJ_TOKENS
    full = lambda a: pl.BlockSpec(a.shape, lambda i: (0,) * a.ndim)
    row = lambda w: pl.BlockSpec((tt, w), lambda i: (i, 0))
    nblk = tt // ATT_Q
    outs = [
        jax.ShapeDtypeStruct((n, wq.shape[1]), BF16),
        jax.ShapeDtypeStruct((n, A_KV_RANK), BF16),
        jax.ShapeDtypeStruct((n // ATT_Q, A_KV_RANK, ATT_Q), BF16),
        jax.ShapeDtypeStruct((n, IDX_HEADS * IDX_DIM), F32),
        jax.ShapeDtypeStruct((n, 2 * LANE), BF16),
        jax.ShapeDtypeStruct((n, LANE), F32),
        jax.ShapeDtypeStruct((n, wm.shape[1]), BF16),
    ]
    out_specs = [
        row(wq.shape[1]), row(A_KV_RANK),
        pl.BlockSpec((nblk, A_KV_RANK, ATT_Q), lambda i: (i, 0, 0)),
        row(IDX_HEADS * IDX_DIM), row(2 * LANE), row(LANE), row(wm.shape[1]),
    ]
    return pl.pallas_call(
        _in_a_kernel,
        grid=(n // tt,),
        in_specs=[row(d), full(g), full(wq), full(wc), full(wm), full(wih), full(wil), full(kvn)],
        out_specs=out_specs,
        out_shape=outs,
        compiler_params=_cparams(("arbitrary",)),
    )(x2, g, wq, wc, wm, wih, wil, kvn)


def _dsa_kernel(q_ref, iq_ref, iw_ref, ka_ref, c_ref, ct_ref, wuk_ref, wuvt_ref, bias_ref,
                o_ref, key_s, lg_s, qbt_s, qt_s, olat_s, ot_s, *, heads):
    i = pl.program_id(1)
    nch = i + 1
    t0 = i * ATT_Q
    Q = ATT_Q
    krow = lax.broadcasted_iota(I32, (Q, Q), 0)
    qcol = lax.broadcasted_iota(I32, (Q, Q), 1)

    iqv = iq_ref[0]
    lane = lax.broadcasted_iota(I32, (Q, LANE), 1)
    for j in range(IDX_HEADS // 2):
        v = iqv[:, j * LANE:(j + 1) * LANE]
        r = pltpu.roll(v, IDX_DIM, 1)
        for hh, dup in ((2 * j, jnp.where(lane < IDX_DIM, v, r)),
                        (2 * j + 1, jnp.where(lane < IDX_DIM, r, v))):
            hi = dup.astype(BF16)
            lo = (dup - hi.astype(F32)).astype(BF16)
            qbt_s[hh, 0:LANE, :] = hi.astype(F32).T.astype(BF16)
            qbt_s[hh, LANE:2 * LANE, :] = lo.astype(F32).T.astype(BF16)
    wt = iw_ref[0].T * (IDX_HEADS ** -0.5)
    qt_s[...] = q_ref[0].astype(F32).T.astype(BF16)

    def score_chunk(c, carry):
        ka = ka_ref[0, pl.ds(pl.multiple_of(c * Q, Q), Q), :]
        acc = jnp.zeros((Q, Q), F32)
        for hh in range(IDX_HEADS):
            z = _dot(ka, qbt_s[hh])
            acc = acc + jnp.maximum(z, 0.0) * wt[hh:hh + 1, :]
        acc = acc * (IDX_DIM ** -0.5)
        bits = pltpu.bitcast(acc, I32)
        skey = jnp.where(bits < 0, bits ^ 0x7FFFFFFF, bits)
        causal = (krow + c * Q) <= (qcol + t0)
        key_s[pl.ds(pl.multiple_of(c * Q, Q), Q), :] = jnp.where(causal, skey, INT_MIN)
        return carry

    lax.fori_loop(0, nch, score_chunk, 0)

    def count_ge(thr):
        def body(c, acc):
            blk = key_s[pl.ds(pl.multiple_of(c * Q, Q), Q), :]
            return acc + _fold_rows((blk >= thr).astype(I32), jnp.add)
        acc = lax.fori_loop(0, nch, body, jnp.zeros((SUBLANE, Q), I32))
        return jnp.sum(acc, axis=0, keepdims=True)

    c0 = count_ge(jnp.zeros((1, Q), I32))
    thr = jnp.where(c0 >= IDX_TOPK, 0, INT_MIN).astype(I32)

    def bisect(it, thr):
        cand = thr + lax.shift_left(jnp.int32(1), 30 - it)
        return jnp.where(count_ge(cand) >= IDX_TOPK, cand, thr)

    thr = lax.fori_loop(0, 31, bisect, thr)
    thr = jnp.maximum(thr, INT_MIN + 1)
    cnt_gt = count_ge(thr + 1)
    need = IDX_TOPK - cnt_gt

    def count_tie_below(bound):
        def body(c, acc):
            blk = key_s[pl.ds(pl.multiple_of(c * Q, Q), Q), :]
            hit = (blk == thr) & ((krow + c * Q) < bound)
            return acc + _fold_rows(hit.astype(I32), jnp.add)
        acc = lax.fori_loop(0, nch, body, jnp.zeros((SUBLANE, Q), I32))
        return jnp.sum(acc, axis=0, keepdims=True)

    nbits = max(1, int(math.ceil(math.log2(ka_ref.shape[1] + 1))))

    def tie_bisect(it, p0):
        cand = p0 + lax.shift_left(jnp.int32(1), nbits - 1 - it)
        return jnp.where(count_tie_below(cand) < need, cand, p0)

    pcut = lax.fori_loop(0, nbits, tie_bisect, jnp.zeros((1, Q), I32)) + 1

    def head_body(h, carry):
        qh = qt_s[pl.ds(pl.multiple_of(h * A_HEAD_DIM, A_HEAD_DIM), A_HEAD_DIM), :]
        qlt = (_dot(wuk_ref[h], qh) * (A_HEAD_DIM ** -0.5)).astype(BF16)

        def logits_chunk(c, m_acc, band):
            off = pl.multiple_of(c * Q, Q)
            lg = _dot(c_ref[0, pl.ds(off, Q), :], qlt)
            if band is not None:
                lg = lg + bias_ref[h, band]
            key = key_s[pl.ds(off, Q), :]
            sel = (key > thr) | ((key == thr) & ((krow + c * Q) < pcut))
            lg = jnp.where(sel, lg, NEG_BIG)
            lg_s[pl.ds(off, Q), :] = lg
            return jnp.maximum(m_acc, _fold_rows(lg, jnp.maximum))

        m_acc = jnp.full((SUBLANE, Q), NEG_BIG, F32)
        m_acc = lax.fori_loop(0, nch - 2, lambda c, m: logits_chunk(c, m, None), m_acc)
        m_acc = lax.cond(nch >= 2, lambda m: logits_chunk(nch - 2, m, 1), lambda m: m, m_acc)
        m_acc = logits_chunk(nch - 1, m_acc, 0)
        m = jnp.max(m_acc, axis=0, keepdims=True)

        olat_s[...] = jnp.zeros_like(olat_s)

        def pv_chunk(c, s_acc):
            off = pl.multiple_of(c * Q, Q)
            p = jnp.exp(lg_s[pl.ds(off, Q), :] - m)
            olat_s[...] += _dot(ct_ref[0, c], p.astype(BF16))
            return s_acc + _fold_rows(p, jnp.add)

        s_acc = lax.fori_loop(0, nch, pv_chunk, jnp.zeros((SUBLANE, Q), F32))
        s = jnp.sum(s_acc, axis=0, keepdims=True)
        oh = _dot(wuvt_ref[h], olat_s[...].astype(BF16)) / s
        ot_s[pl.ds(pl.multiple_of(h * A_HEAD_DIM, A_HEAD_DIM), A_HEAD_DIM), :] = oh
        return carry

    lax.fori_loop(0, heads, head_body, 0)
    o_ref[0] = ot_s[...].T.astype(BF16)


def _dsa(q, iq, iw, ka, c, ct, wuk, wuvt, bias):
    b, s, w = q.shape
    heads = wuk.shape[0]
    nq = s // ATT_Q
    full = lambda a: pl.BlockSpec(a.shape, lambda bi, i: (0,) * a.ndim)
    blk = lambda width: pl.BlockSpec((1, ATT_Q, width), lambda bi, i: (bi, i, 0))
    per_b = lambda width: pl.BlockSpec((1, s, width), lambda bi, i: (bi, 0, 0))
    return pl.pallas_call(
        functools.partial(_dsa_kernel, heads=heads),
        grid=(b, nq),
        in_specs=[
            blk(w), blk(iq.shape[2]), blk(iw.shape[2]),
            per_b(ka.shape[2]), per_b(c.shape[2]),
            pl.BlockSpec((1, nq, A_KV_RANK, ATT_Q), lambda bi, i: (bi, 0, 0, 0)),
            full(wuk), full(wuvt), full(bias),
        ],
        out_specs=blk(w),
        out_shape=jax.ShapeDtypeStruct((b, s, w), BF16),
        scratch_shapes=[
            pltpu.VMEM((s, ATT_Q), I32),
            pltpu.VMEM((s, ATT_Q), F32),
            pltpu.VMEM((IDX_HEADS, 2 * LANE, ATT_Q), BF16),
            pltpu.VMEM((w, ATT_Q), BF16),
            pltpu.VMEM((A_KV_RANK, ATT_Q), F32),
            pltpu.VMEM((w, ATT_Q), F32),
        ],
        compiler_params=_cparams(("arbitrary", "arbitrary")),
    )(q, iq, iw, ka, c, ct, wuk, wuvt, bias)


def _out_kernel(x_ref, seq_ref, qm_ref, kv_ref, woa_ref, wob_ref, o_ref):
    acc = x_ref[...] + _dot(seq_ref[...], woa_ref[...])
    qm = qm_ref[...]
    kv = kv_ref[0]
    for h in range(MEM_HEADS):
        k = kv[:, h * LANE:(h + 1) * LANE]
        v = kv[:, (MEM_HEADS + h) * LANE:(MEM_HEADS + h + 1) * LANE]
        lg = _dot_nt(qm[:, h * LANE:(h + 1) * LANE], k) * (MEM_HEAD_DIM ** -0.5)
        p = jnp.exp(lg - jnp.max(lg, axis=-1, keepdims=True))
        oh = _dot(p.astype(BF16), v) / jnp.sum(p, axis=-1, keepdims=True)
        acc = acc + _dot(oh.astype(BF16), wob_ref[h])
    o_ref[...] = acc


def _out_proj(x2, seq2, qm2, kv, woa, wob, seq_len):
    n, d = x2.shape
    tt = PROJ_TOKENS
    per_seq = seq_len // tt
    full = lambda a: pl.BlockSpec(a.shape, lambda i: (0,) * a.ndim)
    row = lambda w: pl.BlockSpec((tt, w), lambda i: (i, 0))
    return pl.pallas_call(
        _out_kernel,
        grid=(n // tt,),
        in_specs=[row(d), row(seq2.shape[1]), row(qm2.shape[1]),
                  pl.BlockSpec((1,) + kv.shape[1:], lambda i: (i // per_seq, 0, 0)),
                  full(woa), full(wob)],
        out_specs=row(d),
        out_shape=jax.ShapeDtypeStruct((n, d), F32),
        compiler_params=_cparams(("arbitrary",)),
    )(x2, seq2, qm2, kv, woa, wob)


def _in_b_kernel(x_ref, g_ref, wz_ref, wx_ref, wm_ref, wdh_ref, wdl_ref,
                 z_ref, xbc_ref, dt_ref, qm_ref):
    h = _rms(x_ref[...], g_ref[...])
    h_hi, h_lo = _split2(h)
    z_ref[...] = _dot(h_hi, wz_ref[...]).astype(BF16)
    xbc_ref[...] = _dot(h_hi, wx_ref[...])
    qm_ref[...] = _dot(h_hi, wm_ref[...]).astype(BF16)
    wdh = wdh_ref[...]
    dt_ref[...] = _dot(h_hi, wdh) + _dot(h_lo, wdh) + _dot(h_hi, wdl_ref[...])


def _in_b(x2, g, wz, wx, wm, wdh, wdl):
    n, d = x2.shape
    tt = PROJ_TOKENS
    full = lambda a: pl.BlockSpec(a.shape, lambda i: (0,) * a.ndim)
    row = lambda w: pl.BlockSpec((tt, w), lambda i: (i, 0))
    return pl.pallas_call(
        _in_b_kernel,
        grid=(n // tt,),
        in_specs=[row(d), full(g), full(wz), full(wx), full(wm), full(wdh), full(wdl)],
        out_specs=[row(wz.shape[1]), row(wx.shape[1]), row(LANE), row(wm.shape[1])],
        out_shape=[jax.ShapeDtypeStruct((n, wz.shape[1]), BF16),
                   jax.ShapeDtypeStruct((n, wx.shape[1]), F32),
                   jax.ShapeDtypeStruct((n, LANE), F32),
                   jax.ShapeDtypeStruct((n, wm.shape[1]), BF16)],
        compiler_params=_cparams(("arbitrary",)),
    )(x2, g, wz, wx, wm, wdh, wdl)


def _ssd_kernel(xbc_ref, z_ref, dt_ref, cw_ref, cb_ref, dtb_ref, alog_ref, dsk_ref, on_ref,
                o_ref, tail_s, xpad_s, h_s, *, seq_width):
    L = SSD_CHUNK
    ci = pl.program_id(1)

    @pl.when(ci == 0)
    def _():
        tail_s[...] = jnp.zeros_like(tail_s)
        h_s[...] = jnp.zeros_like(h_s)

    xr = xbc_ref[0]
    xpad_s[0:SUBLANE, :] = tail_s[...]
    xpad_s[SUBLANE:SUBLANE + L, :] = xr
    tail_s[...] = xr[L - SUBLANE:L, :]
    conv = cb_ref[...] + jnp.zeros_like(xr)
    for j in range(CONV_WIDTH):
        off = SUBLANE - (CONV_WIDTH - 1) + j
        conv = conv + cw_ref[j:j + 1, :] * xpad_s[off:off + L, :]
    xc = conv * jax.nn.sigmoid(conv)
    gw = SSM_STATE
    bm = xc[:, seq_width:seq_width + SSM_GROUPS * gw]
    cm = xc[:, seq_width + SSM_GROUPS * gw:seq_width + 2 * SSM_GROUPS * gw]

    dt = jax.nn.softplus(dt_ref[0] + dtb_ref[...])
    a = -jnp.exp(alog_ref[...])
    da = dt * a
    row_i = lax.broadcasted_iota(I32, (L, L), 0)
    col_i = lax.broadcasted_iota(I32, (L, L), 1)
    tri = row_i >= col_i
    tri_b = tri.astype(BF16)
    d1 = da.astype(BF16)
    r1 = da - d1.astype(F32)
    d2 = r1.astype(BF16)
    d3 = (r1 - d2.astype(F32)).astype(BF16)
    acs = _dot(tri_b, d1) + _dot(tri_b, d2) + _dot(tri_b, d3)
    acs_t = acs.T
    a_last = acs[L - 1:L, :]
    e_acs = jnp.exp(acs)
    e_end = jnp.exp(a_last - acs)
    lane = lax.broadcasted_iota(I32, (L, LANE), 1)
    lo_half = lane < SSM_HEAD_DIM
    lane1 = lax.broadcasted_iota(I32, (1, LANE), 1)

    def pair_cols(m, r):
        return jnp.where(lo_half, m[:, r:r + 1], m[:, r + 1:r + 2])

    heads_per_group = (seq_width // SSM_HEAD_DIM) // SSM_GROUPS
    pairs_per_group = heads_per_group // 2
    pieces = []
    ssq = jnp.zeros((L, 1), F32)
    for g in range(SSM_GROUPS):
        bg = bm[:, g * gw:(g + 1) * gw]
        cg = cm[:, g * gw:(g + 1) * gw]
        bg_b = bg.astype(BF16)
        cg_b = cg.astype(BF16)
        bgt_b = bg.T.astype(BF16)
        cb = _dot_nt(cg_b, bg_b)
        for jp in range(pairs_per_group):
            r = g * heads_per_group + 2 * jp
            col0 = (g * pairs_per_group + jp) * LANE
            xs = xc[:, col0:col0 + LANE]
            xdt = (xs * pair_cols(dt, r)).astype(BF16)
            ys = []
            for rr in (r, r + 1):
                seg = acs[:, rr:rr + 1] - acs_t[rr:rr + 1, :]
                dec = jnp.exp(jnp.where(tri, seg, -jnp.inf))
                ys.append(_dot((cb * dec).astype(BF16), xdt))
            y = jnp.where(lo_half, ys[0], ys[1])
            hcol = jp * LANE
            h_prev = h_s[g, :, hcol:hcol + LANE]
            y = y + _dot(cg_b, h_prev.astype(BF16)) * pair_cols(e_acs, r)
            xw = (xs * pair_cols(dt * e_end, r)).astype(BF16)
            st = _dot(bgt_b, xw)
            cd = jnp.where(lane1 < SSM_HEAD_DIM, jnp.exp(a_last[:, r:r + 1]),
                           jnp.exp(a_last[:, r + 1:r + 2]))
            h_s[g, :, hcol:hcol + LANE] = h_prev * cd + st
            y = y + xs * dsk_ref[:, col0:col0 + LANE]
            zz = z_ref[0, :, col0:col0 + LANE].astype(F32)
            y = y * (zz * jax.nn.sigmoid(zz))
            ssq = ssq + jnp.sum(y * y, axis=-1, keepdims=True)
            pieces.append(y)
    scale = lax.rsqrt(ssq / seq_width + EPS)
    for k, y in enumerate(pieces):
        o_ref[0, :, k * LANE:(k + 1) * LANE] = (y * scale * on_ref[:, k * LANE:(k + 1) * LANE]).astype(BF16)


def _ssd(xbc, z, dt, cw, cb, dtb, alog, dsk, onorm):
    b, s, cdim = xbc.shape
    w = z.shape[2]
    L = SSD_CHUNK
    full = lambda a: pl.BlockSpec(a.shape, lambda bi, i: (0,) * a.ndim)
    blk = lambda width: pl.BlockSpec((1, L, width), lambda bi, i: (bi, i, 0))
    return pl.pallas_call(
        functools.partial(_ssd_kernel, seq_width=w),
        grid=(b, s // L),
        in_specs=[blk(cdim), blk(w), blk(LANE), full(cw), full(cb), full(dtb), full(alog),
                  full(dsk), full(onorm)],
        out_specs=blk(w),
        out_shape=jax.ShapeDtypeStruct((b, s, w), BF16),
        scratch_shapes=[
            pltpu.VMEM((SUBLANE, cdim), F32),
            pltpu.VMEM((SUBLANE + L, cdim), F32),
            pltpu.VMEM((SSM_GROUPS, SSM_STATE, w // SSM_GROUPS), F32),
        ],
        compiler_params=_cparams(("arbitrary", "arbitrary")),
    )(xbc, z, dt, cw, cb, dtb, alog, dsk, onorm)


def _top_rows(vals, k, val_out, idx_out, payload=None):
    rows = lax.broadcasted_iota(I32, vals.shape, 0)
    big = vals.shape[0]
    for j in range(k):
        m = jnp.max(_fold_rows(vals, jnp.maximum), axis=0, keepdims=True)
        am = jnp.min(_fold_rows(jnp.where(vals == m, rows, big), jnp.minimum), axis=0, keepdims=True)
        hit = rows == am
        val_out[j:j + 1, :] = m
        if payload is None:
            idx_out[j:j + 1, :] = am
        else:
            idx_out[j:j + 1, :] = jnp.sum(_fold_rows(jnp.where(hit, payload, 0), jnp.add),
                                          axis=0, keepdims=True)
        vals = jnp.where(hit, -jnp.inf, vals)


def _route_kernel(x_ref, g_ref, wqt_ref, sk_ref, hn_ref, e_ref, gate_ref,
                  ts_s, ti_s, bs_s, et_s, gt_s):
    T = ROUTE_TOKENS
    hb = _rms(x_ref[...], g_ref[...]).astype(BF16)
    hn_ref[...] = hb
    qt = _dot_nt(wqt_ref[...], hb)
    half = qt.shape[0] // (PEER_HEADS * 2)
    for hd in range(PEER_HEADS):
        for side in range(2):
            r0 = (hd * 2 + side) * half
            sc = _dot(sk_ref[side], qt[r0:r0 + half].astype(BF16))
            _top_rows(sc, PEER_TOPK, ts_s.at[side], ti_s.at[side])
        s0, s1 = ts_s[0], ts_s[1]
        i0, i1 = ti_s[0], ti_s[1]
        cand_s = jnp.concatenate([s0[a:a + 1, :] + s1 for a in range(PEER_TOPK)], axis=0)
        cand_i = jnp.concatenate([i0[a:a + 1, :] * PEER_KEYS + i1 for a in range(PEER_TOPK)], axis=0)
        _top_rows(cand_s, PEER_TOPK, bs_s, et_s.at[pl.ds(hd * PEER_TOPK, PEER_TOPK)], payload=cand_i)
        best = bs_s[...]
        p = jnp.exp(best - jnp.max(best, axis=0, keepdims=True))
        gt_s[hd * PEER_TOPK:(hd + 1) * PEER_TOPK, :] = p / jnp.sum(p, axis=0, keepdims=True)
    e_ref[...] = et_s[...].T
    gate_ref[...] = gt_s[...].T


def _route(x2, g, wqt, sk):
    n, d = x2.shape
    tt = ROUTE_TOKENS
    npair = PEER_HEADS * PEER_TOPK
    full = lambda a: pl.BlockSpec(a.shape, lambda i: (0,) * a.ndim)
    row = lambda w: pl.BlockSpec((tt, w), lambda i: (i, 0))
    return pl.pallas_call(
        _route_kernel,
        grid=(n // tt,),
        in_specs=[row(d), full(g), full(wqt), full(sk)],
        out_specs=[row(d), row(npair), row(npair)],
        out_shape=[jax.ShapeDtypeStruct((n, d), BF16),
                   jax.ShapeDtypeStruct((n, npair), I32),
                   jax.ShapeDtypeStruct((n, npair), F32)],
        scratch_shapes=[
            pltpu.VMEM((2, PEER_TOPK, tt), F32),
            pltpu.VMEM((2, PEER_TOPK, tt), I32),
            pltpu.VMEM((PEER_TOPK, tt), F32),
            pltpu.VMEM((npair, tt), I32),
            pltpu.VMEM((npair, tt), F32),
        ],
        compiler_params=_cparams(("arbitrary",)),
    )(x2, g, wqt, sk)


def _expert_kernel(idx_ref, gate_ref, hn_ref, x_ref, fn_ref, tab_ref, o_ref, buf, sem,
                   *, final_norm):
    T = GATHER_TOKENS
    npair = idx_ref.shape[1]

    def issue(t, slot):
        for p in range(npair):
            e = idx_ref[t, p]
            pltpu.make_async_copy(tab_ref.at[pl.ds(e, 1)], buf.at[slot, pl.ds(p, 1)],
                                  sem.at[slot]).start()

    def wait(slot):
        pltpu.make_async_copy(tab_ref.at[pl.ds(0, npair)], buf.at[slot], sem.at[slot]).wait()

    for t in range(GATHER_LOOKAHEAD):
        issue(t, t % GATHER_SLOTS)

    sub = lax.broadcasted_iota(I32, (SUBLANE, npair), 0)

    def body(t, carry):
        slot = lax.rem(t, GATHER_SLOTS)

        @pl.when(t + GATHER_LOOKAHEAD < T)
        def _():
            issue(t + GATHER_LOOKAHEAD, lax.rem(t + GATHER_LOOKAHEAD, GATHER_SLOTS))

        wait(slot)
        w = buf[slot]
        ub = pltpu.bitcast(lax.shift_left(w, jnp.uint32(16)), F32).astype(BF16)
        vb = pltpu.bitcast(w & jnp.uint32(0xFFFF0000), F32).astype(BF16)
        t8 = pl.multiple_of((t // SUBLANE) * SUBLANE, SUBLANE)
        mine = sub == (t - t8)
        dots = _dot_nt(hn_ref[pl.ds(t8, SUBLANE), :], ub)
        gelu = 0.5 * dots * (1.0 + lax.erf(dots * (2.0 ** -0.5)))
        act = gelu * gate_ref[pl.ds(t8, SUBLANE), :]
        act = jnp.where(mine, act, 0.0).astype(BF16)
        y = jnp.sum(_dot(act, vb), axis=0, keepdims=True)
        y = y + x_ref[pl.ds(t, 1), :]
        if final_norm:
            y = _rms(y, fn_ref[...])
        o_ref[pl.ds(t, 1), :] = y
        return carry

    lax.fori_loop(0, T, body, 0)


def _experts(idx, gate, hn, x2, fnorm, table, final_norm):
    n, d = x2.shape
    tt = GATHER_TOKENS
    npair = idx.shape[1]
    row = lambda w: pl.BlockSpec((tt, w), lambda i: (i, 0))
    return pl.pallas_call(
        functools.partial(_expert_kernel, final_norm=final_norm),
        grid=(n // tt,),
        in_specs=[
            pl.BlockSpec((tt, npair), lambda i: (i, 0), memory_space=pltpu.SMEM),
            row(npair), row(d), row(d),
            pl.BlockSpec((1, d), lambda i: (0, 0)),
            pl.BlockSpec(memory_space=pl.ANY),
        ],
        out_specs=row(d),
        out_shape=jax.ShapeDtypeStruct((n, d), F32),
        scratch_shapes=[
            pltpu.VMEM((GATHER_SLOTS, npair, d), jnp.uint32),
            pltpu.SemaphoreType.DMA((GATHER_SLOTS,)),
        ],
        compiler_params=_cparams(("arbitrary",)),
    )(idx, gate, hn, x2, fnorm, table)


def _pad_heads(w, heads, dim):
    d = w.shape[0]
    w = w.reshape(d, heads, dim)
    return jnp.pad(w, ((0, 0), (0, 0), (0, LANE - dim))).reshape(d, heads * LANE)


def _pack_table(u, v):
    ub = lax.bitcast_convert_type(u.astype(BF16), jnp.uint16).astype(jnp.uint32)
    vb = lax.bitcast_convert_type(v.astype(BF16), jnp.uint16).astype(jnp.uint32)
    return ub | (vb << 16)


def _split_w(w):
    hi = w.astype(BF16)
    return hi, (w - hi.astype(F32)).astype(BF16)


def kernel(x, mem, mem_norm, rel_bias, mix_norm, ffn_norm, final_norm, w_o, w_mem_kv, a_w_in,
           a_kv_norm, a_w_uk, a_w_uv, b_w_in, b_conv_w, b_conv_b, b_dt_bias, b_a_log, b_d_skip,
           b_out_norm, peer_w_q, peer_sub_keys, peer_u, peer_v):
    b, s, d = x.shape
    n = b * s
    depth = w_o.shape[0]
    mem_width = MEM_HEADS * MEM_HEAD_DIM
    seq_width = w_o.shape[1] - mem_width
    a_heads = seq_width // A_HEAD_DIM
    ssm_heads = seq_width // SSM_HEAD_DIM
    conv_dim = seq_width + 2 * SSM_GROUPS * SSM_STATE
    assert s % ATT_Q == 0 and s % PROJ_TOKENS == 0 and n % ROUTE_TOKENS == 0 and s >= 4 * IDX_TOPK

    wk, wv = w_mem_kv[:, :, :mem_width], w_mem_kv[:, :, mem_width:]
    w_kv_pad = jnp.concatenate(
        [jnp.stack([_pad_heads(wk[l], MEM_HEADS, MEM_HEAD_DIM) for l in range(depth)]),
         jnp.stack([_pad_heads(wv[l], MEM_HEADS, MEM_HEAD_DIM) for l in range(depth)])],
        axis=-1).astype(BF16)
    kv_all = _mem_kv(mem, mem_norm, w_kv_pad)

    x2 = x.reshape(n, d)
    for i in range(depth):
        j = i // 2
        woa = w_o[i, :seq_width].astype(BF16)
        wob = jnp.pad(w_o[i, seq_width:].reshape(MEM_HEADS, MEM_HEAD_DIM, d),
                      ((0, 0), (0, LANE - MEM_HEAD_DIM), (0, 0))).astype(BF16)
        g_mix = mix_norm[i].reshape(1, d)
        if i % 2 == 0:
            w_in = a_w_in[j]
            o0 = seq_width
            o1 = o0 + A_KV_RANK
            o2 = o1 + IDX_HEADS * IDX_DIM
            o3 = o2 + IDX_DIM
            o4 = o3 + IDX_HEADS
            wq = w_in[:, :o0].astype(BF16)
            wc = w_in[:, o0:o1].astype(BF16)
            wm = _pad_heads(w_in[:, o4:], MEM_HEADS, MEM_HEAD_DIM).astype(BF16)
            w_idx = jnp.concatenate(
                [w_in[:, o1:o2], w_in[:, o2:o3], w_in[:, o2:o3],
                 jnp.pad(w_in[:, o3:o4], ((0, 0), (0, LANE - IDX_HEADS)))], axis=1)
            wih, wil = _split_w(w_idx)
            q, c, ct, iq, ka, iw, qm = _in_a(x2, g_mix, wq, wc, wm, wih, wil,
                                             a_kv_norm[j].reshape(1, A_KV_RANK))
            bias = _bias_tiles(rel_bias)
            r3 = lambda t: t.reshape(b, s, t.shape[-1])
            seq = _dsa(r3(q), r3(iq), r3(iw), r3(ka), r3(c),
                       ct.reshape(b, s // ATT_Q, A_KV_RANK, ATT_Q),
                       a_w_uk[j].astype(BF16), jnp.swapaxes(a_w_uv[j], 1, 2).astype(BF16), bias)
        else:
            w_in = b_w_in[j]
            o0 = seq_width
            o1 = o0 + conv_dim
            o2 = o1 + ssm_heads
            wz = w_in[:, :o0].astype(BF16)
            wx = w_in[:, o0:o1].astype(BF16)
            wm = _pad_heads(w_in[:, o2:], MEM_HEADS, MEM_HEAD_DIM).astype(BF16)
            wdh, wdl = _split_w(jnp.pad(w_in[:, o1:o2], ((0, 0), (0, LANE - ssm_heads))))
            z, xbc, dt, qm = _in_b(x2, g_mix, wz, wx, wm, wdh, wdl)
            padh = lambda t: jnp.pad(t.reshape(1, ssm_heads), ((0, 0), (0, LANE - ssm_heads)))
            seq = _ssd(xbc.reshape(b, s, conv_dim), z.reshape(b, s, seq_width),
                       dt.reshape(b, s, LANE), b_conv_w[j], b_conv_b[j].reshape(1, conv_dim),
                       padh(b_dt_bias[j]), padh(b_a_log[j]),
                       jnp.repeat(b_d_skip[j], SSM_HEAD_DIM).reshape(1, seq_width),
                       b_out_norm[j].reshape(1, seq_width))
        x2 = _out_proj(x2, seq.reshape(n, seq_width), qm, kv_all[i], woa, wob, s)

        wqt = jnp.transpose(peer_w_q[i]).astype(BF16)
        hn, eidx, gate = _route(x2, ffn_norm[i].reshape(1, d), wqt, peer_sub_keys[i].astype(BF16))
        table = _pack_table(peer_u[i], peer_v[i])
        x2 = _experts(eidx, gate, hn, x2, final_norm.reshape(1, d), table, i == depth - 1)
    return x2.reshape(b, s, d)
```

```python
import functools
import math

import jax
import jax.numpy as jnp
from jax import lax
from jax.experimental import pallas as pl
from jax.experimental.pallas import tpu as pltpu

F32 = jnp.float32
BF16 = jnp.bfloat16
I32 = jnp.int32

EPS = 1e-6
MEM_HEADS = 4
MEM_HEAD_DIM = 64
A_HEAD_DIM = 64
A_KV_RANK = 256
IDX_HEADS = 8
IDX_DIM = 64
IDX_TOPK = 256
REL_BUCKETS = 32
REL_MAX_DIST = 128
SSM_HEAD_DIM = 64
SSM_GROUPS = 2
SSM_STATE = 128
CONV_WIDTH = 4
SSD_CHUNK = 128
PEER_HEADS = 8
PEER_KEYS = 128
PEER_TOPK = 16

LANE = 128
SUBLANE = 8
INT_MIN = -(2 ** 31)
NEG_BIG = -1e30

PROJ_TOKENS = 512
ATT_Q = 256
ROUTE_TOKENS = 256
GATHER_TOKENS = 64
GATHER_GROUP = 8
GATHER_RING = 3
GATHER_AHEAD = 2
VMEM_LIMIT = 56 * 1024 * 1024


def _cparams(sem):
    return pltpu.CompilerParams(dimension_semantics=sem, vmem_limit_bytes=VMEM_LIMIT)


def _fold_rows(x, op):
    parts = [x[k * SUBLANE:(k + 1) * SUBLANE] for k in range(x.shape[0] // SUBLANE)]
    while len(parts) > 1:
        nxt = [op(parts[k], parts[k + 1]) for k in range(0, len(parts) - 1, 2)]
        if len(parts) % 2:
            nxt.append(parts[-1])
        parts = nxt
    return parts[0]


def _rms(x, g):
    return x * lax.rsqrt(jnp.mean(x * x, axis=-1, keepdims=True) + EPS) * g


def _split2(a):
    hi = a.astype(BF16)
    lo = (a - hi.astype(F32)).astype(BF16)
    return hi, lo


def _dot(a, b):
    return jnp.dot(a, b, preferred_element_type=F32)


def _dot_nt(a, b):
    return lax.dot_general(a, b, (((1,), (1,)), ((), ())), preferred_element_type=F32)


def _mem_kv_kernel(mem_ref, g_ref, w_ref, out_ref):
    y = _rms(mem_ref[0], g_ref[...])
    out_ref[0, 0] = _dot(y.astype(BF16), w_ref[0]).astype(BF16)


def _mem_kv(mem, mem_norm, w_pad):
    b, m, d = mem.shape
    depth, _, wcols = w_pad.shape
    return pl.pallas_call(
        _mem_kv_kernel,
        grid=(depth, b),
        in_specs=[
            pl.BlockSpec((1, m, d), lambda l, i: (i, 0, 0)),
            pl.BlockSpec((1, d), lambda l, i: (0, 0)),
            pl.BlockSpec((1, d, wcols), lambda l, i: (l, 0, 0)),
        ],
        out_specs=pl.BlockSpec((1, 1, m, wcols), lambda l, i: (l, i, 0, 0)),
        out_shape=jax.ShapeDtypeStruct((depth, b, m, wcols), BF16),
        compiler_params=_cparams(("arbitrary", "arbitrary")),
    )(mem, mem_norm.reshape(1, d), w_pad)


def _bias_kernel(rb_ref, out_ref):
    h = pl.program_id(0)
    max_exact = REL_BUCKETS // 2
    far = rb_ref[REL_BUCKETS - 1, h]
    krow = lax.broadcasted_iota(I32, (ATT_Q, ATT_Q), 0)
    qcol = lax.broadcasted_iota(I32, (ATT_Q, ATT_Q), 1)
    for r in range(2):
        dist = qcol - krow + ATT_Q * r
        n = jnp.maximum(dist, 0)
        nf = jnp.maximum(n, max_exact).astype(F32)
        large = max_exact + (jnp.log(nf / max_exact) / math.log(REL_MAX_DIST / max_exact)
                             * (REL_BUCKETS - max_exact)).astype(I32)
        large = jnp.minimum(large, REL_BUCKETS - 1)
        bucket = jnp.where(n < max_exact, n, large)
        acc = jnp.zeros((ATT_Q, ATT_Q), F32)
        for k in range(REL_BUCKETS):
            acc = jnp.where(bucket == k, rb_ref[k, h], acc)
        out_ref[0, r] = acc - far


def _bias_tiles(rel_bias):
    heads = rel_bias.shape[1]
    return pl.pallas_call(
        _bias_kernel,
        grid=(heads,),
        in_specs=[pl.BlockSpec(memory_space=pltpu.SMEM)],
        out_specs=pl.BlockSpec((1, 2, ATT_Q, ATT_Q), lambda h: (h, 0, 0, 0)),
        out_shape=jax.ShapeDtypeStruct((heads, 2, ATT_Q, ATT_Q), F32),
        compiler_params=_cparams(("arbitrary",)),
    )(rel_bias)


def _in_a_kernel(x_ref, g_ref, wq_ref, wc_ref, wm_ref, wih_ref, wil_ref, kvn_ref,
                 q_ref, c_ref, ct_ref, iq_ref, ka_ref, iw_ref, qm_ref):
    h = _rms(x_ref[...], g_ref[...])
    h_hi, h_lo = _split2(h)
    q_ref[...] = _dot(h_hi, wq_ref[...]).astype(BF16)
    qm_ref[...] = _dot(h_hi, wm_ref[...]).astype(BF16)
    c = _rms(_dot(h_hi, wc_ref[...]), kvn_ref[...])
    c_ref[...] = c.astype(BF16)
    for j in range(PROJ_TOKENS // ATT_Q):
        ct_ref[j] = c[j * ATT_Q:(j + 1) * ATT_Q].T.astype(BF16)
    wih = wih_ref[...]
    ii = _dot(h_hi, wih) + _dot(h_lo, wih) + _dot(h_hi, wil_ref[...])
    iq_ref[...] = ii[:, :IDX_HEADS * IDX_DIM]
    kk = ii[:, IDX_HEADS * IDX_DIM:IDX_HEADS * IDX_DIM + LANE]
    kk_hi, kk_lo = _split2(kk)
    lane = lax.broadcasted_iota(I32, kk.shape, 1)
    half = jnp.where(lane < IDX_DIM, kk_hi, kk_lo)
    ka_ref[...] = jnp.concatenate([half, half], axis=1)
    iw_ref[...] = ii[:, IDX_HEADS * IDX_DIM + LANE:]


def _in_a(x2, g, wq, wc, wm, wih, wil, kvn):
    n, d = x2.shape
    tt = PROJ_TOKENS
    full = lambda a: pl.BlockSpec(a.shape, lambda i: (0,) * a.ndim)
    row = lambda w: pl.BlockSpec((tt, w), lambda i: (i, 0))
    nblk = tt // ATT_Q
    outs = [
        jax.ShapeDtypeStruct((n, wq.shape[1]), BF16),
        jax.ShapeDtypeStruct((n, A_KV_RANK), BF16),
        jax.ShapeDtypeStruct((n // ATT_Q, A_KV_RANK, ATT_Q), BF16),
        jax.ShapeDtypeStruct((n, IDX_HEADS * IDX_DIM), F32),
        jax.ShapeDtypeStruct((n, 2 * LANE), BF16),
        jax.ShapeDtypeStruct((n, LANE), F32),
        jax.ShapeDtypeStruct((n, wm.shape[1]), BF16),
    ]
    out_specs = [
        row(wq.shape[1]), row(A_KV_RANK),
        pl.BlockSpec((nblk, A_KV_RANK, ATT_Q), lambda i: (i, 0, 0)),
        row(IDX_HEADS * IDX_DIM), row(2 * LANE), row(LANE), row(wm.shape[1]),
    ]
    return pl.pallas_call(
        _in_a_kernel,
        grid=(n // tt,),
        in_specs=[row(d), full(g), full(wq), full(wc), full(wm), full(wih), full(wil), full(kvn)],
        out_specs=out_specs,
        out_shape=outs,
        compiler_params=_cparams(("arbitrary",)),
    )(x2, g, wq, wc, wm, wih, wil, kvn)


def _dsa_kernel(q_ref, iq_ref, iw_ref, ka_ref, c_ref, ct_ref, wuk_ref, wuvt_ref, bias_ref,
                o_ref, key_s, lg_s, qbt_s, qt_s, olat_s, ot_s, *, heads):
    i = pl.program_id(1)
    nch = i + 1
    t0 = i * ATT_Q
    Q = ATT_Q
    krow = lax.broadcasted_iota(I32, (Q, Q), 0)
    qcol = lax.broadcasted_iota(I32, (Q, Q), 1)

    iqv = iq_ref[0]
    lane = lax.broadcasted_iota(I32, (Q, LANE), 1)
    for j in range(IDX_HEADS // 2):
        v = iqv[:, j * LANE:(j + 1) * LANE]
        r = pltpu.roll(v, IDX_DIM, 1)
        for hh, dup in ((2 * j, jnp.where(lane < IDX_DIM, v, r)),
                        (2 * j + 1, jnp.where(lane < IDX_DIM, r, v))):
            hi = dup.astype(BF16)
            lo = (dup - hi.astype(F32)).astype(BF16)
            qbt_s[hh, 0:LANE, :] = hi.astype(F32).T.astype(BF16)
            qbt_s[hh, LANE:2 * LANE, :] = lo.astype(F32).T.astype(BF16)
    wt = iw_ref[0].T * (IDX_HEADS ** -0.5)
    qt_s[...] = q_ref[0].astype(F32).T.astype(BF16)

    def score_chunk(c, carry):
        ka = ka_ref[0, pl.ds(pl.multiple_of(c * Q, Q), Q), :]
        acc = jnp.zeros((Q, Q), F32)
        for hh in range(IDX_HEADS):
            z = _dot(ka, qbt_s[hh])
            acc = acc + jnp.maximum(z, 0.0) * wt[hh:hh + 1, :]
        acc = acc * (IDX_DIM ** -0.5)
        bits = pltpu.bitcast(acc, I32)
        skey = jnp.where(bits < 0, bits ^ 0x7FFFFFFF, bits)
        causal = (krow + c * Q) <= (qcol + t0)
        key_s[pl.ds(pl.multiple_of(c * Q, Q), Q), :] = jnp.where(causal, skey, INT_MIN)
        return carry

    lax.fori_loop(0, nch, score_chunk, 0)

    def count_ge(thr):
        def body(c, acc):
            blk = key_s[pl.ds(pl.multiple_of(c * Q, Q), Q), :]
            return acc + _fold_rows((blk >= thr).astype(I32), jnp.add)
        acc = lax.fori_loop(0, nch, body, jnp.zeros((SUBLANE, Q), I32))
        return jnp.sum(acc, axis=0, keepdims=True)

    c0 = count_ge(jnp.zeros((1, Q), I32))
    thr = jnp.where(c0 >= IDX_TOPK, 0, INT_MIN).astype(I32)

    def bisect(it, thr):
        cand = thr + lax.shift_left(jnp.int32(1), 30 - it)
        return jnp.where(count_ge(cand) >= IDX_TOPK, cand, thr)

    thr = lax.fori_loop(0, 31, bisect, thr)
    thr = jnp.maximum(thr, INT_MIN + 1)
    cnt_gt = count_ge(thr + 1)
    need = IDX_TOPK - cnt_gt

    def count_tie_below(bound):
        def body(c, acc):
            blk = key_s[pl.ds(pl.multiple_of(c * Q, Q), Q), :]
            hit = (blk == thr) & ((krow + c * Q) < bound)
            return acc + _fold_rows(hit.astype(I32), jnp.add)
        acc = lax.fori_loop(0, nch, body, jnp.zeros((SUBLANE, Q), I32))
        return jnp.sum(acc, axis=0, keepdims=True)

    nbits = max(1, int(math.ceil(math.log2(ka_ref.shape[1] + 1))))

    def tie_bisect(it, p0):
        cand = p0 + lax.shift_left(jnp.int32(1), nbits - 1 - it)
        return jnp.where(count_tie_below(cand) < need, cand, p0)

    pcut = lax.fori_loop(0, nbits, tie_bisect, jnp.zeros((1, Q), I32)) + 1

    def head_body(h, carry):
        qh = qt_s[pl.ds(pl.multiple_of(h * A_HEAD_DIM, A_HEAD_DIM), A_HEAD_DIM), :]
        qlt = (_dot(wuk_ref[h], qh) * (A_HEAD_DIM ** -0.5)).astype(BF16)

        def logits_chunk(c, m_acc, band):
            off = pl.multiple_of(c * Q, Q)
            lg = _dot(c_ref[0, pl.ds(off, Q), :], qlt)
            if band is not None:
                lg = lg + bias_ref[h, band]
            key = key_s[pl.ds(off, Q), :]
            sel = (key > thr) | ((key == thr) & ((krow + c * Q) < pcut))
            lg = jnp.where(sel, lg, NEG_BIG)
            lg_s[pl.ds(off, Q), :] = lg
            return jnp.maximum(m_acc, _fold_rows(lg, jnp.maximum))

        m_acc = jnp.full((SUBLANE, Q), NEG_BIG, F32)
        m_acc = lax.fori_loop(0, nch - 2, lambda c, m: logits_chunk(c, m, None), m_acc)
        m_acc = lax.cond(nch >= 2, lambda m: logits_chunk(nch - 2, m, 1), lambda m: m, m_acc)
        m_acc = logits_chunk(nch - 1, m_acc, 0)
        m = jnp.max(m_acc, axis=0, keepdims=True)

        olat_s[...] = jnp.zeros_like(olat_s)

        def pv_chunk(c, s_acc):
            off = pl.multiple_of(c * Q, Q)
            p = jnp.exp(lg_s[pl.ds(off, Q), :] - m)
            olat_s[...] += _dot(ct_ref[0, c], p.astype(BF16))
            return s_acc + _fold_rows(p, jnp.add)

        s_acc = lax.fori_loop(0, nch, pv_chunk, jnp.zeros((SUBLANE, Q), F32))
        s = jnp.sum(s_acc, axis=0, keepdims=True)
        oh = _dot(wuvt_ref[h], olat_s[...].astype(BF16)) / s
        ot_s[pl.ds(pl.multiple_of(h * A_HEAD_DIM, A_HEAD_DIM), A_HEAD_DIM), :] = oh
        return carry

    lax.fori_loop(0, heads, head_body, 0)
    o_ref[0] = ot_s[...].T.astype(BF16)


def _dsa(q, iq, iw, ka, c, ct, wuk, wuvt, bias):
    b, s, w = q.shape
    heads = wuk.shape[0]
    nq = s // ATT_Q
    full = lambda a: pl.BlockSpec(a.shape, lambda bi, i: (0,) * a.ndim)
    blk = lambda width: pl.BlockSpec((1, ATT_Q, width), lambda bi, i: (bi, i, 0))
    per_b = lambda width: pl.BlockSpec((1, s, width), lambda bi, i: (bi, 0, 0))
    return pl.pallas_call(
        functools.partial(_dsa_kernel, heads=heads),
        grid=(b, nq),
        in_specs=[
            blk(w), blk(iq.shape[2]), blk(iw.shape[2]),
            per_b(ka.shape[2]), per_b(c.shape[2]),
            pl.BlockSpec((1, nq, A_KV_RANK, ATT_Q), lambda bi, i: (bi, 0, 0, 0)),
            full(wuk), full(wuvt), full(bias),
        ],
        out_specs=blk(w),
        out_shape=jax.ShapeDtypeStruct((b, s, w), BF16),
        scratch_shapes=[
            pltpu.VMEM((s, ATT_Q), I32),
            pltpu.VMEM((s, ATT_Q), F32),
            pltpu.VMEM((IDX_HEADS, 2 * LANE, ATT_Q), BF16),
            pltpu.VMEM((w, ATT_Q), BF16),
            pltpu.VMEM((A_KV_RANK, ATT_Q), F32),
            pltpu.VMEM((w, ATT_Q), F32),
        ],
        compiler_params=_cparams(("arbitrary", "arbitrary")),
    )(q, iq, iw, ka, c, ct, wuk, wuvt, bias)


def _out_kernel(x_ref, seq_ref, qm_ref, kv_ref, woa_ref, wob_ref, o_ref):
    acc = x_ref[...] + _dot(seq_ref[...], woa_ref[...])
    qm = qm_ref[...]
    kv = kv_ref[0]
    for h in range(MEM_HEADS):
        k = kv[:, h * LANE:(h + 1) * LANE]
        v = kv[:, (MEM_HEADS + h) * LANE:(MEM_HEADS + h + 1) * LANE]
        lg = _dot_nt(qm[:, h * LANE:(h + 1) * LANE], k) * (MEM_HEAD_DIM ** -0.5)
        p = jnp.exp(lg - jnp.max(lg, axis=-1, keepdims=True))
        oh = _dot(p.astype(BF16), v) / jnp.sum(p, axis=-1, keepdims=True)
        acc = acc + _dot(oh.astype(BF16), wob_ref[h])
    o_ref[...] = acc


def _out_proj(x2, seq2, qm2, kv, woa, wob, seq_len):
    n, d = x2.shape
    tt = PROJ_TOKENS
    per_seq = seq_len // tt
    full = lambda a: pl.BlockSpec(a.shape, lambda i: (0,) * a.ndim)
    row = lambda w: pl.BlockSpec((tt, w), lambda i: (i, 0))
    return pl.pallas_call(
        _out_kernel,
        grid=(n // tt,),
        in_specs=[row(d), row(seq2.shape[1]), row(qm2.shape[1]),
                  pl.BlockSpec((1,) + kv.shape[1:], lambda i: (i // per_seq, 0, 0)),
                  full(woa), full(wob)],
        out_specs=row(d),
        out_shape=jax.ShapeDtypeStruct((n, d), F32),
        compiler_params=_cparams(("arbitrary",)),
    )(x2, seq2, qm2, kv, woa, wob)


def _in_b_kernel(x_ref, g_ref, wz_ref, wx_ref, wm_ref, wdh_ref, wdl_ref,
                 z_ref, xbc_ref, dt_ref, qm_ref):
    h = _rms(x_ref[...], g_ref[...])
    h_hi, h_lo = _split2(h)
    z_ref[...] = _dot(h_hi, wz_ref[...]).astype(BF16)
    xbc_ref[...] = _dot(h_hi, wx_ref[...])
    qm_ref[...] = _dot(h_hi, wm_ref[...]).astype(BF16)
    wdh = wdh_ref[...]
    dt_ref[...] = _dot(h_hi, wdh) + _dot(h_lo, wdh) + _dot(h_hi, wdl_ref[...])


def _in_b(x2, g, wz, wx, wm, wdh, wdl):
    n, d = x2.shape
    tt = PROJ_TOKENS
    full = lambda a: pl.BlockSpec(a.shape, lambda i: (0,) * a.ndim)
    row = lambda w: pl.BlockSpec((tt, w), lambda i: (i, 0))
    return pl.pallas_call(
        _in_b_kernel,
        grid=(n // tt,),
        in_specs=[row(d), full(g), full(wz), full(wx), full(wm), full(wdh), full(wdl)],
        out_specs=[row(wz.shape[1]), row(wx.shape[1]), row(LANE), row(wm.shape[1])],
        out_shape=[jax.ShapeDtypeStruct((n, wz.shape[1]), BF16),
                   jax.ShapeDtypeStruct((n, wx.shape[1]), F32),
                   jax.ShapeDtypeStruct((n, LANE), F32),
                   jax.ShapeDtypeStruct((n, wm.shape[1]), BF16)],
        compiler_params=_cparams(("arbitrary",)),
    )(x2, g, wz, wx, wm, wdh, wdl)


def _ssd_kernel(xbc_ref, z_ref, dt_ref, cw_ref, cb_ref, dtb_ref, alog_ref, dsk_ref, on_ref,
                o_ref, tail_s, xpad_s, h_s, *, seq_width):
    L = SSD_CHUNK
    ci = pl.program_id(1)

    @pl.when(ci == 0)
    def _():
        tail_s[...] = jnp.zeros_like(tail_s)
        h_s[...] = jnp.zeros_like(h_s)

    xr = xbc_ref[0]
    xpad_s[0:SUBLANE, :] = tail_s[...]
    xpad_s[SUBLANE:SUBLANE + L, :] = xr
    tail_s[...] = xr[L - SUBLANE:L, :]
    conv = cb_ref[...] + jnp.zeros_like(xr)
    for j in range(CONV_WIDTH):
        off = SUBLANE - (CONV_WIDTH - 1) + j
        conv = conv + cw_ref[j:j + 1, :] * xpad_s[off:off + L, :]
    xc = conv * jax.nn.sigmoid(conv)
    gw = SSM_STATE
    bm = xc[:, seq_width:seq_width + SSM_GROUPS * gw]
    cm = xc[:, seq_width + SSM_GROUPS * gw:seq_width + 2 * SSM_GROUPS * gw]

    dt = jax.nn.softplus(dt_ref[0] + dtb_ref[...])
    a = -jnp.exp(alog_ref[...])
    da = dt * a
    row_i = lax.broadcasted_iota(I32, (L, L), 0)
    col_i = lax.broadcasted_iota(I32, (L, L), 1)
    tri = row_i >= col_i
    tri_b = tri.astype(BF16)
    d1 = da.astype(BF16)
    r1 = da - d1.astype(F32)
    d2 = r1.astype(BF16)
    d3 = (r1 - d2.astype(F32)).astype(BF16)
    acs = _dot(tri_b, d1) + _dot(tri_b, d2) + _dot(tri_b, d3)
    acs_t = acs.T
    a_last = acs[L - 1:L, :]
    e_acs = jnp.exp(acs)
    e_end = jnp.exp(a_last - acs)
    lane = lax.broadcasted_iota(I32, (L, LANE), 1)
    lo_half = lane < SSM_HEAD_DIM
    lane1 = lax.broadcasted_iota(I32, (1, LANE), 1)

    def pair_cols(m, r):
        return jnp.where(lo_half, m[:, r:r + 1], m[:, r + 1:r + 2])

    heads_per_group = (seq_width // SSM_HEAD_DIM) // SSM_GROUPS
    pairs_per_group = heads_per_group // 2
    pieces = []
    ssq = jnp.zeros((L, 1), F32)
    for g in range(SSM_GROUPS):
        bg = bm[:, g * gw:(g + 1) * gw]
        cg = cm[:, g * gw:(g + 1) * gw]
        bg_b = bg.astype(BF16)
        cg_b = cg.astype(BF16)
        bgt_b = bg.T.astype(BF16)
        cb = _dot_nt(cg_b, bg_b)
        for jp in range(pairs_per_group):
            r = g * heads_per_group + 2 * jp
            col0 = (g * pairs_per_group + jp) * LANE
            xs = xc[:, col0:col0 + LANE]
            xdt = (xs * pair_cols(dt, r)).astype(BF16)
            ys = []
            for rr in (r, r + 1):
                seg = acs[:, rr:rr + 1] - acs_t[rr:rr + 1, :]
                dec = jnp.exp(jnp.where(tri, seg, -jnp.inf))
                ys.append(_dot((cb * dec).astype(BF16), xdt))
            y = jnp.where(lo_half, ys[0], ys[1])
            hcol = jp * LANE
            h_prev = h_s[g, :, hcol:hcol + LANE]
            y = y + _dot(cg_b, h_prev.astype(BF16)) * pair_cols(e_acs, r)
            xw = (xs * pair_cols(dt * e_end, r)).astype(BF16)
            st = _dot(bgt_b, xw)
            cd = jnp.where(lane1 < SSM_HEAD_DIM, jnp.exp(a_last[:, r:r + 1]),
                           jnp.exp(a_last[:, r + 1:r + 2]))
            h_s[g, :, hcol:hcol + LANE] = h_prev * cd + st
            y = y + xs * dsk_ref[:, col0:col0 + LANE]
            zz = z_ref[0, :, col0:col0 + LANE].astype(F32)
            y = y * (zz * jax.nn.sigmoid(zz))
            ssq = ssq + jnp.sum(y * y, axis=-1, keepdims=True)
            pieces.append(y)
    scale = lax.rsqrt(ssq / seq_width + EPS)
    for k, y in enumerate(pieces):
        o_ref[0, :, k * LANE:(k + 1) * LANE] = (y * scale * on_ref[:, k * LANE:(k + 1) * LANE]).astype(BF16)


def _ssd(xbc, z, dt, cw, cb, dtb, alog, dsk, onorm):
    b, s, cdim = xbc.shape
    w = z.shape[2]
    L = SSD_CHUNK
    full = lambda a: pl.BlockSpec(a.shape, lambda bi, i: (0,) * a.ndim)
    blk = lambda width: pl.BlockSpec((1, L, width), lambda bi, i: (bi, i, 0))
    return pl.pallas_call(
        functools.partial(_ssd_kernel, seq_width=w),
        grid=(b, s // L),
        in_specs=[blk(cdim), blk(w), blk(LANE), full(cw), full(cb), full(dtb), full(alog),
                  full(dsk), full(onorm)],
        out_specs=blk(w),
        out_shape=jax.ShapeDtypeStruct((b, s, w), BF16),
        scratch_shapes=[
            pltpu.VMEM((SUBLANE, cdim), F32),
            pltpu.VMEM((SUBLANE + L, cdim), F32),
            pltpu.VMEM((SSM_GROUPS, SSM_STATE, w // SSM_GROUPS), F32),
        ],
        compiler_params=_cparams(("arbitrary", "arbitrary")),
    )(xbc, z, dt, cw, cb, dtb, alog, dsk, onorm)


def _top_rows(vals, k, val_out, idx_out, payload=None):
    rows = lax.broadcasted_iota(I32, vals.shape, 0)
    big = vals.shape[0]
    for j in range(k):
        m = jnp.max(_fold_rows(vals, jnp.maximum), axis=0, keepdims=True)
        am = jnp.min(_fold_rows(jnp.where(vals == m, rows, big), jnp.minimum), axis=0, keepdims=True)
        hit = rows == am
        val_out[j:j + 1, :] = m
        if payload is None:
            idx_out[j:j + 1, :] = am
        else:
            idx_out[j:j + 1, :] = jnp.sum(_fold_rows(jnp.where(hit, payload, 0), jnp.add),
                                          axis=0, keepdims=True)
        vals = jnp.where(hit, -jnp.inf, vals)


def _route_kernel(x_ref, g_ref, wqt_ref, sk_ref, hn_ref, e_ref, gate_ref,
                  ts_s, ti_s, bs_s, et_s, gt_s):
    T = ROUTE_TOKENS
    hn = _rms(x_ref[...], g_ref[...])
    hn_ref[...] = hn
    hb = hn.astype(BF16)
    qt = _dot_nt(wqt_ref[...], hb)
    half = qt.shape[0] // (PEER_HEADS * 2)
    for hd in range(PEER_HEADS):
        for side in range(2):
            r0 = (hd * 2 + side) * half
            sc = _dot(sk_ref[side], qt[r0:r0 + half].astype(BF16))
            _top_rows(sc, PEER_TOPK, ts_s.at[side], ti_s.at[side])
        s0, s1 = ts_s[0], ts_s[1]
        i0, i1 = ti_s[0], ti_s[1]
        cand_s = jnp.concatenate([s0[a:a + 1, :] + s1 for a in range(PEER_TOPK)], axis=0)
        cand_i = jnp.concatenate([i0[a:a + 1, :] * PEER_KEYS + i1 for a in range(PEER_TOPK)], axis=0)
        _top_rows(cand_s, PEER_TOPK, bs_s, et_s.at[pl.ds(hd * PEER_TOPK, PEER_TOPK)], payload=cand_i)
        best = bs_s[...]
        p = jnp.exp(best - jnp.max(best, axis=0, keepdims=True))
        gt_s[hd * PEER_TOPK:(hd + 1) * PEER_TOPK, :] = p / jnp.sum(p, axis=0, keepdims=True)
    e_ref[...] = et_s[...].T
    gate_ref[...] = gt_s[...].T


def _route(x2, g, wqt, sk):
    n, d = x2.shape
    tt = ROUTE_TOKENS
    npair = PEER_HEADS * PEER_TOPK
    full = lambda a: pl.BlockSpec(a.shape, lambda i: (0,) * a.ndim)
    row = lambda w: pl.BlockSpec((tt, w), lambda i: (i, 0))
    return pl.pallas_call(
        _route_kernel,
        grid=(n // tt,),
        in_specs=[row(d), full(g), full(wqt), full(sk)],
        out_specs=[row(d), row(npair), row(npair)],
        out_shape=[jax.ShapeDtypeStruct((n, d), F32),
                   jax.ShapeDtypeStruct((n, npair), I32),
                   jax.ShapeDtypeStruct((n, npair), F32)],
        scratch_shapes=[
            pltpu.VMEM((2, PEER_TOPK, tt), F32),
            pltpu.VMEM((2, PEER_TOPK, tt), I32),
            pltpu.VMEM((PEER_TOPK, tt), F32),
            pltpu.VMEM((npair, tt), I32),
            pltpu.VMEM((npair, tt), F32),
        ],
        compiler_params=_cparams(("arbitrary",)),
    )(x2, g, wqt, sk)


def _expert_kernel(idx_ref, gate_ref, hn_ref, x_ref, fn_ref, tab_ref, o_ref, buf, sem,
                   *, final_norm):
    T = GATHER_TOKENS
    G = GATHER_GROUP
    npair = idx_ref.shape[1]
    ngroups = T // G
    rows = npair * SUBLANE

    def issue(g, ring):
        for j in range(G):
            for p in range(npair):
                e = idx_ref[g * G + j, p]
                pltpu.make_async_copy(tab_ref.at[e], buf.at[ring * G + j, p],
                                      sem.at[ring]).start(priority=p % 2)

    def wait(ring, j):
        pltpu.make_async_copy(tab_ref.at[pl.ds(0, npair)], buf.at[ring * G + j],
                              sem.at[ring]).wait()

    for g in range(GATHER_AHEAD):
        issue(g, g % GATHER_RING)

    sel = (lax.shift_right_logical(lax.broadcasted_iota(I32, (rows, npair), 0), 3)
           == lax.broadcasted_iota(I32, (rows, npair), 1)).astype(BF16)
    sel_t = (lax.broadcasted_iota(I32, (npair, rows), 0)
             == lax.shift_right_logical(lax.broadcasted_iota(I32, (npair, rows), 1), 3)).astype(BF16)
    diag = (lax.broadcasted_iota(I32, (SUBLANE, rows), 0)
            == (lax.broadcasted_iota(I32, (SUBLANE, rows), 1) & (SUBLANE - 1)))

    def body(g, carry):
        @pl.when(g + GATHER_AHEAD < ngroups)
        def _():
            issue(g + GATHER_AHEAD, lax.rem(g + GATHER_AHEAD, GATHER_RING))

        ring = lax.rem(g, GATHER_RING)
        t0 = pl.multiple_of(g * G, G)
        for j in range(G):
            wait(ring, j)
        parts = []
        for j in range(G):
            w = buf[ring * G + j].reshape(rows, LANE)
            mu = pltpu.bitcast(lax.shift_left(w, jnp.uint32(16)), F32).astype(BF16)
            gj = _dot_nt(hn_ref[t0 + j].astype(BF16), mu)
            parts.append(jnp.where(diag, gj, 0.0))
        dg = _dot(jnp.concatenate(parts, axis=0).astype(BF16), sel)
        dots = jnp.concatenate(
            [jnp.sum(dg[j * SUBLANE:(j + 1) * SUBLANE], axis=0, keepdims=True) for j in range(G)], axis=0)
        gelu = 0.5 * dots * (1.0 + lax.erf(dots * (2.0 ** -0.5)))
        act = (gelu * gate_ref[pl.ds(t0, G), :]).astype(BF16)
        arep = _dot(act, sel_t)
        for j in range(G):
            w = buf[ring * G + j].reshape(rows, LANE)
            mv = pltpu.bitcast(w & jnp.uint32(0xFFFF0000), F32).astype(BF16)
            aexp = jnp.where(diag, arep[j:j + 1, :], 0.0).astype(BF16)
            y = _dot(aexp, mv) + x_ref[t0 + j]
            if final_norm:
                ssq = jnp.sum(jnp.sum(y * y, axis=1, keepdims=True), axis=0, keepdims=True)
                y = y * lax.rsqrt(ssq / (SUBLANE * LANE) + EPS) * fn_ref[...]
            o_ref[t0 + j] = y
        return carry

    lax.fori_loop(0, ngroups, body, 0)


def _experts(idx, gate, hn3, x3, fnorm3, table3, final_norm):
    n = x3.shape[0]
    tt = GATHER_TOKENS
    npair = idx.shape[1]
    tile = lambda: pl.BlockSpec((tt, SUBLANE, LANE), lambda i: (i, 0, 0))
    return pl.pallas_call(
        functools.partial(_expert_kernel, final_norm=final_norm),
        grid=(n // tt,),
        in_specs=[
            pl.BlockSpec((tt, npair), lambda i: (i, 0), memory_space=pltpu.SMEM),
            pl.BlockSpec((tt, npair), lambda i: (i, 0)),
            tile(), tile(),
            pl.BlockSpec((SUBLANE, LANE), lambda i: (0, 0)),
            pl.BlockSpec(memory_space=pl.ANY),
        ],
        out_specs=tile(),
        out_shape=jax.ShapeDtypeStruct((n, SUBLANE, LANE), F32),
        scratch_shapes=[
            pltpu.VMEM((GATHER_RING * GATHER_GROUP, npair, SUBLANE, LANE), jnp.uint32),
            pltpu.SemaphoreType.DMA((GATHER_RING,)),
        ],
        compiler_params=_cparams(("arbitrary",)),
    )(idx, gate, hn3, x3, fnorm3, table3)


def _pad_heads(w, heads, dim):
    d = w.shape[0]
    w = w.reshape(d, heads, dim)
    return jnp.pad(w, ((0, 0), (0, 0), (0, LANE - dim))).reshape(d, heads * LANE)


def _pack_table(u, v):
    ub = lax.bitcast_convert_type(u.astype(BF16), jnp.uint16).astype(jnp.uint32)
    vb = lax.bitcast_convert_type(v.astype(BF16), jnp.uint16).astype(jnp.uint32)
    return ub | (vb << 16)


def _split_w(w):
    hi = w.astype(BF16)
    return hi, (w - hi.astype(F32)).astype(BF16)


def kernel(x, mem, mem_norm, rel_bias, mix_norm, ffn_norm, final_norm, w_o, w_mem_kv, a_w_in,
           a_kv_norm, a_w_uk, a_w_uv, b_w_in, b_conv_w, b_conv_b, b_dt_bias, b_a_log, b_d_skip,
           b_out_norm, peer_w_q, peer_sub_keys, peer_u, peer_v):
    b, s, d = x.shape
    n = b * s
    depth = w_o.shape[0]
    mem_width = MEM_HEADS * MEM_HEAD_DIM
    seq_width = w_o.shape[1] - mem_width
    a_heads = seq_width // A_HEAD_DIM
    ssm_heads = seq_width // SSM_HEAD_DIM
    conv_dim = seq_width + 2 * SSM_GROUPS * SSM_STATE
    assert s % ATT_Q == 0 and s % PROJ_TOKENS == 0 and n % ROUTE_TOKENS == 0 and s >= 4 * IDX_TOPK
    assert d == SUBLANE * LANE and n % GATHER_TOKENS == 0

    wk, wv = w_mem_kv[:, :, :mem_width], w_mem_kv[:, :, mem_width:]
    w_kv_pad = jnp.concatenate(
        [jnp.stack([_pad_heads(wk[l], MEM_HEADS, MEM_HEAD_DIM) for l in range(depth)]),
         jnp.stack([_pad_heads(wv[l], MEM_HEADS, MEM_HEAD_DIM) for l in range(depth)])],
        axis=-1).astype(BF16)
    kv_all = _mem_kv(mem, mem_norm, w_kv_pad)

    x2 = x.reshape(n, d)
    for i in range(depth):
        j = i // 2
        woa = w_o[i, :seq_width].astype(BF16)
        wob = jnp.pad(w_o[i, seq_width:].reshape(MEM_HEADS, MEM_HEAD_DIM, d),
                      ((0, 0), (0, LANE - MEM_HEAD_DIM), (0, 0))).astype(BF16)
        g_mix = mix_norm[i].reshape(1, d)
        if i % 2 == 0:
            w_in = a_w_in[j]
            o0 = seq_width
            o1 = o0 + A_KV_RANK
            o2 = o1 + IDX_HEADS * IDX_DIM
            o3 = o2 + IDX_DIM
            o4 = o3 + IDX_HEADS
            wq = w_in[:, :o0].astype(BF16)
            wc = w_in[:, o0:o1].astype(BF16)
            wm = _pad_heads(w_in[:, o4:], MEM_HEADS, MEM_HEAD_DIM).astype(BF16)
            w_idx = jnp.concatenate(
                [w_in[:, o1:o2], w_in[:, o2:o3], w_in[:, o2:o3],
                 jnp.pad(w_in[:, o3:o4], ((0, 0), (0, LANE - IDX_HEADS)))], axis=1)
            wih, wil = _split_w(w_idx)
            q, c, ct, iq, ka, iw, qm = _in_a(x2, g_mix, wq, wc, wm, wih, wil,
                                             a_kv_norm[j].reshape(1, A_KV_RANK))
            bias = _bias_tiles(rel_bias)
            r3 = lambda t: t.reshape(b, s, t.shape[-1])
            seq = _dsa(r3(q), r3(iq), r3(iw), r3(ka), r3(c),
                       ct.reshape(b, s // ATT_Q, A_KV_RANK, ATT_Q),
                       a_w_uk[j].astype(BF16), jnp.swapaxes(a_w_uv[j], 1, 2).astype(BF16), bias)
        else:
            w_in = b_w_in[j]
            o0 = seq_width
            o1 = o0 + conv_dim
            o2 = o1 + ssm_heads
            wz = w_in[:, :o0].astype(BF16)
            wx = w_in[:, o0:o1].astype(BF16)
            wm = _pad_heads(w_in[:, o2:], MEM_HEADS, MEM_HEAD_DIM).astype(BF16)
            wdh, wdl = _split_w(jnp.pad(w_in[:, o1:o2], ((0, 0), (0, LANE - ssm_heads))))
            z, xbc, dt, qm = _in_b(x2, g_mix, wz, wx, wm, wdh, wdl)
            padh = lambda t: jnp.pad(t.reshape(1, ssm_heads), ((0, 0), (0, LANE - ssm_heads)))
            seq = _ssd(xbc.reshape(b, s, conv_dim), z.reshape(b, s, seq_width),
                       dt.reshape(b, s, LANE), b_conv_w[j], b_conv_b[j].reshape(1, conv_dim),
                       padh(b_dt_bias[j]), padh(b_a_log[j]),
                       jnp.repeat(b_d_skip[j], SSM_HEAD_DIM).reshape(1, seq_width),
                       b_out_norm[j].reshape(1, seq_width))
        x2 = _out_proj(x2, seq.reshape(n, seq_width), qm, kv_all[i], woa, wob, s)

        wqt = jnp.transpose(peer_w_q[i]).astype(BF16)
        hn, eidx, gate = _route(x2, ffn_norm[i].reshape(1, d), wqt, peer_sub_keys[i].astype(BF16))
        tile3 = lambda t: t.reshape(t.shape[0], SUBLANE, LANE)
        table3 = tile3(_pack_table(peer_u[i], peer_v[i]))
        x2 = _experts(eidx, gate, tile3(hn), tile3(x2), final_norm.reshape(SUBLANE, LANE), table3,
                      i == depth - 1).reshape(n, d)
    return x2.reshape(b, s, d)
```

```python
import functools
import math

import jax
import jax.numpy as jnp
from jax import lax
from jax.experimental import pallas as pl
from jax.experimental.pallas import tpu as pltpu

F32 = jnp.float32
BF16 = jnp.bfloat16
I32 = jnp.int32

EPS = 1e-6
MEM_HEADS = 4
MEM_HEAD_DIM = 64
A_HEAD_DIM = 64
A_KV_RANK = 256
IDX_HEADS = 8
IDX_DIM = 64
IDX_TOPK = 256
REL_BUCKETS = 32
REL_MAX_DIST = 128
SSM_HEAD_DIM = 64
SSM_GROUPS = 2
SSM_STATE = 128
CONV_WIDTH = 4
SSD_CHUNK = 128
PEER_HEADS = 8
PEER_KEYS = 128
PEER_TOPK = 16

LANE = 128
SUBLANE = 8
INT_MIN = -(2 ** 31)
NEG_BIG = -1e30

PROJ_TOKENS = 512
ATT_Q = 256
ROUTE_TOKENS = 256
GATHER_GROUP = 8
GATHER_RING = 4
GATHER_AHEAD = 2
GATHER_TOKENS = GATHER_GROUP * GATHER_RING
VMEM_LIMIT = 56 * 1024 * 1024


def _cparams(sem):
    return pltpu.CompilerParams(dimension_semantics=sem, vmem_limit_bytes=VMEM_LIMIT)


def _fold_rows(x, op):
    parts = [x[k * SUBLANE:(k + 1) * SUBLANE] for k in range(x.shape[0] // SUBLANE)]
    while len(parts) > 1:
        nxt = [op(parts[k], parts[k + 1]) for k in range(0, len(parts) - 1, 2)]
        if len(parts) % 2:
            nxt.append(parts[-1])
        parts = nxt
    return parts[0]


def _rms(x, g):
    return x * lax.rsqrt(jnp.mean(x * x, axis=-1, keepdims=True) + EPS) * g


def _split2(a):
    hi = a.astype(BF16)
    lo = (a - hi.astype(F32)).astype(BF16)
    return hi, lo


def _dot(a, b):
    return jnp.dot(a, b, preferred_element_type=F32)


def _dot_nt(a, b):
    return lax.dot_general(a, b, (((1,), (1,)), ((), ())), preferred_element_type=F32)


def _mem_kv_kernel(mem_ref, g_ref, w_ref, out_ref):
    y = _rms(mem_ref[0], g_ref[...])
    out_ref[0, 0] = _dot(y.astype(BF16), w_ref[0]).astype(BF16)


def _mem_kv(mem, mem_norm, w_pad):
    b, m, d = mem.shape
    depth, _, wcols = w_pad.shape
    return pl.pallas_call(
        _mem_kv_kernel,
        grid=(depth, b),
        in_specs=[
            pl.BlockSpec((1, m, d), lambda l, i: (i, 0, 0)),
            pl.BlockSpec((1, d), lambda l, i: (0, 0)),
            pl.BlockSpec((1, d, wcols), lambda l, i: (l, 0, 0)),
        ],
        out_specs=pl.BlockSpec((1, 1, m, wcols), lambda l, i: (l, i, 0, 0)),
        out_shape=jax.ShapeDtypeStruct((depth, b, m, wcols), BF16),
        compiler_params=_cparams(("arbitrary", "arbitrary")),
    )(mem, mem_norm.reshape(1, d), w_pad)


def _bias_kernel(rb_ref, out_ref):
    h = pl.program_id(0)
    max_exact = REL_BUCKETS // 2
    far = rb_ref[REL_BUCKETS - 1, h]
    krow = lax.broadcasted_iota(I32, (ATT_Q, ATT_Q), 0)
    qcol = lax.broadcasted_iota(I32, (ATT_Q, ATT_Q), 1)
    for r in range(2):
        dist = qcol - krow + ATT_Q * r
        n = jnp.maximum(dist, 0)
        nf = jnp.maximum(n, max_exact).astype(F32)
        large = max_exact + (jnp.log(nf / max_exact) / math.log(REL_MAX_DIST / max_exact)
                             * (REL_BUCKETS - max_exact)).astype(I32)
        large = jnp.minimum(large, REL_BUCKETS - 1)
        bucket = jnp.where(n < max_exact, n, large)
        acc = jnp.zeros((ATT_Q, ATT_Q), F32)
        for k in range(REL_BUCKETS):
            acc = jnp.where(bucket == k, rb_ref[k, h], acc)
        out_ref[0, r] = acc - far


def _bias_tiles(rel_bias):
    heads = rel_bias.shape[1]
    return pl.pallas_call(
        _bias_kernel,
        grid=(heads,),
        in_specs=[pl.BlockSpec(memory_space=pltpu.SMEM)],
        out_specs=pl.BlockSpec((1, 2, ATT_Q, ATT_Q), lambda h: (h, 0, 0, 0)),
        out_shape=jax.ShapeDtypeStruct((heads, 2, ATT_Q, ATT_Q), F32),
        compiler_params=_cparams(("arbitrary",)),
    )(rel_bias)


def _in_a_kernel(x_ref, g_ref, wq_ref, wc_ref, wm_ref, wih_ref, wil_ref, kvn_ref,
                 q_ref, c_ref, ct_ref, iq_ref, ka_ref, iw_ref, qm_ref):
    h = _rms(x_ref[...], g_ref[...])
    h_hi, h_lo = _split2(h)
    q_ref[...] = _dot(h_hi, wq_ref[...]).astype(BF16)
    qm_ref[...] = _dot(h_hi, wm_ref[...]).astype(BF16)
    c = _rms(_dot(h_hi, wc_ref[...]), kvn_ref[...])
    c_ref[...] = c.astype(BF16)
    for j in range(PROJ_TOKENS // ATT_Q):
        ct_ref[j] = c[j * ATT_Q:(j + 1) * ATT_Q].T.astype(BF16)
    wih = wih_ref[...]
    ii = _dot(h_hi, wih) + _dot(h_lo, wih) + _dot(h_hi, wil_ref[...])
    iq_ref[...] = ii[:, :IDX_HEADS * IDX_DIM]
    kk = ii[:, IDX_HEADS * IDX_DIM:IDX_HEADS * IDX_DIM + LANE]
    kk_hi, kk_lo = _split2(kk)
    lane = lax.broadcasted_iota(I32, kk.shape, 1)
    half = jnp.where(lane < IDX_DIM, kk_hi, kk_lo)
    ka_ref[...] = jnp.concatenate([half, half], axis=1)
    iw_ref[...] = ii[:, IDX_HEADS * IDX_DIM + LANE:]


def _in_a(x2, g, wq, wc, wm, wih, wil, kvn):
    n, d = x2.shape
    tt = PROJ_TOKENS
    full = lambda a: pl.BlockSpec(a.shape, lambda i: (0,) * a.ndim)
    row = lambda w: pl.BlockSpec((tt, w), lambda i: (i, 0))
    nblk = tt // ATT_Q
    outs = [
        jax.ShapeDtypeStruct((n, wq.shape[1]), BF16),
        jax.ShapeDtypeStruct((n, A_KV_RANK), BF16),
        jax.ShapeDtypeStruct((n // ATT_Q, A_KV_RANK, ATT_Q), BF16),
        jax.ShapeDtypeStruct((n, IDX_HEADS * IDX_DIM), F32),
        jax.ShapeDtypeStruct((n, 2 * LANE), BF16),
        jax.ShapeDtypeStruct((n, LANE), F32),
        jax.ShapeDtypeStruct((n, wm.shape[1]), BF16),
    ]
    out_specs = [
        row(wq.shape[1]), row(A_KV_RANK),
        pl.BlockSpec((nblk, A_KV_RANK, ATT_Q), lambda i: (i, 0, 0)),
        row(IDX_HEADS * IDX_DIM), row(2 * LANE), row(LANE), row(wm.shape[1]),
    ]
    return pl.pallas_call(
        _in_a_kernel,
        grid=(n // tt,),
        in_specs=[row(d), full(g), full(wq), full(wc), full(wm), full(wih), full(wil), full(kvn)],
        out_specs=out_specs,
        out_shape=outs,
        compiler_params=_cparams(("arbitrary",)),
    )(x2, g, wq, wc, wm, wih, wil, kvn)


def _dsa_kernel(q_ref, iq_ref, iw_ref, ka_ref, c_ref, ct_ref, wuk_ref, wuvt_ref, bias_ref,
                o_ref, key_s, lg_s, qbt_s, qt_s, olat_s, ot_s, *, heads):
    i = pl.program_id(1)
    nch = i + 1
    t0 = i * ATT_Q
    Q = ATT_Q
    krow = lax.broadcasted_iota(I32, (Q, Q), 0)
    qcol = lax.broadcasted_iota(I32, (Q, Q), 1)

    iqv = iq_ref[0]
    lane = lax.broadcasted_iota(I32, (Q, LANE), 1)
    for j in range(IDX_HEADS // 2):
        v = iqv[:, j * LANE:(j + 1) * LANE]
        r = pltpu.roll(v, IDX_DIM, 1)
        for hh, dup in ((2 * j, jnp.where(lane < IDX_DIM, v, r)),
                        (2 * j + 1, jnp.where(lane < IDX_DIM, r, v))):
            hi = dup.astype(BF16)
            lo = (dup - hi.astype(F32)).astype(BF16)
            qbt_s[hh, 0:LANE, :] = hi.astype(F32).T.astype(BF16)
            qbt_s[hh, LANE:2 * LANE, :] = lo.astype(F32).T.astype(BF16)
    wt = iw_ref[0].T * (IDX_HEADS ** -0.5)
    qt_s[...] = q_ref[0].astype(F32).T.astype(BF16)

    def score_chunk(c, carry):
        ka = ka_ref[0, pl.ds(pl.multiple_of(c * Q, Q), Q), :]
        acc = jnp.zeros((Q, Q), F32)
        for hh in range(IDX_HEADS):
            z = _dot(ka, qbt_s[hh])
            acc = acc + jnp.maximum(z, 0.0) * wt[hh:hh + 1, :]
        acc = acc * (IDX_DIM ** -0.5)
        bits = pltpu.bitcast(acc, I32)
        skey = jnp.where(bits < 0, bits ^ 0x7FFFFFFF, bits)
        causal = (krow + c * Q) <= (qcol + t0)
        key_s[pl.ds(pl.multiple_of(c * Q, Q), Q), :] = jnp.where(causal, skey, INT_MIN)
        return carry

    lax.fori_loop(0, nch, score_chunk, 0)

    def count_ge(thr):
        def body(c, acc):
            blk = key_s[pl.ds(pl.multiple_of(c * Q, Q), Q), :]
            return acc + _fold_rows((blk >= thr).astype(I32), jnp.add)
        acc = lax.fori_loop(0, nch, body, jnp.zeros((SUBLANE, Q), I32))
        return jnp.sum(acc, axis=0, keepdims=True)

    c0 = count_ge(jnp.zeros((1, Q), I32))
    has_k = c0 >= IDX_TOPK
    thr = jnp.where(has_k, 0, INT_MIN).astype(I32)
    cnt = jnp.where(has_k, c0, nch * Q)
    reachable = (lax.broadcasted_iota(I32, (1, Q), 1) + t0 + 1) >= IDX_TOPK

    def unsettled(cnt):
        return jnp.max(jnp.where((cnt != IDX_TOPK) & reachable, 1, 0)) > 0

    def bisect(state):
        it, thr, cnt = state
        cand = thr + lax.shift_left(jnp.int32(1), 30 - it)
        c = count_ge(cand)
        ok = c >= IDX_TOPK
        return it + 1, jnp.where(ok, cand, thr), jnp.where(ok, c, cnt)

    _, thr, cnt = lax.while_loop(lambda st: (st[0] < 31) & unsettled(st[2]), bisect,
                                 (jnp.int32(0), thr, cnt))
    thr = jnp.maximum(thr, INT_MIN + 1)

    nbits = max(1, int(math.ceil(math.log2(ka_ref.shape[1] + 1))))

    def tie_cut(thr):
        need = IDX_TOPK - count_ge(thr + 1)

        def count_tie_below(bound):
            def body(c, acc):
                blk = key_s[pl.ds(pl.multiple_of(c * Q, Q), Q), :]
                hit = (blk == thr) & ((krow + c * Q) < bound)
                return acc + _fold_rows(hit.astype(I32), jnp.add)
            acc = lax.fori_loop(0, nch, body, jnp.zeros((SUBLANE, Q), I32))
            return jnp.sum(acc, axis=0, keepdims=True)

        def tie_bisect(it, p0):
            cand = p0 + lax.shift_left(jnp.int32(1), nbits - 1 - it)
            return jnp.where(count_tie_below(cand) < need, cand, p0)

        return lax.fori_loop(0, nbits, tie_bisect, jnp.zeros((1, Q), I32)) + 1

    pcut = lax.cond(unsettled(cnt), tie_cut, lambda thr: jnp.full((1, Q), 2 ** nbits, I32), thr)

    def mask_chunk(c, carry):
        off = pl.multiple_of(c * Q, Q)
        key = key_s[pl.ds(off, Q), :]
        sel = (key > thr) | ((key == thr) & ((krow + c * Q) < pcut))
        key_s[pl.ds(off, Q), :] = pltpu.bitcast(jnp.where(sel, 0.0, NEG_BIG).astype(F32), I32)
        return carry

    lax.fori_loop(0, nch, mask_chunk, 0)

    def head_body(h, carry):
        qh = qt_s[pl.ds(pl.multiple_of(h * A_HEAD_DIM, A_HEAD_DIM), A_HEAD_DIM), :]
        qlt = (_dot(wuk_ref[h], qh) * (A_HEAD_DIM ** -0.5)).astype(BF16)

        def logits_chunk(c, m_acc, band):
            off = pl.multiple_of(c * Q, Q)
            lg = _dot(c_ref[0, pl.ds(off, Q), :], qlt) + pltpu.bitcast(key_s[pl.ds(off, Q), :], F32)
            if band is not None:
                lg = lg + bias_ref[h, band]
            lg_s[pl.ds(off, Q), :] = lg
            return jnp.maximum(m_acc, _fold_rows(lg, jnp.maximum))

        m_acc = jnp.full((SUBLANE, Q), NEG_BIG, F32)
        m_acc = lax.fori_loop(0, nch - 2, lambda c, m: logits_chunk(c, m, None), m_acc)
        m_acc = lax.cond(nch >= 2, lambda m: logits_chunk(nch - 2, m, 1), lambda m: m, m_acc)
        m_acc = logits_chunk(nch - 1, m_acc, 0)
        m = jnp.max(m_acc, axis=0, keepdims=True)

        olat_s[...] = jnp.zeros_like(olat_s)

        def pv_chunk(c, s_acc):
            off = pl.multiple_of(c * Q, Q)
            p = jnp.exp(lg_s[pl.ds(off, Q), :] - m)
            olat_s[...] += _dot(ct_ref[0, c], p.astype(BF16))
            return s_acc + _fold_rows(p, jnp.add)

        s_acc = lax.fori_loop(0, nch, pv_chunk, jnp.zeros((SUBLANE, Q), F32))
        s = jnp.sum(s_acc, axis=0, keepdims=True)
        oh = _dot(wuvt_ref[h], olat_s[...].astype(BF16)) / s
        ot_s[pl.ds(pl.multiple_of(h * A_HEAD_DIM, A_HEAD_DIM), A_HEAD_DIM), :] = oh
        return carry

    lax.fori_loop(0, heads, head_body, 0)
    o_ref[0] = ot_s[...].T.astype(BF16)


def _dsa(q, iq, iw, ka, c, ct, wuk, wuvt, bias):
    b, s, w = q.shape
    heads = wuk.shape[0]
    nq = s // ATT_Q
    full = lambda a: pl.BlockSpec(a.shape, lambda bi, i: (0,) * a.ndim)
    blk = lambda width: pl.BlockSpec((1, ATT_Q, width), lambda bi, i: (bi, i, 0))
    per_b = lambda width: pl.BlockSpec((1, s, width), lambda bi, i: (bi, 0, 0))
    return pl.pallas_call(
        functools.partial(_dsa_kernel, heads=heads),
        grid=(b, nq),
        in_specs=[
            blk(w), blk(iq.shape[2]), blk(iw.shape[2]),
            per_b(ka.shape[2]), per_b(c.shape[2]),
            pl.BlockSpec((1, nq, A_KV_RANK, ATT_Q), lambda bi, i: (bi, 0, 0, 0)),
            full(wuk), full(wuvt), full(bias),
        ],
        out_specs=blk(w),
        out_shape=jax.ShapeDtypeStruct((b, s, w), BF16),
        scratch_shapes=[
            pltpu.VMEM((s, ATT_Q), I32),
            pltpu.VMEM((s, ATT_Q), F32),
            pltpu.VMEM((IDX_HEADS, 2 * LANE, ATT_Q), BF16),
            pltpu.VMEM((w, ATT_Q), BF16),
            pltpu.VMEM((A_KV_RANK, ATT_Q), F32),
            pltpu.VMEM((w, ATT_Q), F32),
        ],
        compiler_params=_cparams(("arbitrary", "arbitrary")),
    )(q, iq, iw, ka, c, ct, wuk, wuvt, bias)


def _out_kernel(x_ref, seq_ref, qm_ref, kv_ref, woa_ref, wob_ref, o_ref):
    acc = x_ref[...] + _dot(seq_ref[...], woa_ref[...])
    qm = qm_ref[...]
    kv = kv_ref[0]
    for h in range(MEM_HEADS):
        k = kv[:, h * LANE:(h + 1) * LANE]
        v = kv[:, (MEM_HEADS + h) * LANE:(MEM_HEADS + h + 1) * LANE]
        lg = _dot_nt(qm[:, h * LANE:(h + 1) * LANE], k) * (MEM_HEAD_DIM ** -0.5)
        p = jnp.exp(lg - jnp.max(lg, axis=-1, keepdims=True))
        oh = _dot(p.astype(BF16), v) / jnp.sum(p, axis=-1, keepdims=True)
        acc = acc + _dot(oh.astype(BF16), wob_ref[h])
    o_ref[...] = acc


def _out_proj(x2, seq2, qm2, kv, woa, wob, seq_len):
    n, d = x2.shape
    tt = PROJ_TOKENS
    per_seq = seq_len // tt
    full = lambda a: pl.BlockSpec(a.shape, lambda i: (0,) * a.ndim)
    row = lambda w: pl.BlockSpec((tt, w), lambda i: (i, 0))
    return pl.pallas_call(
        _out_kernel,
        grid=(n // tt,),
        in_specs=[row(d), row(seq2.shape[1]), row(qm2.shape[1]),
                  pl.BlockSpec((1,) + kv.shape[1:], lambda i: (i // per_seq, 0, 0)),
                  full(woa), full(wob)],
        out_specs=row(d),
        out_shape=jax.ShapeDtypeStruct((n, d), F32),
        compiler_params=_cparams(("arbitrary",)),
    )(x2, seq2, qm2, kv, woa, wob)


def _in_b_kernel(x_ref, g_ref, wz_ref, wx_ref, wm_ref, wdh_ref, wdl_ref,
                 z_ref, xbc_ref, dt_ref, qm_ref):
    h = _rms(x_ref[...], g_ref[...])
    h_hi, h_lo = _split2(h)
    z_ref[...] = _dot(h_hi, wz_ref[...]).astype(BF16)
    xbc_ref[...] = _dot(h_hi, wx_ref[...])
    qm_ref[...] = _dot(h_hi, wm_ref[...]).astype(BF16)
    wdh = wdh_ref[...]
    dt_ref[...] = _dot(h_hi, wdh) + _dot(h_lo, wdh) + _dot(h_hi, wdl_ref[...])


def _in_b(x2, g, wz, wx, wm, wdh, wdl):
    n, d = x2.shape
    tt = PROJ_TOKENS
    full = lambda a: pl.BlockSpec(a.shape, lambda i: (0,) * a.ndim)
    row = lambda w: pl.BlockSpec((tt, w), lambda i: (i, 0))
    return pl.pallas_call(
        _in_b_kernel,
        grid=(n // tt,),
        in_specs=[row(d), full(g), full(wz), full(wx), full(wm), full(wdh), full(wdl)],
        out_specs=[row(wz.shape[1]), row(wx.shape[1]), row(LANE), row(wm.shape[1])],
        out_shape=[jax.ShapeDtypeStruct((n, wz.shape[1]), BF16),
                   jax.ShapeDtypeStruct((n, wx.shape[1]), F32),
                   jax.ShapeDtypeStruct((n, LANE), F32),
                   jax.ShapeDtypeStruct((n, wm.shape[1]), BF16)],
        compiler_params=_cparams(("arbitrary",)),
    )(x2, g, wz, wx, wm, wdh, wdl)


def _ssd_kernel(xbc_ref, z_ref, dt_ref, cw_ref, cb_ref, dtb_ref, alog_ref, dsk_ref, on_ref,
                o_ref, tail_s, xpad_s, h_s, *, seq_width):
    L = SSD_CHUNK
    ci = pl.program_id(1)

    @pl.when(ci == 0)
    def _():
        tail_s[...] = jnp.zeros_like(tail_s)
        h_s[...] = jnp.zeros_like(h_s)

    xr = xbc_ref[0]
    xpad_s[0:SUBLANE, :] = tail_s[...]
    xpad_s[SUBLANE:SUBLANE + L, :] = xr
    tail_s[...] = xr[L - SUBLANE:L, :]
    conv = cb_ref[...] + jnp.zeros_like(xr)
    for j in range(CONV_WIDTH):
        off = SUBLANE - (CONV_WIDTH - 1) + j
        conv = conv + cw_ref[j:j + 1, :] * xpad_s[off:off + L, :]
    xc = conv * jax.nn.sigmoid(conv)
    gw = SSM_STATE
    bm = xc[:, seq_width:seq_width + SSM_GROUPS * gw]
    cm = xc[:, seq_width + SSM_GROUPS * gw:seq_width + 2 * SSM_GROUPS * gw]

    dt = jax.nn.softplus(dt_ref[0] + dtb_ref[...])
    a = -jnp.exp(alog_ref[...])
    da = dt * a
    row_i = lax.broadcasted_iota(I32, (L, L), 0)
    col_i = lax.broadcasted_iota(I32, (L, L), 1)
    tri = row_i >= col_i
    tri_b = tri.astype(BF16)
    d1 = da.astype(BF16)
    r1 = da - d1.astype(F32)
    d2 = r1.astype(BF16)
    d3 = (r1 - d2.astype(F32)).astype(BF16)
    acs = _dot(tri_b, d1) + _dot(tri_b, d2) + _dot(tri_b, d3)
    acs_t = acs.T
    a_last = acs[L - 1:L, :]
    e_acs = jnp.exp(acs)
    e_end = jnp.exp(a_last - acs)
    lane = lax.broadcasted_iota(I32, (L, LANE), 1)
    lo_half = lane < SSM_HEAD_DIM
    lane1 = lax.broadcasted_iota(I32, (1, LANE), 1)

    def pair_cols(m, r):
        return jnp.where(lo_half, m[:, r:r + 1], m[:, r + 1:r + 2])

    heads_per_group = (seq_width // SSM_HEAD_DIM) // SSM_GROUPS
    pairs_per_group = heads_per_group // 2
    pieces = []
    ssq = jnp.zeros((L, 1), F32)
    for g in range(SSM_GROUPS):
        bg = bm[:, g * gw:(g + 1) * gw]
        cg = cm[:, g * gw:(g + 1) * gw]
        bg_b = bg.astype(BF16)
        cg_b = cg.astype(BF16)
        bgt_b = bg.T.astype(BF16)
        cb = _dot_nt(cg_b, bg_b)
        for jp in range(pairs_per_group):
            r = g * heads_per_group + 2 * jp
            col0 = (g * pairs_per_group + jp) * LANE
            xs = xc[:, col0:col0 + LANE]
            xdt = (xs * pair_cols(dt, r)).astype(BF16)
            ys = []
            for rr in (r, r + 1):
                seg = acs[:, rr:rr + 1] - acs_t[rr:rr + 1, :]
                dec = jnp.exp(jnp.where(tri, seg, -jnp.inf))
                ys.append(_dot((cb * dec).astype(BF16), xdt))
            y = jnp.where(lo_half, ys[0], ys[1])
            hcol = jp * LANE
            h_prev = h_s[g, :, hcol:hcol + LANE]
            y = y + _dot(cg_b, h_prev.astype(BF16)) * pair_cols(e_acs, r)
            xw = (xs * pair_cols(dt * e_end, r)).astype(BF16)
            st = _dot(bgt_b, xw)
            cd = jnp.where(lane1 < SSM_HEAD_DIM, jnp.exp(a_last[:, r:r + 1]),
                           jnp.exp(a_last[:, r + 1:r + 2]))
            h_s[g, :, hcol:hcol + LANE] = h_prev * cd + st
            y = y + xs * dsk_ref[:, col0:col0 + LANE]
            zz = z_ref[0, :, col0:col0 + LANE].astype(F32)
            y = y * (zz * jax.nn.sigmoid(zz))
            ssq = ssq + jnp.sum(y * y, axis=-1, keepdims=True)
            pieces.append(y)
    scale = lax.rsqrt(ssq / seq_width + EPS)
    for k, y in enumerate(pieces):
        o_ref[0, :, k * LANE:(k + 1) * LANE] = (y * scale * on_ref[:, k * LANE:(k + 1) * LANE]).astype(BF16)


def _ssd(xbc, z, dt, cw, cb, dtb, alog, dsk, onorm):
    b, s, cdim = xbc.shape
    w = z.shape[2]
    L = SSD_CHUNK
    full = lambda a: pl.BlockSpec(a.shape, lambda bi, i: (0,) * a.ndim)
    blk = lambda width: pl.BlockSpec((1, L, width), lambda bi, i: (bi, i, 0))
    return pl.pallas_call(
        functools.partial(_ssd_kernel, seq_width=w),
        grid=(b, s // L),
        in_specs=[blk(cdim), blk(w), blk(LANE), full(cw), full(cb), full(dtb), full(alog),
                  full(dsk), full(onorm)],
        out_specs=blk(w),
        out_shape=jax.ShapeDtypeStruct((b, s, w), BF16),
        scratch_shapes=[
            pltpu.VMEM((SUBLANE, cdim), F32),
            pltpu.VMEM((SUBLANE + L, cdim), F32),
            pltpu.VMEM((SSM_GROUPS, SSM_STATE, w // SSM_GROUPS), F32),
        ],
        compiler_params=_cparams(("arbitrary", "arbitrary")),
    )(xbc, z, dt, cw, cb, dtb, alog, dsk, onorm)


def _top_rows(vals, k, val_out, idx_out, payload=None):
    rows = lax.broadcasted_iota(I32, vals.shape, 0)
    big = vals.shape[0]
    for j in range(k):
        m = jnp.max(_fold_rows(vals, jnp.maximum), axis=0, keepdims=True)
        am = jnp.min(_fold_rows(jnp.where(vals == m, rows, big), jnp.minimum), axis=0, keepdims=True)
        hit = rows == am
        val_out[j:j + 1, :] = m
        if payload is None:
            idx_out[j:j + 1, :] = am
        else:
            idx_out[j:j + 1, :] = jnp.sum(_fold_rows(jnp.where(hit, payload, 0), jnp.add),
                                          axis=0, keepdims=True)
        vals = jnp.where(hit, -jnp.inf, vals)


def _route_kernel(x_ref, g_ref, wqt_ref, sk_ref, hn_ref, e_ref, gate_ref,
                  ts_s, ti_s, bs_s, et_s, gt_s, cs_s, ci_s):
    T = ROUTE_TOKENS
    hn = _rms(x_ref[...], g_ref[...])
    hn_ref[...] = hn
    hb = hn.astype(BF16)
    qt = _dot_nt(wqt_ref[...], hb)
    half = qt.shape[0] // (PEER_HEADS * 2)
    for hd in range(PEER_HEADS):
        for side in range(2):
            r0 = (hd * 2 + side) * half
            sc = _dot(sk_ref[side], qt[r0:r0 + half].astype(BF16))
            _top_rows(sc, PEER_TOPK, ts_s.at[side], ti_s.at[side])
        off = 0
        for a in range(PEER_TOPK):
            nb = PEER_TOPK // (a + 1)
            cs_s[off:off + nb, :] = ts_s[0, a:a + 1, :] + ts_s[1, 0:nb, :]
            ci_s[off:off + nb, :] = ti_s[0, a:a + 1, :] * PEER_KEYS + ti_s[1, 0:nb, :]
            off += nb
        cs_s[off:, :] = jnp.full((cs_s.shape[0] - off, T), -jnp.inf, F32)
        ci_s[off:, :] = jnp.zeros((ci_s.shape[0] - off, T), I32)
        _top_rows(cs_s[...], PEER_TOPK, bs_s, et_s.at[pl.ds(hd * PEER_TOPK, PEER_TOPK)],
                  payload=ci_s[...])
        best = bs_s[...]
        p = jnp.exp(best - jnp.max(best, axis=0, keepdims=True))
        gt_s[hd * PEER_TOPK:(hd + 1) * PEER_TOPK, :] = p / jnp.sum(p, axis=0, keepdims=True)
    e_ref[...] = et_s[...].T
    gate_ref[...] = gt_s[...].T


def _route(x2, g, wqt, sk):
    n, d = x2.shape
    tt = ROUTE_TOKENS
    npair = PEER_HEADS * PEER_TOPK
    ncand = sum(PEER_TOPK // (a + 1) for a in range(PEER_TOPK))
    ncand = -(-ncand // SUBLANE) * SUBLANE
    full = lambda a: pl.BlockSpec(a.shape, lambda i: (0,) * a.ndim)
    row = lambda w: pl.BlockSpec((tt, w), lambda i: (i, 0))
    return pl.pallas_call(
        _route_kernel,
        grid=(n // tt,),
        in_specs=[row(d), full(g), full(wqt), full(sk)],
        out_specs=[row(d), row(npair), row(npair)],
        out_shape=[jax.ShapeDtypeStruct((n, d), F32),
                   jax.ShapeDtypeStruct((n, npair), I32),
                   jax.ShapeDtypeStruct((n, npair), F32)],
        scratch_shapes=[
            pltpu.VMEM((2, PEER_TOPK, tt), F32),
            pltpu.VMEM((2, PEER_TOPK, tt), I32),
            pltpu.VMEM((PEER_TOPK, tt), F32),
            pltpu.VMEM((npair, tt), I32),
            pltpu.VMEM((npair, tt), F32),
            pltpu.VMEM((ncand, tt), F32),
            pltpu.VMEM((ncand, tt), I32),
        ],
        compiler_params=_cparams(("arbitrary",)),
    )(x2, g, wqt, sk)


def _expert_kernel(idx_ref, idxn_ref, gate_ref, hn_ref, x_ref, fn_ref, tab_ref, o_ref, buf, sem,
                   *, final_norm):
    T = GATHER_TOKENS
    G = GATHER_GROUP
    npair = idx_ref.shape[1]
    ngroups = T // G
    rows = npair * SUBLANE

    assert ngroups == GATHER_RING and GATHER_AHEAD < ngroups
    step = pl.program_id(0)
    half = npair // 2

    def issue(iref, g, j, part):
        for p in range(part * half, (part + 1) * half):
            e = iref[g * G + j, p]
            pltpu.make_async_copy(tab_ref.at[e], buf.at[g * G + j, p], sem.at[g]).start(priority=p % 2)

    def wait_group(g):
        for j in range(G):
            pltpu.make_async_copy(tab_ref.at[pl.ds(0, npair)], buf.at[g * G + j], sem.at[g]).wait()

    @pl.when(step == 0)
    def _():
        for g in range(GATHER_AHEAD):
            for j in range(G):
                issue(idx_ref, g, j, 0)
                issue(idx_ref, g, j, 1)

    sel = (lax.shift_right_logical(lax.broadcasted_iota(I32, (rows, npair), 0), 3)
           == lax.broadcasted_iota(I32, (rows, npair), 1)).astype(BF16)
    sel_t = (lax.broadcasted_iota(I32, (npair, rows), 0)
             == lax.shift_right_logical(lax.broadcasted_iota(I32, (npair, rows), 1), 3)).astype(BF16)
    diag = (lax.broadcasted_iota(I32, (SUBLANE, rows), 0)
            == (lax.broadcasted_iota(I32, (SUBLANE, rows), 1) & (SUBLANE - 1)))

    for g in range(ngroups):
        nxt = g + GATHER_AHEAD
        nref, ng = (idx_ref, nxt) if nxt < ngroups else (idxn_ref, nxt - ngroups)
        t0 = g * G
        wait_group(g)
        parts = []
        for j in range(G):
            issue(nref, ng, j, 0)
            w = buf[g * G + j].reshape(rows, LANE)
            mu = pltpu.bitcast(lax.shift_left(w, jnp.uint32(16)), F32).astype(BF16)
            gj = _dot_nt(hn_ref[t0 + j].astype(BF16), mu)
            parts.append(jnp.where(diag, gj, 0.0))
        dg = _dot(jnp.concatenate(parts, axis=0).astype(BF16), sel)
        dots = jnp.concatenate(
            [jnp.sum(dg[j * SUBLANE:(j + 1) * SUBLANE], axis=0, keepdims=True) for j in range(G)], axis=0)
        gelu = 0.5 * dots * (1.0 + lax.erf(dots * (2.0 ** -0.5)))
        act = (gelu * gate_ref[t0:t0 + G, :]).astype(BF16)
        arep = _dot(act, sel_t)
        ys = []
        for j in range(G):
            issue(nref, ng, j, 1)
            w = buf[g * G + j].reshape(rows, LANE)
            mv = pltpu.bitcast(w & jnp.uint32(0xFFFF0000), F32).astype(BF16)
            aexp = jnp.where(diag, arep[j:j + 1, :], 0.0).astype(BF16)
            y = _dot(aexp, mv) + x_ref[t0 + j]
            if final_norm:
                ssq = jnp.sum(jnp.sum(y * y, axis=1, keepdims=True), axis=0, keepdims=True)
                y = y * lax.rsqrt(ssq / (SUBLANE * LANE) + EPS) * fn_ref[...]
            ys.append(y)
        for j in range(G):
            o_ref[t0 + j] = ys[j]

    @pl.when(step == pl.num_programs(0) - 1)
    def _():
        for g in range(GATHER_AHEAD):
            wait_group(g)


def _experts(idx, gate, hn3, x3, fnorm3, table3, final_norm):
    n = x3.shape[0]
    tt = GATHER_TOKENS
    npair = idx.shape[1]
    tile = lambda: pl.BlockSpec((tt, SUBLANE, LANE), lambda i: (i, 0, 0))
    nsteps = n // tt
    return pl.pallas_call(
        functools.partial(_expert_kernel, final_norm=final_norm),
        grid=(nsteps,),
        in_specs=[
            pl.BlockSpec((tt, npair), lambda i: (i, 0), memory_space=pltpu.SMEM),
            pl.BlockSpec((tt, npair), lambda i: (jnp.minimum(i + 1, nsteps - 1), 0),
                         memory_space=pltpu.SMEM),
            pl.BlockSpec((tt, npair), lambda i: (i, 0)),
            tile(), tile(),
            pl.BlockSpec((SUBLANE, LANE), lambda i: (0, 0)),
            pl.BlockSpec(memory_space=pl.ANY),
        ],
        out_specs=tile(),
        out_shape=jax.ShapeDtypeStruct((n, SUBLANE, LANE), F32),
        scratch_shapes=[
            pltpu.VMEM((GATHER_RING * GATHER_GROUP, npair, SUBLANE, LANE), jnp.uint32),
            pltpu.SemaphoreType.DMA((GATHER_RING,)),
        ],
        compiler_params=_cparams(("arbitrary",)),
    )(idx, idx, gate, hn3, x3, fnorm3, table3)


def _pad_heads(w, heads, dim):
    d = w.shape[0]
    w = w.reshape(d, heads, dim)
    return jnp.pad(w, ((0, 0), (0, 0), (0, LANE - dim))).reshape(d, heads * LANE)


def _pack_table(u, v):
    ub = lax.bitcast_convert_type(u.astype(BF16), jnp.uint16).astype(jnp.uint32)
    vb = lax.bitcast_convert_type(v.astype(BF16), jnp.uint16).astype(jnp.uint32)
    return ub | (vb << 16)


def _split_w(w):
    hi = w.astype(BF16)
    return hi, (w - hi.astype(F32)).astype(BF16)


def kernel(x, mem, mem_norm, rel_bias, mix_norm, ffn_norm, final_norm, w_o, w_mem_kv, a_w_in,
           a_kv_norm, a_w_uk, a_w_uv, b_w_in, b_conv_w, b_conv_b, b_dt_bias, b_a_log, b_d_skip,
           b_out_norm, peer_w_q, peer_sub_keys, peer_u, peer_v):
    b, s, d = x.shape
    n = b * s
    depth = w_o.shape[0]
    mem_width = MEM_HEADS * MEM_HEAD_DIM
    seq_width = w_o.shape[1] - mem_width
    a_heads = seq_width // A_HEAD_DIM
    ssm_heads = seq_width // SSM_HEAD_DIM
    conv_dim = seq_width + 2 * SSM_GROUPS * SSM_STATE
    assert s % ATT_Q == 0 and s % PROJ_TOKENS == 0 and n % ROUTE_TOKENS == 0 and s >= 4 * IDX_TOPK
    assert d == SUBLANE * LANE and n % GATHER_TOKENS == 0

    wk, wv = w_mem_kv[:, :, :mem_width], w_mem_kv[:, :, mem_width:]
    w_kv_pad = jnp.concatenate(
        [jnp.stack([_pad_heads(wk[l], MEM_HEADS, MEM_HEAD_DIM) for l in range(depth)]),
         jnp.stack([_pad_heads(wv[l], MEM_HEADS, MEM_HEAD_DIM) for l in range(depth)])],
        axis=-1).astype(BF16)
    kv_all = _mem_kv(mem, mem_norm, w_kv_pad)

    x2 = x.reshape(n, d)
    for i in range(depth):
        j = i // 2
        woa = w_o[i, :seq_width].astype(BF16)
        wob = jnp.pad(w_o[i, seq_width:].reshape(MEM_HEADS, MEM_HEAD_DIM, d),
                      ((0, 0), (0, LANE - MEM_HEAD_DIM), (0, 0))).astype(BF16)
        g_mix = mix_norm[i].reshape(1, d)
        if i % 2 == 0:
            w_in = a_w_in[j]
            o0 = seq_width
            o1 = o0 + A_KV_RANK
            o2 = o1 + IDX_HEADS * IDX_DIM
            o3 = o2 + IDX_DIM
            o4 = o3 + IDX_HEADS
            wq = w_in[:, :o0].astype(BF16)
            wc = w_in[:, o0:o1].astype(BF16)
            wm = _pad_heads(w_in[:, o4:], MEM_HEADS, MEM_HEAD_DIM).astype(BF16)
            w_idx = jnp.concatenate(
                [w_in[:, o1:o2], w_in[:, o2:o3], w_in[:, o2:o3],
                 jnp.pad(w_in[:, o3:o4], ((0, 0), (0, LANE - IDX_HEADS)))], axis=1)
            wih, wil = _split_w(w_idx)
            q, c, ct, iq, ka, iw, qm = _in_a(x2, g_mix, wq, wc, wm, wih, wil,
                                             a_kv_norm[j].reshape(1, A_KV_RANK))
            bias = _bias_tiles(rel_bias)
            r3 = lambda t: t.reshape(b, s, t.shape[-1])
            seq = _dsa(r3(q), r3(iq), r3(iw), r3(ka), r3(c),
                       ct.reshape(b, s // ATT_Q, A_KV_RANK, ATT_Q),
                       a_w_uk[j].astype(BF16), jnp.swapaxes(a_w_uv[j], 1, 2).astype(BF16), bias)
        else:
            w_in = b_w_in[j]
            o0 = seq_width
            o1 = o0 + conv_dim
            o2 = o1 + ssm_heads
            wz = w_in[:, :o0].astype(BF16)
            wx = w_in[:, o0:o1].astype(BF16)
            wm = _pad_heads(w_in[:, o2:], MEM_HEADS, MEM_HEAD_DIM).astype(BF16)
            wdh, wdl = _split_w(jnp.pad(w_in[:, o1:o2], ((0, 0), (0, LANE - ssm_heads))))
            z, xbc, dt, qm = _in_b(x2, g_mix, wz, wx, wm, wdh, wdl)
            padh = lambda t: jnp.pad(t.reshape(1, ssm_heads), ((0, 0), (0, LANE - ssm_heads)))
            seq = _ssd(xbc.reshape(b, s, conv_dim), z.reshape(b, s, seq_width),
                       dt.reshape(b, s, LANE), b_conv_w[j], b_conv_b[j].reshape(1, conv_dim),
                       padh(b_dt_bias[j]), padh(b_a_log[j]),
                       jnp.repeat(b_d_skip[j], SSM_HEAD_DIM).reshape(1, seq_width),
                       b_out_norm[j].reshape(1, seq_width))
        x2 = _out_proj(x2, seq.reshape(n, seq_width), qm, kv_all[i], woa, wob, s)

        wqt = jnp.transpose(peer_w_q[i]).astype(BF16)
        hn, eidx, gate = _route(x2, ffn_norm[i].reshape(1, d), wqt, peer_sub_keys[i].astype(BF16))
        tile3 = lambda t: t.reshape(t.shape[0], SUBLANE, LANE)
        table3 = tile3(_pack_table(peer_u[i], peer_v[i]))
        x2 = _experts(eidx, gate, tile3(hn), tile3(x2), final_norm.reshape(SUBLANE, LANE), table3,
                      i == depth - 1).reshape(n, d)
    return x2.reshape(b, s, d)
```

```python
import functools
import math

import jax
import jax.numpy as jnp
from jax import lax
from jax.experimental import pallas as pl
from jax.experimental.pallas import tpu as pltpu

F32 = jnp.float32
BF16 = jnp.bfloat16
I32 = jnp.int32

EPS = 1e-6
MEM_HEADS = 4
MEM_HEAD_DIM = 64
A_HEAD_DIM = 64
A_KV_RANK = 256
IDX_HEADS = 8
IDX_DIM = 64
IDX_TOPK = 256
REL_BUCKETS = 32
REL_MAX_DIST = 128
SSM_HEAD_DIM = 64
SSM_GROUPS = 2
SSM_STATE = 128
CONV_WIDTH = 4
SSD_CHUNK = 128
PEER_HEADS = 8
PEER_KEYS = 128
PEER_TOPK = 16

LANE = 128
SUBLANE = 8
INT_MIN = -(2 ** 31)
NEG_BIG = -1e30

PROJ_TOKENS = 512
ATT_Q = 256
ATT_HEADS = 4
ROUTE_TOKENS = 256
GATHER_GROUP = 8
GATHER_RING = 4
GATHER_AHEAD = 2
GATHER_TOKENS = GATHER_GROUP * GATHER_RING
VMEM_LIMIT = 56 * 1024 * 1024


def _cparams(sem):
    return pltpu.CompilerParams(dimension_semantics=sem, vmem_limit_bytes=VMEM_LIMIT)


def _fold_rows(x, op):
    parts = [x[k * SUBLANE:(k + 1) * SUBLANE] for k in range(x.shape[0] // SUBLANE)]
    while len(parts) > 1:
        nxt = [op(parts[k], parts[k + 1]) for k in range(0, len(parts) - 1, 2)]
        if len(parts) % 2:
            nxt.append(parts[-1])
        parts = nxt
    return parts[0]


def _rms(x, g):
    return x * lax.rsqrt(jnp.mean(x * x, axis=-1, keepdims=True) + EPS) * g


def _split2(a):
    hi = a.astype(BF16)
    lo = (a - hi.astype(F32)).astype(BF16)
    return hi, lo


def _dot(a, b):
    return jnp.dot(a, b, preferred_element_type=F32)


def _dot_nt(a, b):
    return lax.dot_general(a, b, (((1,), (1,)), ((), ())), preferred_element_type=F32)


def _mem_kv_kernel(mem_ref, g_ref, w_ref, out_ref):
    y = _rms(mem_ref[0], g_ref[...])
    out_ref[0, 0] = _dot(y.astype(BF16), w_ref[0]).astype(BF16)


def _mem_kv(mem, mem_norm, w_pad):
    b, m, d = mem.shape
    depth, _, wcols = w_pad.shape
    return pl.pallas_call(
        _mem_kv_kernel,
        grid=(depth, b),
        in_specs=[
            pl.BlockSpec((1, m, d), lambda l, i: (i, 0, 0)),
            pl.BlockSpec((1, d), lambda l, i: (0, 0)),
            pl.BlockSpec((1, d, wcols), lambda l, i: (l, 0, 0)),
        ],
        out_specs=pl.BlockSpec((1, 1, m, wcols), lambda l, i: (l, i, 0, 0)),
        out_shape=jax.ShapeDtypeStruct((depth, b, m, wcols), BF16),
        compiler_params=_cparams(("arbitrary", "arbitrary")),
    )(mem, mem_norm.reshape(1, d), w_pad)


def _bias_kernel(rb_ref, out_ref):
    h = pl.program_id(0)
    max_exact = REL_BUCKETS // 2
    far = rb_ref[REL_BUCKETS - 1, h]
    krow = lax.broadcasted_iota(I32, (ATT_Q, ATT_Q), 0)
    qcol = lax.broadcasted_iota(I32, (ATT_Q, ATT_Q), 1)
    for r in range(2):
        dist = qcol - krow + ATT_Q * r
        n = jnp.maximum(dist, 0)
        nf = jnp.maximum(n, max_exact).astype(F32)
        large = max_exact + (jnp.log(nf / max_exact) / math.log(REL_MAX_DIST / max_exact)
                             * (REL_BUCKETS - max_exact)).astype(I32)
        large = jnp.minimum(large, REL_BUCKETS - 1)
        bucket = jnp.where(n < max_exact, n, large)
        acc = jnp.zeros((ATT_Q, ATT_Q), F32)
        for k in range(REL_BUCKETS):
            acc = jnp.where(bucket == k, rb_ref[k, h], acc)
        out_ref[0, r] = acc - far


def _bias_tiles(rel_bias):
    heads = rel_bias.shape[1]
    return pl.pallas_call(
        _bias_kernel,
        grid=(heads,),
        in_specs=[pl.BlockSpec(memory_space=pltpu.SMEM)],
        out_specs=pl.BlockSpec((1, 2, ATT_Q, ATT_Q), lambda h: (h, 0, 0, 0)),
        out_shape=jax.ShapeDtypeStruct((heads, 2, ATT_Q, ATT_Q), F32),
        compiler_params=_cparams(("arbitrary",)),
    )(rel_bias)


def _in_a_kernel(x_ref, g_ref, wq_ref, wc_ref, wm_ref, wih_ref, wil_ref, kvn_ref,
                 q_ref, c_ref, ct_ref, iq_ref, ka_ref, iw_ref, qm_ref):
    h = _rms(x_ref[...], g_ref[...])
    h_hi, h_lo = _split2(h)
    q_ref[...] = _dot(h_hi, wq_ref[...]).astype(BF16)
    qm_ref[...] = _dot(h_hi, wm_ref[...]).astype(BF16)
    c = _rms(_dot(h_hi, wc_ref[...]), kvn_ref[...])
    c_ref[...] = c.astype(BF16)
    for j in range(PROJ_TOKENS // ATT_Q):
        ct_ref[j] = c[j * ATT_Q:(j + 1) * ATT_Q].T.astype(BF16)
    wih = wih_ref[...]
    ii = _dot(h_hi, wih) + _dot(h_lo, wih) + _dot(h_hi, wil_ref[...])
    iq_ref[...] = ii[:, :IDX_HEADS * IDX_DIM]
    kk = ii[:, IDX_HEADS * IDX_DIM:IDX_HEADS * IDX_DIM + LANE]
    kk_hi, kk_lo = _split2(kk)
    lane = lax.broadcasted_iota(I32, kk.shape, 1)
    half = jnp.where(lane < IDX_DIM, kk_hi, kk_lo)
    ka_ref[...] = jnp.concatenate([half, half], axis=1)
    iw_ref[...] = ii[:, IDX_HEADS * IDX_DIM + LANE:]


def _in_a(x2, g, wq, wc, wm, wih, wil, kvn):
    n, d = x2.shape
    tt = PROJ_TOKENS
    full = lambda a: pl.BlockSpec(a.shape, lambda i: (0,) * a.ndim)
    row = lambda w: pl.BlockSpec((tt, w), lambda i: (i, 0))
    nblk = tt // ATT_Q
    outs = [
        jax.ShapeDtypeStruct((n, wq.shape[1]), BF16),
        jax.ShapeDtypeStruct((n, A_KV_RANK), BF16),
        jax.ShapeDtypeStruct((n // ATT_Q, A_KV_RANK, ATT_Q), BF16),
        jax.ShapeDtypeStruct((n, IDX_HEADS * IDX_DIM), F32),
        jax.ShapeDtypeStruct((n, 2 * LANE), BF16),
        jax.ShapeDtypeStruct((n, LANE), F32),
        jax.ShapeDtypeStruct((n, wm.shape[1]), BF16),
    ]
    out_specs = [
        row(wq.shape[1]), row(A_KV_RANK),
        pl.BlockSpec((nblk, A_KV_RANK, ATT_Q), lambda i: (i, 0, 0)),
        row(IDX_HEADS * IDX_DIM), row(2 * LANE), row(LANE), row(wm.shape[1]),
    ]
    return pl.pallas_call(
        _in_a_kernel,
        grid=(n // tt,),
        in_specs=[row(d), full(g), full(wq), full(wc), full(wm), full(wih), full(wil), full(kvn)],
        out_specs=out_specs,
        out_shape=outs,
        compiler_params=_cparams(("arbitrary",)),
    )(x2, g, wq, wc, wm, wih, wil, kvn)


def _dsa_kernel(q_ref, iq_ref, iw_ref, ka_ref, c_ref, ct_ref, wuk_ref, wuvt_ref, bias_ref,
                o_ref, key_s, lg_s, qbt_s, qt_s, olat_s, ot_s, *, heads):
    i = pl.program_id(1)
    nch = i + 1
    t0 = i * ATT_Q
    Q = ATT_Q
    krow = lax.broadcasted_iota(I32, (Q, Q), 0)
    qcol = lax.broadcasted_iota(I32, (Q, Q), 1)

    iqv = iq_ref[0]
    lane = lax.broadcasted_iota(I32, (Q, LANE), 1)
    for j in range(IDX_HEADS // 2):
        v = iqv[:, j * LANE:(j + 1) * LANE]
        r = pltpu.roll(v, IDX_DIM, 1)
        for hh, dup in ((2 * j, jnp.where(lane < IDX_DIM, v, r)),
                        (2 * j + 1, jnp.where(lane < IDX_DIM, r, v))):
            hi = dup.astype(BF16)
            lo = (dup - hi.astype(F32)).astype(BF16)
            qbt_s[hh, 0:LANE, :] = hi.astype(F32).T.astype(BF16)
            qbt_s[hh, LANE:2 * LANE, :] = lo.astype(F32).T.astype(BF16)
    wt = iw_ref[0].T * (IDX_HEADS ** -0.5)
    qt_s[...] = q_ref[0].astype(F32).T.astype(BF16)

    def score_chunk(c, carry):
        ka = ka_ref[0, pl.ds(pl.multiple_of(c * Q, Q), Q), :]
        acc = jnp.zeros((Q, Q), F32)
        for hh in range(IDX_HEADS):
            z = _dot(ka, qbt_s[hh])
            acc = acc + jnp.maximum(z, 0.0) * wt[hh:hh + 1, :]
        acc = acc * (IDX_DIM ** -0.5)
        bits = pltpu.bitcast(acc, I32)
        skey = jnp.where(bits < 0, bits ^ 0x7FFFFFFF, bits)
        causal = (krow + c * Q) <= (qcol + t0)
        key_s[pl.ds(pl.multiple_of(c * Q, Q), Q), :] = jnp.where(causal, skey, INT_MIN)
        return carry

    lax.fori_loop(0, nch, score_chunk, 0)

    def count_ge(thr):
        def body(c, acc):
            blk = key_s[pl.ds(pl.multiple_of(c * Q, Q), Q), :]
            return acc + _fold_rows((blk >= thr).astype(I32), jnp.add)
        acc = lax.fori_loop(0, nch, body, jnp.zeros((SUBLANE, Q), I32))
        return jnp.sum(acc, axis=0, keepdims=True)

    c0 = count_ge(jnp.zeros((1, Q), I32))
    has_k = c0 >= IDX_TOPK
    thr = jnp.where(has_k, 0, INT_MIN).astype(I32)
    cnt = jnp.where(has_k, c0, nch * Q)
    reachable = (lax.broadcasted_iota(I32, (1, Q), 1) + t0 + 1) >= IDX_TOPK

    def unsettled(cnt):
        return jnp.max(jnp.where((cnt != IDX_TOPK) & reachable, 1, 0)) > 0

    def bisect(it, state):
        thr, cnt = state
        cand = thr + lax.shift_left(jnp.int32(1), 30 - it)
        c = count_ge(cand)
        ok = c >= IDX_TOPK
        return jnp.where(ok, cand, thr), jnp.where(ok, c, cnt)

    thr, cnt = lax.fori_loop(0, 31, bisect, (thr, cnt))
    thr = jnp.maximum(thr, INT_MIN + 1)

    nbits = max(1, int(math.ceil(math.log2(ka_ref.shape[1] + 1))))

    def tie_cut(thr):
        need = IDX_TOPK - count_ge(thr + 1)

        def count_tie_below(bound):
            def body(c, acc):
                blk = key_s[pl.ds(pl.multiple_of(c * Q, Q), Q), :]
                hit = (blk == thr) & ((krow + c * Q) < bound)
                return acc + _fold_rows(hit.astype(I32), jnp.add)
            acc = lax.fori_loop(0, nch, body, jnp.zeros((SUBLANE, Q), I32))
            return jnp.sum(acc, axis=0, keepdims=True)

        def tie_bisect(it, p0):
            cand = p0 + lax.shift_left(jnp.int32(1), nbits - 1 - it)
            return jnp.where(count_tie_below(cand) < need, cand, p0)

        return lax.fori_loop(0, nbits, tie_bisect, jnp.zeros((1, Q), I32)) + 1

    pcut = lax.cond(unsettled(cnt), tie_cut, lambda thr: jnp.full((1, Q), 2 ** nbits, I32), thr)

    def mask_chunk(c, carry):
        off = pl.multiple_of(c * Q, Q)
        key = key_s[pl.ds(off, Q), :]
        sel = (key > thr) | ((key == thr) & ((krow + c * Q) < pcut))
        key_s[pl.ds(off, Q), :] = pltpu.bitcast(jnp.where(sel, 0.0, NEG_BIG).astype(F32), I32)
        return carry

    lax.fori_loop(0, nch, mask_chunk, 0)

    HB = ATT_HEADS

    def head_group(hg, carry):
        qlts = []
        for hh in range(HB):
            h = hg * HB + hh
            qh = qt_s[pl.ds(pl.multiple_of(h * A_HEAD_DIM, A_HEAD_DIM), A_HEAD_DIM), :]
            qlts.append((_dot(wuk_ref[h], qh) * (A_HEAD_DIM ** -0.5)).astype(BF16))
        qlt = jnp.concatenate(qlts, axis=1)

        def logits_chunk(c, m_acc, band):
            off = pl.multiple_of(c * Q, Q)
            lg = _dot(c_ref[0, pl.ds(off, Q), :], qlt)
            mask = pltpu.bitcast(key_s[pl.ds(off, Q), :], F32)
            new = []
            for hh in range(HB):
                lgh = lg[:, hh * Q:(hh + 1) * Q] + mask
                if band is not None:
                    lgh = lgh + bias_ref[hg * HB + hh, band]
                lg_s[pl.ds(off, Q), hh * Q:(hh + 1) * Q] = lgh
                new.append(jnp.maximum(m_acc[:, hh * Q:(hh + 1) * Q], _fold_rows(lgh, jnp.maximum)))
            return jnp.concatenate(new, axis=1)

        m_acc = jnp.full((SUBLANE, HB * Q), NEG_BIG, F32)
        m_acc = lax.fori_loop(0, nch - 2, lambda c, m: logits_chunk(c, m, None), m_acc)
        m_acc = lax.cond(nch >= 2, lambda m: logits_chunk(nch - 2, m, 1), lambda m: m, m_acc)
        m_acc = logits_chunk(nch - 1, m_acc, 0)
        m = jnp.max(m_acc, axis=0, keepdims=True)

        olat_s[...] = jnp.zeros_like(olat_s)

        def pv_chunk(c, s_acc):
            off = pl.multiple_of(c * Q, Q)
            p = jnp.exp(lg_s[pl.ds(off, Q), :] - m)
            olat_s[...] += _dot(ct_ref[0, c], p.astype(BF16))
            return s_acc + _fold_rows(p, jnp.add)

        s_acc = lax.fori_loop(0, nch, pv_chunk, jnp.zeros((SUBLANE, HB * Q), F32))
        s = jnp.sum(s_acc, axis=0, keepdims=True)
        for hh in range(HB):
            h = hg * HB + hh
            ol = olat_s[:, hh * Q:(hh + 1) * Q].astype(BF16)
            oh = _dot(wuvt_ref[h], ol) / s[:, hh * Q:(hh + 1) * Q]
            ot_s[pl.ds(pl.multiple_of(h * A_HEAD_DIM, A_HEAD_DIM), A_HEAD_DIM), :] = oh
        return carry

    lax.fori_loop(0, heads // HB, head_group, 0)
    o_ref[0] = ot_s[...].T.astype(BF16)


def _dsa(q, iq, iw, ka, c, ct, wuk, wuvt, bias):
    b, s, w = q.shape
    heads = wuk.shape[0]
    nq = s // ATT_Q
    assert heads % ATT_HEADS == 0
    once = pl.Buffered(1)
    full = lambda a: pl.BlockSpec(a.shape, lambda bi, i: (0,) * a.ndim, pipeline_mode=once)
    blk = lambda width: pl.BlockSpec((1, ATT_Q, width), lambda bi, i: (bi, i, 0))
    per_b = lambda width: pl.BlockSpec((1, s, width), lambda bi, i: (bi, 0, 0), pipeline_mode=once)
    return pl.pallas_call(
        functools.partial(_dsa_kernel, heads=heads),
        grid=(b, nq),
        in_specs=[
            blk(w), blk(iq.shape[2]), blk(iw.shape[2]),
            per_b(ka.shape[2]), per_b(c.shape[2]),
            pl.BlockSpec((1, nq, A_KV_RANK, ATT_Q), lambda bi, i: (bi, 0, 0, 0), pipeline_mode=once),
            full(wuk), full(wuvt), full(bias),
        ],
        out_specs=blk(w),
        out_shape=jax.ShapeDtypeStruct((b, s, w), BF16),
        scratch_shapes=[
            pltpu.VMEM((s, ATT_Q), I32),
            pltpu.VMEM((s, ATT_HEADS * ATT_Q), F32),
            pltpu.VMEM((IDX_HEADS, 2 * LANE, ATT_Q), BF16),
            pltpu.VMEM((w, ATT_Q), BF16),
            pltpu.VMEM((A_KV_RANK, ATT_HEADS * ATT_Q), F32),
            pltpu.VMEM((w, ATT_Q), F32),
        ],
        compiler_params=_cparams(("arbitrary", "arbitrary")),
    )(q, iq, iw, ka, c, ct, wuk, wuvt, bias)


def _out_kernel(x_ref, seq_ref, qm_ref, kv_ref, woa_ref, wob_ref, o_ref):
    acc = x_ref[...] + _dot(seq_ref[...], woa_ref[...])
    qm = qm_ref[...]
    kv = kv_ref[0]
    for h in range(MEM_HEADS):
        k = kv[:, h * LANE:(h + 1) * LANE]
        v = kv[:, (MEM_HEADS + h) * LANE:(MEM_HEADS + h + 1) * LANE]
        lg = _dot_nt(qm[:, h * LANE:(h + 1) * LANE], k) * (MEM_HEAD_DIM ** -0.5)
        p = jnp.exp(lg - jnp.max(lg, axis=-1, keepdims=True))
        oh = _dot(p.astype(BF16), v) / jnp.sum(p, axis=-1, keepdims=True)
        acc = acc + _dot(oh.astype(BF16), wob_ref[h])
    o_ref[...] = acc


def _out_proj(x2, seq2, qm2, kv, woa, wob, seq_len):
    n, d = x2.shape
    tt = PROJ_TOKENS
    per_seq = seq_len // tt
    full = lambda a: pl.BlockSpec(a.shape, lambda i: (0,) * a.ndim)
    row = lambda w: pl.BlockSpec((tt, w), lambda i: (i, 0))
    return pl.pallas_call(
        _out_kernel,
        grid=(n // tt,),
        in_specs=[row(d), row(seq2.shape[1]), row(qm2.shape[1]),
                  pl.BlockSpec((1,) + kv.shape[1:], lambda i: (i // per_seq, 0, 0)),
                  full(woa), full(wob)],
        out_specs=row(d),
        out_shape=jax.ShapeDtypeStruct((n, d), F32),
        compiler_params=_cparams(("arbitrary",)),
    )(x2, seq2, qm2, kv, woa, wob)


def _in_b_kernel(x_ref, g_ref, wz_ref, wx_ref, wm_ref, wdh_ref, wdl_ref,
                 z_ref, xbc_ref, dt_ref, qm_ref):
    h = _rms(x_ref[...], g_ref[...])
    h_hi, h_lo = _split2(h)
    z_ref[...] = _dot(h_hi, wz_ref[...]).astype(BF16)
    xbc_ref[...] = _dot(h_hi, wx_ref[...])
    qm_ref[...] = _dot(h_hi, wm_ref[...]).astype(BF16)
    wdh = wdh_ref[...]
    dt_ref[...] = _dot(h_hi, wdh) + _dot(h_lo, wdh) + _dot(h_hi, wdl_ref[...])


def _in_b(x2, g, wz, wx, wm, wdh, wdl):
    n, d = x2.shape
    tt = PROJ_TOKENS
    full = lambda a: pl.BlockSpec(a.shape, lambda i: (0,) * a.ndim)
    row = lambda w: pl.BlockSpec((tt, w), lambda i: (i, 0))
    return pl.pallas_call(
        _in_b_kernel,
        grid=(n // tt,),
        in_specs=[row(d), full(g), full(wz), full(wx), full(wm), full(wdh), full(wdl)],
        out_specs=[row(wz.shape[1]), row(wx.shape[1]), row(LANE), row(wm.shape[1])],
        out_shape=[jax.ShapeDtypeStruct((n, wz.shape[1]), BF16),
                   jax.ShapeDtypeStruct((n, wx.shape[1]), F32),
                   jax.ShapeDtypeStruct((n, LANE), F32),
                   jax.ShapeDtypeStruct((n, wm.shape[1]), BF16)],
        compiler_params=_cparams(("arbitrary",)),
    )(x2, g, wz, wx, wm, wdh, wdl)


def _ssd_kernel(xbc_ref, z_ref, dt_ref, cw_ref, cb_ref, dtb_ref, alog_ref, dsk_ref, on_ref,
                o_ref, tail_s, xpad_s, h_s, *, seq_width):
    L = SSD_CHUNK
    ci = pl.program_id(1)

    @pl.when(ci == 0)
    def _():
        tail_s[...] = jnp.zeros_like(tail_s)
        h_s[...] = jnp.zeros_like(h_s)

    xr = xbc_ref[0]
    xpad_s[0:SUBLANE, :] = tail_s[...]
    xpad_s[SUBLANE:SUBLANE + L, :] = xr
    tail_s[...] = xr[L - SUBLANE:L, :]
    conv = cb_ref[...] + jnp.zeros_like(xr)
    for j in range(CONV_WIDTH):
        off = SUBLANE - (CONV_WIDTH - 1) + j
        conv = conv + cw_ref[j:j + 1, :] * xpad_s[off:off + L, :]
    xc = conv * jax.nn.sigmoid(conv)
    gw = SSM_STATE
    bm = xc[:, seq_width:seq_width + SSM_GROUPS * gw]
    cm = xc[:, seq_width + SSM_GROUPS * gw:seq_width + 2 * SSM_GROUPS * gw]

    dt = jax.nn.softplus(dt_ref[0] + dtb_ref[...])
    a = -jnp.exp(alog_ref[...])
    da = dt * a
    row_i = lax.broadcasted_iota(I32, (L, L), 0)
    col_i = lax.broadcasted_iota(I32, (L, L), 1)
    tri = row_i >= col_i
    tri_b = tri.astype(BF16)
    d1 = da.astype(BF16)
    r1 = da - d1.astype(F32)
    d2 = r1.astype(BF16)
    d3 = (r1 - d2.astype(F32)).astype(BF16)
    acs = _dot(tri_b, d1) + _dot(tri_b, d2) + _dot(tri_b, d3)
    acs_t = acs.T
    a_last = acs[L - 1:L, :]
    e_acs = jnp.exp(acs)
    e_end = jnp.exp(a_last - acs)
    lane = lax.broadcasted_iota(I32, (L, LANE), 1)
    lo_half = lane < SSM_HEAD_DIM
    lane1 = lax.broadcasted_iota(I32, (1, LANE), 1)

    def pair_cols(m, r):
        return jnp.where(lo_half, m[:, r:r + 1], m[:, r + 1:r + 2])

    heads_per_group = (seq_width // SSM_HEAD_DIM) // SSM_GROUPS
    pairs_per_group = heads_per_group // 2
    pieces = []
    ssq = jnp.zeros((L, 1), F32)
    for g in range(SSM_GROUPS):
        bg = bm[:, g * gw:(g + 1) * gw]
        cg = cm[:, g * gw:(g + 1) * gw]
        bg_b = bg.astype(BF16)
        cg_b = cg.astype(BF16)
        bgt_b = bg.T.astype(BF16)
        cb = _dot_nt(cg_b, bg_b)
        for jp in range(pairs_per_group):
            r = g * heads_per_group + 2 * jp
            col0 = (g * pairs_per_group + jp) * LANE
            xs = xc[:, col0:col0 + LANE]
            xdt = (xs * pair_cols(dt, r)).astype(BF16)
            ys = []
            for rr in (r, r + 1):
                seg = acs[:, rr:rr + 1] - acs_t[rr:rr + 1, :]
                dec = jnp.exp(jnp.where(tri, seg, -jnp.inf))
                ys.append(_dot((cb * dec).astype(BF16), xdt))
            y = jnp.where(lo_half, ys[0], ys[1])
            hcol = jp * LANE
            h_prev = h_s[g, :, hcol:hcol + LANE]
            y = y + _dot(cg_b, h_prev.astype(BF16)) * pair_cols(e_acs, r)
            xw = (xs * pair_cols(dt * e_end, r)).astype(BF16)
            st = _dot(bgt_b, xw)
            cd = jnp.where(lane1 < SSM_HEAD_DIM, jnp.exp(a_last[:, r:r + 1]),
                           jnp.exp(a_last[:, r + 1:r + 2]))
            h_s[g, :, hcol:hcol + LANE] = h_prev * cd + st
            y = y + xs * dsk_ref[:, col0:col0 + LANE]
            zz = z_ref[0, :, col0:col0 + LANE].astype(F32)
            y = y * (zz * jax.nn.sigmoid(zz))
            ssq = ssq + jnp.sum(y * y, axis=-1, keepdims=True)
            pieces.append(y)
    scale = lax.rsqrt(ssq / seq_width + EPS)
    for k, y in enumerate(pieces):
        o_ref[0, :, k * LANE:(k + 1) * LANE] = (y * scale * on_ref[:, k * LANE:(k + 1) * LANE]).astype(BF16)


def _ssd(xbc, z, dt, cw, cb, dtb, alog, dsk, onorm):
    b, s, cdim = xbc.shape
    w = z.shape[2]
    L = SSD_CHUNK
    full = lambda a: pl.BlockSpec(a.shape, lambda bi, i: (0,) * a.ndim)
    blk = lambda width: pl.BlockSpec((1, L, width), lambda bi, i: (bi, i, 0))
    return pl.pallas_call(
        functools.partial(_ssd_kernel, seq_width=w),
        grid=(b, s // L),
        in_specs=[blk(cdim), blk(w), blk(LANE), full(cw), full(cb), full(dtb), full(alog),
                  full(dsk), full(onorm)],
        out_specs=blk(w),
        out_shape=jax.ShapeDtypeStruct((b, s, w), BF16),
        scratch_shapes=[
            pltpu.VMEM((SUBLANE, cdim), F32),
            pltpu.VMEM((SUBLANE + L, cdim), F32),
            pltpu.VMEM((SSM_GROUPS, SSM_STATE, w // SSM_GROUPS), F32),
        ],
        compiler_params=_cparams(("arbitrary", "arbitrary")),
    )(xbc, z, dt, cw, cb, dtb, alog, dsk, onorm)


def _top_rows(vals, k, val_out, idx_out, payload=None):
    rows = lax.broadcasted_iota(I32, vals.shape, 0)
    big = vals.shape[0]
    for j in range(k):
        m = jnp.max(_fold_rows(vals, jnp.maximum), axis=0, keepdims=True)
        am = jnp.min(_fold_rows(jnp.where(vals == m, rows, big), jnp.minimum), axis=0, keepdims=True)
        hit = rows == am
        val_out[j:j + 1, :] = m
        if payload is None:
            idx_out[j:j + 1, :] = am
        else:
            idx_out[j:j + 1, :] = jnp.sum(_fold_rows(jnp.where(hit, payload, 0), jnp.add),
                                          axis=0, keepdims=True)
        vals = jnp.where(hit, -jnp.inf, vals)


def _route_kernel(x_ref, g_ref, wqt_ref, sk_ref, hn_ref, e_ref, gate_ref,
                  ts_s, ti_s, bs_s, et_s, gt_s, cs_s, ci_s):
    T = ROUTE_TOKENS
    hn = _rms(x_ref[...], g_ref[...])
    hn_ref[...] = hn
    hb = hn.astype(BF16)
    qt = _dot_nt(wqt_ref[...], hb)
    half = qt.shape[0] // (PEER_HEADS * 2)
    for hd in range(PEER_HEADS):
        for side in range(2):
            r0 = (hd * 2 + side) * half
            sc = _dot(sk_ref[side], qt[r0:r0 + half].astype(BF16))
            _top_rows(sc, PEER_TOPK, ts_s.at[side], ti_s.at[side])
        off = 0
        for a in range(PEER_TOPK):
            nb = PEER_TOPK // (a + 1)
            cs_s[off:off + nb, :] = ts_s[0, a:a + 1, :] + ts_s[1, 0:nb, :]
            ci_s[off:off + nb, :] = ti_s[0, a:a + 1, :] * PEER_KEYS + ti_s[1, 0:nb, :]
            off += nb
        cs_s[off:, :] = jnp.full((cs_s.shape[0] - off, T), -jnp.inf, F32)
        ci_s[off:, :] = jnp.zeros((ci_s.shape[0] - off, T), I32)
        _top_rows(cs_s[...], PEER_TOPK, bs_s, et_s.at[pl.ds(hd * PEER_TOPK, PEER_TOPK)],
                  payload=ci_s[...])
        best = bs_s[...]
        p = jnp.exp(best - jnp.max(best, axis=0, keepdims=True))
        gt_s[hd * PEER_TOPK:(hd + 1) * PEER_TOPK, :] = p / jnp.sum(p, axis=0, keepdims=True)
    e_ref[...] = et_s[...].T
    gate_ref[...] = gt_s[...].T


def _route(x2, g, wqt, sk):
    n, d = x2.shape
    tt = ROUTE_TOKENS
    npair = PEER_HEADS * PEER_TOPK
    ncand = sum(PEER_TOPK // (a + 1) for a in range(PEER_TOPK))
    ncand = -(-ncand // SUBLANE) * SUBLANE
    full = lambda a: pl.BlockSpec(a.shape, lambda i: (0,) * a.ndim)
    row = lambda w: pl.BlockSpec((tt, w), lambda i: (i, 0))
    return pl.pallas_call(
        _route_kernel,
        grid=(n // tt,),
        in_specs=[row(d), full(g), full(wqt), full(sk)],
        out_specs=[row(d), row(npair), row(npair)],
        out_shape=[jax.ShapeDtypeStruct((n, d), F32),
                   jax.ShapeDtypeStruct((n, npair), I32),
                   jax.ShapeDtypeStruct((n, npair), F32)],
        scratch_shapes=[
            pltpu.VMEM((2, PEER_TOPK, tt), F32),
            pltpu.VMEM((2, PEER_TOPK, tt), I32),
            pltpu.VMEM((PEER_TOPK, tt), F32),
            pltpu.VMEM((npair, tt), I32),
            pltpu.VMEM((npair, tt), F32),
            pltpu.VMEM((ncand, tt), F32),
            pltpu.VMEM((ncand, tt), I32),
        ],
        compiler_params=_cparams(("arbitrary",)),
    )(x2, g, wqt, sk)


def _expert_kernel(idx_ref, idxn_ref, gate_ref, hn_ref, x_ref, fn_ref, tab_ref, o_ref, buf, sem,
                   *, final_norm):
    T = GATHER_TOKENS
    G = GATHER_GROUP
    npair = idx_ref.shape[1]
    ngroups = T // G
    rows = npair * SUBLANE

    assert ngroups == GATHER_RING and GATHER_AHEAD < ngroups
    step = pl.program_id(0)
    half = npair // 2

    def issue(iref, g, j, part):
        for p in range(part * half, (part + 1) * half):
            e = iref[g * G + j, p]
            pltpu.make_async_copy(tab_ref.at[e], buf.at[g * G + j, p], sem.at[g]).start(priority=p % 2)

    def wait_group(g):
        for j in range(G):
            pltpu.make_async_copy(tab_ref.at[pl.ds(0, npair)], buf.at[g * G + j], sem.at[g]).wait()

    @pl.when(step == 0)
    def _():
        for g in range(GATHER_AHEAD):
            for j in range(G):
                issue(idx_ref, g, j, 0)
                issue(idx_ref, g, j, 1)

    sel = (lax.shift_right_logical(lax.broadcasted_iota(I32, (rows, npair), 0), 3)
           == lax.broadcasted_iota(I32, (rows, npair), 1)).astype(BF16)
    sel_t = (lax.broadcasted_iota(I32, (npair, rows), 0)
             == lax.shift_right_logical(lax.broadcasted_iota(I32, (npair, rows), 1), 3)).astype(BF16)
    diag = (lax.broadcasted_iota(I32, (SUBLANE, rows), 0)
            == (lax.broadcasted_iota(I32, (SUBLANE, rows), 1) & (SUBLANE - 1)))

    for g in range(ngroups):
        nxt = g + GATHER_AHEAD
        nref, ng = (idx_ref, nxt) if nxt < ngroups else (idxn_ref, nxt - ngroups)
        t0 = g * G
        wait_group(g)
        parts = []
        for j in range(G):
            issue(nref, ng, j, 0)
            w = buf[g * G + j].reshape(rows, LANE)
            mu = pltpu.bitcast(lax.shift_left(w, jnp.uint32(16)), F32).astype(BF16)
            gj = _dot_nt(hn_ref[t0 + j].astype(BF16), mu)
            parts.append(jnp.where(diag, gj, 0.0))
        dg = _dot(jnp.concatenate(parts, axis=0).astype(BF16), sel)
        dots = jnp.concatenate(
            [jnp.sum(dg[j * SUBLANE:(j + 1) * SUBLANE], axis=0, keepdims=True) for j in range(G)], axis=0)
        gelu = 0.5 * dots * (1.0 + lax.erf(dots * (2.0 ** -0.5)))
        act = (gelu * gate_ref[t0:t0 + G, :]).astype(BF16)
        arep = _dot(act, sel_t)
        ys = []
        for j in range(G):
            issue(nref, ng, j, 1)
            w = buf[g * G + j].reshape(rows, LANE)
            mv = pltpu.bitcast(w & jnp.uint32(0xFFFF0000), F32).astype(BF16)
            aexp = jnp.where(diag, arep[j:j + 1, :], 0.0).astype(BF16)
            y = _dot(aexp, mv) + x_ref[t0 + j]
            if final_norm:
                ssq = jnp.sum(jnp.sum(y * y, axis=1, keepdims=True), axis=0, keepdims=True)
                y = y * lax.rsqrt(ssq / (SUBLANE * LANE) + EPS) * fn_ref[...]
            ys.append(y)
        for j in range(G):
            o_ref[t0 + j] = ys[j]

    @pl.when(step == pl.num_programs(0) - 1)
    def _():
        for g in range(GATHER_AHEAD):
            wait_group(g)


def _experts(idx, gate, hn3, x3, fnorm3, table3, final_norm):
    n = x3.shape[0]
    tt = GATHER_TOKENS
    npair = idx.shape[1]
    tile = lambda: pl.BlockSpec((tt, SUBLANE, LANE), lambda i: (i, 0, 0))
    nsteps = n // tt
    return pl.pallas_call(
        functools.partial(_expert_kernel, final_norm=final_norm),
        grid=(nsteps,),
        in_specs=[
            pl.BlockSpec((tt, npair), lambda i: (i, 0), memory_space=pltpu.SMEM),
            pl.BlockSpec((tt, npair), lambda i: (jnp.minimum(i + 1, nsteps - 1), 0),
                         memory_space=pltpu.SMEM),
            pl.BlockSpec((tt, npair), lambda i: (i, 0)),
            tile(), tile(),
            pl.BlockSpec((SUBLANE, LANE), lambda i: (0, 0)),
            pl.BlockSpec(memory_space=pl.ANY),
        ],
        out_specs=tile(),
        out_shape=jax.ShapeDtypeStruct((n, SUBLANE, LANE), F32),
        scratch_shapes=[
            pltpu.VMEM((GATHER_RING * GATHER_GROUP, npair, SUBLANE, LANE), jnp.uint32),
            pltpu.SemaphoreType.DMA((GATHER_RING,)),
        ],
        compiler_params=_cparams(("arbitrary",)),
    )(idx, idx, gate, hn3, x3, fnorm3, table3)


def _pad_heads(w, heads, dim):
    d = w.shape[0]
    w = w.reshape(d, heads, dim)
    return jnp.pad(w, ((0, 0), (0, 0), (0, LANE - dim))).reshape(d, heads * LANE)


def _pack_table(u, v):
    ub = lax.bitcast_convert_type(u.astype(BF16), jnp.uint16).astype(jnp.uint32)
    vb = lax.bitcast_convert_type(v.astype(BF16), jnp.uint16).astype(jnp.uint32)
    return ub | (vb << 16)


def _split_w(w):
    hi = w.astype(BF16)
    return hi, (w - hi.astype(F32)).astype(BF16)


def kernel(x, mem, mem_norm, rel_bias, mix_norm, ffn_norm, final_norm, w_o, w_mem_kv, a_w_in,
           a_kv_norm, a_w_uk, a_w_uv, b_w_in, b_conv_w, b_conv_b, b_dt_bias, b_a_log, b_d_skip,
           b_out_norm, peer_w_q, peer_sub_keys, peer_u, peer_v):
    b, s, d = x.shape
    n = b * s
    depth = w_o.shape[0]
    mem_width = MEM_HEADS * MEM_HEAD_DIM
    seq_width = w_o.shape[1] - mem_width
    a_heads = seq_width // A_HEAD_DIM
    ssm_heads = seq_width // SSM_HEAD_DIM
    conv_dim = seq_width + 2 * SSM_GROUPS * SSM_STATE
    assert s % ATT_Q == 0 and s % PROJ_TOKENS == 0 and n % ROUTE_TOKENS == 0 and s >= 4 * IDX_TOPK
    assert d == SUBLANE * LANE and n % GATHER_TOKENS == 0

    wk, wv = w_mem_kv[:, :, :mem_width], w_mem_kv[:, :, mem_width:]
    w_kv_pad = jnp.concatenate(
        [jnp.stack([_pad_heads(wk[l], MEM_HEADS, MEM_HEAD_DIM) for l in range(depth)]),
         jnp.stack([_pad_heads(wv[l], MEM_HEADS, MEM_HEAD_DIM) for l in range(depth)])],
        axis=-1).astype(BF16)
    kv_all = _mem_kv(mem, mem_norm, w_kv_pad)

    x2 = x.reshape(n, d)
    for i in range(depth):
        j = i // 2
        woa = w_o[i, :seq_width].astype(BF16)
        wob = jnp.pad(w_o[i, seq_width:].reshape(MEM_HEADS, MEM_HEAD_DIM, d),
                      ((0, 0), (0, LANE - MEM_HEAD_DIM), (0, 0))).astype(BF16)
        g_mix = mix_norm[i].reshape(1, d)
        if i % 2 == 0:
            w_in = a_w_in[j]
            o0 = seq_width
            o1 = o0 + A_KV_RANK
            o2 = o1 + IDX_HEADS * IDX_DIM
            o3 = o2 + IDX_DIM
            o4 = o3 + IDX_HEADS
            wq = w_in[:, :o0].astype(BF16)
            wc = w_in[:, o0:o1].astype(BF16)
            wm = _pad_heads(w_in[:, o4:], MEM_HEADS, MEM_HEAD_DIM).astype(BF16)
            w_idx = jnp.concatenate(
                [w_in[:, o1:o2], w_in[:, o2:o3], w_in[:, o2:o3],
                 jnp.pad(w_in[:, o3:o4], ((0, 0), (0, LANE - IDX_HEADS)))], axis=1)
            wih, wil = _split_w(w_idx)
            q, c, ct, iq, ka, iw, qm = _in_a(x2, g_mix, wq, wc, wm, wih, wil,
                                             a_kv_norm[j].reshape(1, A_KV_RANK))
            bias = _bias_tiles(rel_bias)
            r3 = lambda t: t.reshape(b, s, t.shape[-1])
            seq = _dsa(r3(q), r3(iq), r3(iw), r3(ka), r3(c),
                       ct.reshape(b, s // ATT_Q, A_KV_RANK, ATT_Q),
                       a_w_uk[j].astype(BF16), jnp.swapaxes(a_w_uv[j], 1, 2).astype(BF16), bias)
        else:
            w_in = b_w_in[j]
            o0 = seq_width
            o1 = o0 + conv_dim
            o2 = o1 + ssm_heads
            wz = w_in[:, :o0].astype(BF16)
            wx = w_in[:, o0:o1].astype(BF16)
            wm = _pad_heads(w_in[:, o2:], MEM_HEADS, MEM_HEAD_DIM).astype(BF16)
            wdh, wdl = _split_w(jnp.pad(w_in[:, o1:o2], ((0, 0), (0, LANE - ssm_heads))))
            z, xbc, dt, qm = _in_b(x2, g_mix, wz, wx, wm, wdh, wdl)
            padh = lambda t: jnp.pad(t.reshape(1, ssm_heads), ((0, 0), (0, LANE - ssm_heads)))
            seq = _ssd(xbc.reshape(b, s, conv_dim), z.reshape(b, s, seq_width),
                       dt.reshape(b, s, LANE), b_conv_w[j], b_conv_b[j].reshape(1, conv_dim),
                       padh(b_dt_bias[j]), padh(b_a_log[j]),
                       jnp.repeat(b_d_skip[j], SSM_HEAD_DIM).reshape(1, seq_width),
                       b_out_norm[j].reshape(1, seq_width))
        x2 = _out_proj(x2, seq.reshape(n, seq_width), qm, kv_all[i], woa, wob, s)

        wqt = jnp.transpose(peer_w_q[i]).astype(BF16)
        hn, eidx, gate = _route(x2, ffn_norm[i].reshape(1, d), wqt, peer_sub_keys[i].astype(BF16))
        tile3 = lambda t: t.reshape(t.shape[0], SUBLANE, LANE)
        table3 = tile3(_pack_table(peer_u[i], peer_v[i]))
        x2 = _experts(eidx, gate, tile3(hn), tile3(x2), final_norm.reshape(SUBLANE, LANE), table3,
                      i == depth - 1).reshape(n, d)
    return x2.reshape(b, s, d)
```

```python
import functools
import math

import jax
import jax.numpy as jnp
from jax import lax
from jax.experimental import pallas as pl
from jax.experimental.pallas import tpu as pltpu
from jax.experimental.pallas import tpu_sc as plsc

F32 = jnp.float32
BF16 = jnp.bfloat16
I32 = jnp.int32

EPS = 1e-6
MEM_HEADS = 4
MEM_HEAD_DIM = 64
A_HEAD_DIM = 64
A_KV_RANK = 256
IDX_HEADS = 8
IDX_DIM = 64
IDX_TOPK = 256
REL_BUCKETS = 32
REL_MAX_DIST = 128
SSM_HEAD_DIM = 64
SSM_GROUPS = 2
SSM_STATE = 128
CONV_WIDTH = 4
SSD_CHUNK = 128
PEER_HEADS = 8
PEER_KEYS = 128
PEER_TOPK = 16

LANE = 128
SUBLANE = 8
INT_MIN = -(2 ** 31)
NEG_BIG = -1e30

PROJ_TOKENS = 512
ATT_Q = 256
ATT_HEADS = 4
ROUTE_TOKENS = 256
GATHER_GROUP = 8
GATHER_RING = 4
GATHER_AHEAD = 2
GATHER_TOKENS = GATHER_GROUP * GATHER_RING
SC_CORES = 2
SC_SUBCORES = 16
SC_CHUNK = 64
SC_SHARE = 4
DENSE_TOKENS = 16
VMEM_LIMIT = 56 * 1024 * 1024


def _cparams(sem):
    return pltpu.CompilerParams(dimension_semantics=sem, vmem_limit_bytes=VMEM_LIMIT)


def _fold_rows(x, op):
    parts = [x[k * SUBLANE:(k + 1) * SUBLANE] for k in range(x.shape[0] // SUBLANE)]
    while len(parts) > 1:
        nxt = [op(parts[k], parts[k + 1]) for k in range(0, len(parts) - 1, 2)]
        if len(parts) % 2:
            nxt.append(parts[-1])
        parts = nxt
    return parts[0]


def _rms(x, g):
    return x * lax.rsqrt(jnp.mean(x * x, axis=-1, keepdims=True) + EPS) * g


def _split2(a):
    hi = a.astype(BF16)
    lo = (a - hi.astype(F32)).astype(BF16)
    return hi, lo


def _dot(a, b):
    return jnp.dot(a, b, preferred_element_type=F32)


def _dot_nt(a, b):
    return lax.dot_general(a, b, (((1,), (1,)), ((), ())), preferred_element_type=F32)


def _mem_kv_kernel(mem_ref, g_ref, w_ref, out_ref):
    y = _rms(mem_ref[0], g_ref[...])
    out_ref[0, 0] = _dot(y.astype(BF16), w_ref[0]).astype(BF16)


def _mem_kv(mem, mem_norm, w_pad):
    b, m, d = mem.shape
    depth, _, wcols = w_pad.shape
    return pl.pallas_call(
        _mem_kv_kernel,
        grid=(depth, b),
        in_specs=[
            pl.BlockSpec((1, m, d), lambda l, i: (i, 0, 0)),
            pl.BlockSpec((1, d), lambda l, i: (0, 0)),
            pl.BlockSpec((1, d, wcols), lambda l, i: (l, 0, 0)),
        ],
        out_specs=pl.BlockSpec((1, 1, m, wcols), lambda l, i: (l, i, 0, 0)),
        out_shape=jax.ShapeDtypeStruct((depth, b, m, wcols), BF16),
        compiler_params=_cparams(("arbitrary", "arbitrary")),
    )(mem, mem_norm.reshape(1, d), w_pad)


def _bias_kernel(rb_ref, out_ref):
    h = pl.program_id(0)
    max_exact = REL_BUCKETS // 2
    far = rb_ref[REL_BUCKETS - 1, h]
    krow = lax.broadcasted_iota(I32, (ATT_Q, ATT_Q), 0)
    qcol = lax.broadcasted_iota(I32, (ATT_Q, ATT_Q), 1)
    for r in range(2):
        dist = qcol - krow + ATT_Q * r
        n = jnp.maximum(dist, 0)
        nf = jnp.maximum(n, max_exact).astype(F32)
        large = max_exact + (jnp.log(nf / max_exact) / math.log(REL_MAX_DIST / max_exact)
                             * (REL_BUCKETS - max_exact)).astype(I32)
        large = jnp.minimum(large, REL_BUCKETS - 1)
        bucket = jnp.where(n < max_exact, n, large)
        acc = jnp.zeros((ATT_Q, ATT_Q), F32)
        for k in range(REL_BUCKETS):
            acc = jnp.where(bucket == k, rb_ref[k, h], acc)
        out_ref[0, r] = acc - far


def _bias_tiles(rel_bias):
    heads = rel_bias.shape[1]
    return pl.pallas_call(
        _bias_kernel,
        grid=(heads,),
        in_specs=[pl.BlockSpec(memory_space=pltpu.SMEM)],
        out_specs=pl.BlockSpec((1, 2, ATT_Q, ATT_Q), lambda h: (h, 0, 0, 0)),
        out_shape=jax.ShapeDtypeStruct((heads, 2, ATT_Q, ATT_Q), F32),
        compiler_params=_cparams(("arbitrary",)),
    )(rel_bias)


def _in_a_kernel(x_ref, g_ref, wq_ref, wc_ref, wm_ref, wih_ref, wil_ref, kvn_ref,
                 q_ref, c_ref, ct_ref, iq_ref, ka_ref, iw_ref, qm_ref):
    h = _rms(x_ref[...], g_ref[...])
    h_hi, h_lo = _split2(h)
    q_ref[...] = _dot(h_hi, wq_ref[...]).astype(BF16)
    qm_ref[...] = _dot(h_hi, wm_ref[...]).astype(BF16)
    c = _rms(_dot(h_hi, wc_ref[...]), kvn_ref[...])
    c_ref[...] = c.astype(BF16)
    for j in range(PROJ_TOKENS // ATT_Q):
        ct_ref[j] = c[j * ATT_Q:(j + 1) * ATT_Q].T.astype(BF16)
    wih = wih_ref[...]
    ii = _dot(h_hi, wih) + _dot(h_lo, wih) + _dot(h_hi, wil_ref[...])
    iq_ref[...] = ii[:, :IDX_HEADS * IDX_DIM]
    kk = ii[:, IDX_HEADS * IDX_DIM:IDX_HEADS * IDX_DIM + LANE]
    kk_hi, kk_lo = _split2(kk)
    lane = lax.broadcasted_iota(I32, kk.shape, 1)
    half = jnp.where(lane < IDX_DIM, kk_hi, kk_lo)
    ka_ref[...] = jnp.concatenate([half, half], axis=1)
    iw_ref[...] = ii[:, IDX_HEADS * IDX_DIM + LANE:]


def _in_a(x2, g, wq, wc, wm, wih, wil, kvn):
    n, d = x2.shape
    tt = PROJ_TOKENS
    full = lambda a: pl.BlockSpec(a.shape, lambda i: (0,) * a.ndim)
    row = lambda w: pl.BlockSpec((tt, w), lambda i: (i, 0))
    nblk = tt // ATT_Q
    outs = [
        jax.ShapeDtypeStruct((n, wq.shape[1]), BF16),
        jax.ShapeDtypeStruct((n, A_KV_RANK), BF16),
        jax.ShapeDtypeStruct((n // ATT_Q, A_KV_RANK, ATT_Q), BF16),
        jax.ShapeDtypeStruct((n, IDX_HEADS * IDX_DIM), F32),
        jax.ShapeDtypeStruct((n, 2 * LANE), BF16),
        jax.ShapeDtypeStruct((n, LANE), F32),
        jax.ShapeDtypeStruct((n, wm.shape[1]), BF16),
    ]
    out_specs = [
        row(wq.shape[1]), row(A_KV_RANK),
        pl.BlockSpec((nblk, A_KV_RANK, ATT_Q), lambda i: (i, 0, 0)),
        row(IDX_HEADS * IDX_DIM), row(2 * LANE), row(LANE), row(wm.shape[1]),
    ]
    return pl.pallas_call(
        _in_a_kernel,
        grid=(n // tt,),
        in_specs=[row(d), full(g), full(wq), full(wc), full(wm), full(wih), full(wil), full(kvn)],
        out_specs=out_specs,
        out_shape=outs,
        compiler_params=_cparams(("arbitrary",)),
    )(x2, g, wq, wc, wm, wih, wil, kvn)


def _dsa_kernel(q_ref, iq_ref, iw_ref, ka_ref, c_ref, ct_ref, wuk_ref, wuvt_ref, bias_ref,
                o_ref, key_s, lg_s, qbt_s, qt_s, olat_s, ot_s, *, heads):
    i = pl.program_id(1)
    nch = i + 1
    t0 = i * ATT_Q
    Q = ATT_Q
    krow = lax.broadcasted_iota(I32, (Q, Q), 0)
    qcol = lax.broadcasted_iota(I32, (Q, Q), 1)

    iqv = iq_ref[0]
    lane = lax.broadcasted_iota(I32, (Q, LANE), 1)
    for j in range(IDX_HEADS // 2):
        v = iqv[:, j * LANE:(j + 1) * LANE]
        r = pltpu.roll(v, IDX_DIM, 1)
        for hh, dup in ((2 * j, jnp.where(lane < IDX_DIM, v, r)),
                        (2 * j + 1, jnp.where(lane < IDX_DIM, r, v))):
            hi = dup.astype(BF16)
            lo = (dup - hi.astype(F32)).astype(BF16)
            qbt_s[hh, 0:LANE, :] = hi.astype(F32).T.astype(BF16)
            qbt_s[hh, LANE:2 * LANE, :] = lo.astype(F32).T.astype(BF16)
    wt = iw_ref[0].T * (IDX_HEADS ** -0.5)
    qt_s[...] = q_ref[0].astype(F32).T.astype(BF16)

    def score_chunk(c, carry):
        ka = ka_ref[0, pl.ds(pl.multiple_of(c * Q, Q), Q), :]
        acc = jnp.zeros((Q, Q), F32)
        for hh in range(IDX_HEADS):
            z = _dot(ka, qbt_s[hh])
            acc = acc + jnp.maximum(z, 0.0) * wt[hh:hh + 1, :]
        acc = acc * (IDX_DIM ** -0.5)
        bits = pltpu.bitcast(acc, I32)
        skey = jnp.where(bits < 0, bits ^ 0x7FFFFFFF, bits)
        causal = (krow + c * Q) <= (qcol + t0)
        key_s[pl.ds(pl.multiple_of(c * Q, Q), Q), :] = jnp.where(causal, skey, INT_MIN)
        return carry

    lax.fori_loop(0, nch, score_chunk, 0)

    def count_ge(thr):
        def body(c, acc):
            blk = key_s[pl.ds(pl.multiple_of(c * Q, Q), Q), :]
            return acc + _fold_rows((blk >= thr).astype(I32), jnp.add)
        acc = lax.fori_loop(0, nch, body, jnp.zeros((SUBLANE, Q), I32))
        return jnp.sum(acc, axis=0, keepdims=True)

    c0 = count_ge(jnp.zeros((1, Q), I32))
    has_k = c0 >= IDX_TOPK
    thr = jnp.where(has_k, 0, INT_MIN).astype(I32)
    cnt = jnp.where(has_k, c0, nch * Q)
    reachable = (lax.broadcasted_iota(I32, (1, Q), 1) + t0 + 1) >= IDX_TOPK

    def unsettled(cnt):
        return jnp.max(jnp.where((cnt != IDX_TOPK) & reachable, 1, 0)) > 0

    def bisect(it, state):
        thr, cnt = state
        cand = thr + lax.shift_left(jnp.int32(1), 30 - it)
        c = count_ge(cand)
        ok = c >= IDX_TOPK
        return jnp.where(ok, cand, thr), jnp.where(ok, c, cnt)

    thr, cnt = lax.fori_loop(0, 31, bisect, (thr, cnt))
    thr = jnp.maximum(thr, INT_MIN + 1)

    nbits = max(1, int(math.ceil(math.log2(ka_ref.shape[1] + 1))))

    def tie_cut(thr):
        need = IDX_TOPK - count_ge(thr + 1)

        def count_tie_below(bound):
            def body(c, acc):
                blk = key_s[pl.ds(pl.multiple_of(c * Q, Q), Q), :]
                hit = (blk == thr) & ((krow + c * Q) < bound)
                return acc + _fold_rows(hit.astype(I32), jnp.add)
            acc = lax.fori_loop(0, nch, body, jnp.zeros((SUBLANE, Q), I32))
            return jnp.sum(acc, axis=0, keepdims=True)

        def tie_bisect(it, p0):
            cand = p0 + lax.shift_left(jnp.int32(1), nbits - 1 - it)
            return jnp.where(count_tie_below(cand) < need, cand, p0)

        return lax.fori_loop(0, nbits, tie_bisect, jnp.zeros((1, Q), I32)) + 1

    pcut = lax.cond(unsettled(cnt), tie_cut, lambda thr: jnp.full((1, Q), 2 ** nbits, I32), thr)

    def mask_chunk(c, carry):
        off = pl.multiple_of(c * Q, Q)
        key = key_s[pl.ds(off, Q), :]
        sel = (key > thr) | ((key == thr) & ((krow + c * Q) < pcut))
        key_s[pl.ds(off, Q), :] = pltpu.bitcast(jnp.where(sel, 0.0, NEG_BIG).astype(F32), I32)
        return carry

    lax.fori_loop(0, nch, mask_chunk, 0)

    HB = ATT_HEADS

    def head_group(hg, carry):
        qlts = []
        for hh in range(HB):
            h = hg * HB + hh
            qh = qt_s[pl.ds(pl.multiple_of(h * A_HEAD_DIM, A_HEAD_DIM), A_HEAD_DIM), :]
            qlts.append((_dot(wuk_ref[h], qh) * (A_HEAD_DIM ** -0.5)).astype(BF16))
        qlt = jnp.concatenate(qlts, axis=1)

        def logits_chunk(c, m_acc, band):
            off = pl.multiple_of(c * Q, Q)
            lg = _dot(c_ref[0, pl.ds(off, Q), :], qlt)
            mask = pltpu.bitcast(key_s[pl.ds(off, Q), :], F32)
            new = []
            for hh in range(HB):
                lgh = lg[:, hh * Q:(hh + 1) * Q] + mask
                if band is not None:
                    lgh = lgh + bias_ref[hg * HB + hh, band]
                lg_s[pl.ds(off, Q), hh * Q:(hh + 1) * Q] = lgh
                new.append(jnp.maximum(m_acc[:, hh * Q:(hh + 1) * Q], _fold_rows(lgh, jnp.maximum)))
            return jnp.concatenate(new, axis=1)

        m_acc = jnp.full((SUBLANE, HB * Q), NEG_BIG, F32)
        m_acc = lax.fori_loop(0, nch - 2, lambda c, m: logits_chunk(c, m, None), m_acc)
        m_acc = lax.cond(nch >= 2, lambda m: logits_chunk(nch - 2, m, 1), lambda m: m, m_acc)
        m_acc = logits_chunk(nch - 1, m_acc, 0)
        m = jnp.max(m_acc, axis=0, keepdims=True)

        olat_s[...] = jnp.zeros_like(olat_s)

        def pv_chunk(c, s_acc):
            off = pl.multiple_of(c * Q, Q)
            p = jnp.exp(lg_s[pl.ds(off, Q), :] - m)
            olat_s[...] += _dot(ct_ref[0, c], p.astype(BF16))
            return s_acc + _fold_rows(p, jnp.add)

        s_acc = lax.fori_loop(0, nch, pv_chunk, jnp.zeros((SUBLANE, HB * Q), F32))
        s = jnp.sum(s_acc, axis=0, keepdims=True)
        for hh in range(HB):
            h = hg * HB + hh
            ol = olat_s[:, hh * Q:(hh + 1) * Q].astype(BF16)
            oh = _dot(wuvt_ref[h], ol) / s[:, hh * Q:(hh + 1) * Q]
            ot_s[pl.ds(pl.multiple_of(h * A_HEAD_DIM, A_HEAD_DIM), A_HEAD_DIM), :] = oh
        return carry

    lax.fori_loop(0, heads // HB, head_group, 0)
    o_ref[0] = ot_s[...].T.astype(BF16)


def _dsa(q, iq, iw, ka, c, ct, wuk, wuvt, bias):
    b, s, w = q.shape
    heads = wuk.shape[0]
    nq = s // ATT_Q
    assert heads % ATT_HEADS == 0
    once = pl.Buffered(1)
    full = lambda a: pl.BlockSpec(a.shape, lambda bi, i: (0,) * a.ndim, pipeline_mode=once)
    blk = lambda width: pl.BlockSpec((1, ATT_Q, width), lambda bi, i: (bi, i, 0))
    per_b = lambda width: pl.BlockSpec((1, s, width), lambda bi, i: (bi, 0, 0), pipeline_mode=once)
    return pl.pallas_call(
        functools.partial(_dsa_kernel, heads=heads),
        grid=(b, nq),
        in_specs=[
            blk(w), blk(iq.shape[2]), blk(iw.shape[2]),
            per_b(ka.shape[2]), per_b(c.shape[2]),
            pl.BlockSpec((1, nq, A_KV_RANK, ATT_Q), lambda bi, i: (bi, 0, 0, 0), pipeline_mode=once),
            full(wuk), full(wuvt), full(bias),
        ],
        out_specs=blk(w),
        out_shape=jax.ShapeDtypeStruct((b, s, w), BF16),
        scratch_shapes=[
            pltpu.VMEM((s, ATT_Q), I32),
            pltpu.VMEM((s, ATT_HEADS * ATT_Q), F32),
            pltpu.VMEM((IDX_HEADS, 2 * LANE, ATT_Q), BF16),
            pltpu.VMEM((w, ATT_Q), BF16),
            pltpu.VMEM((A_KV_RANK, ATT_HEADS * ATT_Q), F32),
            pltpu.VMEM((w, ATT_Q), F32),
        ],
        compiler_params=_cparams(("arbitrary", "arbitrary")),
    )(q, iq, iw, ka, c, ct, wuk, wuvt, bias)


def _out_kernel(x_ref, seq_ref, qm_ref, kv_ref, woa_ref, wob_ref, o_ref):
    acc = x_ref[...] + _dot(seq_ref[...], woa_ref[...])
    qm = qm_ref[...]
    kv = kv_ref[0]
    for h in range(MEM_HEADS):
        k = kv[:, h * LANE:(h + 1) * LANE]
        v = kv[:, (MEM_HEADS + h) * LANE:(MEM_HEADS + h + 1) * LANE]
        lg = _dot_nt(qm[:, h * LANE:(h + 1) * LANE], k) * (MEM_HEAD_DIM ** -0.5)
        p = jnp.exp(lg - jnp.max(lg, axis=-1, keepdims=True))
        oh = _dot(p.astype(BF16), v) / jnp.sum(p, axis=-1, keepdims=True)
        acc = acc + _dot(oh.astype(BF16), wob_ref[h])
    o_ref[...] = acc


def _out_proj(x2, seq2, qm2, kv, woa, wob, seq_len):
    n, d = x2.shape
    tt = PROJ_TOKENS
    per_seq = seq_len // tt
    full = lambda a: pl.BlockSpec(a.shape, lambda i: (0,) * a.ndim)
    row = lambda w: pl.BlockSpec((tt, w), lambda i: (i, 0))
    return pl.pallas_call(
        _out_kernel,
        grid=(n // tt,),
        in_specs=[row(d), row(seq2.shape[1]), row(qm2.shape[1]),
                  pl.BlockSpec((1,) + kv.shape[1:], lambda i: (i // per_seq, 0, 0)),
                  full(woa), full(wob)],
        out_specs=row(d),
        out_shape=jax.ShapeDtypeStruct((n, d), F32),
        compiler_params=_cparams(("arbitrary",)),
    )(x2, seq2, qm2, kv, woa, wob)


def _in_b_kernel(x_ref, g_ref, wz_ref, wx_ref, wm_ref, wdh_ref, wdl_ref,
                 z_ref, xbc_ref, dt_ref, qm_ref):
    h = _rms(x_ref[...], g_ref[...])
    h_hi, h_lo = _split2(h)
    z_ref[...] = _dot(h_hi, wz_ref[...]).astype(BF16)
    xbc_ref[...] = _dot(h_hi, wx_ref[...])
    qm_ref[...] = _dot(h_hi, wm_ref[...]).astype(BF16)
    wdh = wdh_ref[...]
    dt_ref[...] = _dot(h_hi, wdh) + _dot(h_lo, wdh) + _dot(h_hi, wdl_ref[...])


def _in_b(x2, g, wz, wx, wm, wdh, wdl):
    n, d = x2.shape
    tt = PROJ_TOKENS
    full = lambda a: pl.BlockSpec(a.shape, lambda i: (0,) * a.ndim)
    row = lambda w: pl.BlockSpec((tt, w), lambda i: (i, 0))
    return pl.pallas_call(
        _in_b_kernel,
        grid=(n // tt,),
        in_specs=[row(d), full(g), full(wz), full(wx), full(wm), full(wdh), full(wdl)],
        out_specs=[row(wz.shape[1]), row(wx.shape[1]), row(LANE), row(wm.shape[1])],
        out_shape=[jax.ShapeDtypeStruct((n, wz.shape[1]), BF16),
                   jax.ShapeDtypeStruct((n, wx.shape[1]), F32),
                   jax.ShapeDtypeStruct((n, LANE), F32),
                   jax.ShapeDtypeStruct((n, wm.shape[1]), BF16)],
        compiler_params=_cparams(("arbitrary",)),
    )(x2, g, wz, wx, wm, wdh, wdl)


def _ssd_kernel(xbc_ref, z_ref, dt_ref, cw_ref, cb_ref, dtb_ref, alog_ref, dsk_ref, on_ref,
                o_ref, tail_s, xpad_s, h_s, *, seq_width):
    L = SSD_CHUNK
    ci = pl.program_id(1)

    @pl.when(ci == 0)
    def _():
        tail_s[...] = jnp.zeros_like(tail_s)
        h_s[...] = jnp.zeros_like(h_s)

    xr = xbc_ref[0]
    xpad_s[0:SUBLANE, :] = tail_s[...]
    xpad_s[SUBLANE:SUBLANE + L, :] = xr
    tail_s[...] = xr[L - SUBLANE:L, :]
    conv = cb_ref[...] + jnp.zeros_like(xr)
    for j in range(CONV_WIDTH):
        off = SUBLANE - (CONV_WIDTH - 1) + j
        conv = conv + cw_ref[j:j + 1, :] * xpad_s[off:off + L, :]
    xc = conv * jax.nn.sigmoid(conv)
    gw = SSM_STATE
    bm = xc[:, seq_width:seq_width + SSM_GROUPS * gw]
    cm = xc[:, seq_width + SSM_GROUPS * gw:seq_width + 2 * SSM_GROUPS * gw]

    dt = jax.nn.softplus(dt_ref[0] + dtb_ref[...])
    a = -jnp.exp(alog_ref[...])
    da = dt * a
    row_i = lax.broadcasted_iota(I32, (L, L), 0)
    col_i = lax.broadcasted_iota(I32, (L, L), 1)
    tri = row_i >= col_i
    tri_b = tri.astype(BF16)
    d1 = da.astype(BF16)
    r1 = da - d1.astype(F32)
    d2 = r1.astype(BF16)
    d3 = (r1 - d2.astype(F32)).astype(BF16)
    acs = _dot(tri_b, d1) + _dot(tri_b, d2) + _dot(tri_b, d3)
    acs_t = acs.T
    a_last = acs[L - 1:L, :]
    e_acs = jnp.exp(acs)
    e_end = jnp.exp(a_last - acs)
    lane = lax.broadcasted_iota(I32, (L, LANE), 1)
    lo_half = lane < SSM_HEAD_DIM
    lane1 = lax.broadcasted_iota(I32, (1, LANE), 1)

    def pair_cols(m, r):
        return jnp.where(lo_half, m[:, r:r + 1], m[:, r + 1:r + 2])

    heads_per_group = (seq_width // SSM_HEAD_DIM) // SSM_GROUPS
    pairs_per_group = heads_per_group // 2
    pieces = []
    ssq = jnp.zeros((L, 1), F32)
    for g in range(SSM_GROUPS):
        bg = bm[:, g * gw:(g + 1) * gw]
        cg = cm[:, g * gw:(g + 1) * gw]
        bg_b = bg.astype(BF16)
        cg_b = cg.astype(BF16)
        bgt_b = bg.T.astype(BF16)
        cb = _dot_nt(cg_b, bg_b)
        for jp in range(pairs_per_group):
            r = g * heads_per_group + 2 * jp
            col0 = (g * pairs_per_group + jp) * LANE
            xs = xc[:, col0:col0 + LANE]
            xdt = (xs * pair_cols(dt, r)).astype(BF16)
            ys = []
            for rr in (r, r + 1):
                seg = acs[:, rr:rr + 1] - acs_t[rr:rr + 1, :]
                dec = jnp.exp(jnp.where(tri, seg, -jnp.inf))
                ys.append(_dot((cb * dec).astype(BF16), xdt))
            y = jnp.where(lo_half, ys[0], ys[1])
            hcol = jp * LANE
            h_prev = h_s[g, :, hcol:hcol + LANE]
            y = y + _dot(cg_b, h_prev.astype(BF16)) * pair_cols(e_acs, r)
            xw = (xs * pair_cols(dt * e_end, r)).astype(BF16)
            st = _dot(bgt_b, xw)
            cd = jnp.where(lane1 < SSM_HEAD_DIM, jnp.exp(a_last[:, r:r + 1]),
                           jnp.exp(a_last[:, r + 1:r + 2]))
            h_s[g, :, hcol:hcol + LANE] = h_prev * cd + st
            y = y + xs * dsk_ref[:, col0:col0 + LANE]
            zz = z_ref[0, :, col0:col0 + LANE].astype(F32)
            y = y * (zz * jax.nn.sigmoid(zz))
            ssq = ssq + jnp.sum(y * y, axis=-1, keepdims=True)
            pieces.append(y)
    scale = lax.rsqrt(ssq / seq_width + EPS)
    for k, y in enumerate(pieces):
        o_ref[0, :, k * LANE:(k + 1) * LANE] = (y * scale * on_ref[:, k * LANE:(k + 1) * LANE]).astype(BF16)


def _ssd(xbc, z, dt, cw, cb, dtb, alog, dsk, onorm):
    b, s, cdim = xbc.shape
    w = z.shape[2]
    L = SSD_CHUNK
    full = lambda a: pl.BlockSpec(a.shape, lambda bi, i: (0,) * a.ndim)
    blk = lambda width: pl.BlockSpec((1, L, width), lambda bi, i: (bi, i, 0))
    return pl.pallas_call(
        functools.partial(_ssd_kernel, seq_width=w),
        grid=(b, s // L),
        in_specs=[blk(cdim), blk(w), blk(LANE), full(cw), full(cb), full(dtb), full(alog),
                  full(dsk), full(onorm)],
        out_specs=blk(w),
        out_shape=jax.ShapeDtypeStruct((b, s, w), BF16),
        scratch_shapes=[
            pltpu.VMEM((SUBLANE, cdim), F32),
            pltpu.VMEM((SUBLANE + L, cdim), F32),
            pltpu.VMEM((SSM_GROUPS, SSM_STATE, w // SSM_GROUPS), F32),
        ],
        compiler_params=_cparams(("arbitrary", "arbitrary")),
    )(xbc, z, dt, cw, cb, dtb, alog, dsk, onorm)


def _top_rows(vals, k, val_out, idx_out, payload=None):
    rows = lax.broadcasted_iota(I32, vals.shape, 0)
    big = vals.shape[0]
    for j in range(k):
        m = jnp.max(_fold_rows(vals, jnp.maximum), axis=0, keepdims=True)
        am = jnp.min(_fold_rows(jnp.where(vals == m, rows, big), jnp.minimum), axis=0, keepdims=True)
        hit = rows == am
        val_out[j:j + 1, :] = m
        if payload is None:
            idx_out[j:j + 1, :] = am
        else:
            idx_out[j:j + 1, :] = jnp.sum(_fold_rows(jnp.where(hit, payload, 0), jnp.add),
                                          axis=0, keepdims=True)
        vals = jnp.where(hit, -jnp.inf, vals)


def _route_kernel(x_ref, g_ref, wqt_ref, sk_ref, hn_ref, e_ref, gate_ref,
                  ts_s, ti_s, bs_s, et_s, gt_s, cs_s, ci_s):
    T = ROUTE_TOKENS
    hn = _rms(x_ref[...], g_ref[...])
    hn_ref[...] = hn
    hb = hn.astype(BF16)
    qt = _dot_nt(wqt_ref[...], hb)
    half = qt.shape[0] // (PEER_HEADS * 2)
    for hd in range(PEER_HEADS):
        for side in range(2):
            r0 = (hd * 2 + side) * half
            sc = _dot(sk_ref[side], qt[r0:r0 + half].astype(BF16))
            _top_rows(sc, PEER_TOPK, ts_s.at[side], ti_s.at[side])
        off = 0
        for a in range(PEER_TOPK):
            nb = PEER_TOPK // (a + 1)
            cs_s[off:off + nb, :] = ts_s[0, a:a + 1, :] + ts_s[1, 0:nb, :]
            ci_s[off:off + nb, :] = ti_s[0, a:a + 1, :] * PEER_KEYS + ti_s[1, 0:nb, :]
            off += nb
        cs_s[off:, :] = jnp.full((cs_s.shape[0] - off, T), -jnp.inf, F32)
        ci_s[off:, :] = jnp.zeros((ci_s.shape[0] - off, T), I32)
        _top_rows(cs_s[...], PEER_TOPK, bs_s, et_s.at[pl.ds(hd * PEER_TOPK, PEER_TOPK)],
                  payload=ci_s[...])
        best = bs_s[...]
        p = jnp.exp(best - jnp.max(best, axis=0, keepdims=True))
        gt_s[hd * PEER_TOPK:(hd + 1) * PEER_TOPK, :] = p / jnp.sum(p, axis=0, keepdims=True)
    e_ref[...] = et_s[...].T
    gate_ref[...] = gt_s[...].T


def _route(x2, g, wqt, sk):
    n, d = x2.shape
    tt = ROUTE_TOKENS
    npair = PEER_HEADS * PEER_TOPK
    ncand = sum(PEER_TOPK // (a + 1) for a in range(PEER_TOPK))
    ncand = -(-ncand // SUBLANE) * SUBLANE
    full = lambda a: pl.BlockSpec(a.shape, lambda i: (0,) * a.ndim)
    row = lambda w: pl.BlockSpec((tt, w), lambda i: (i, 0))
    return pl.pallas_call(
        _route_kernel,
        grid=(n // tt,),
        in_specs=[row(d), full(g), full(wqt), full(sk)],
        out_specs=[row(d), row(npair), row(npair)],
        out_shape=[jax.ShapeDtypeStruct((n, d), F32),
                   jax.ShapeDtypeStruct((n, npair), I32),
                   jax.ShapeDtypeStruct((n, npair), F32)],
        scratch_shapes=[
            pltpu.VMEM((2, PEER_TOPK, tt), F32),
            pltpu.VMEM((2, PEER_TOPK, tt), I32),
            pltpu.VMEM((PEER_TOPK, tt), F32),
            pltpu.VMEM((npair, tt), I32),
            pltpu.VMEM((npair, tt), F32),
            pltpu.VMEM((ncand, tt), F32),
            pltpu.VMEM((ncand, tt), I32),
        ],
        compiler_params=_cparams(("arbitrary",)),
    )(x2, g, wqt, sk)


def _peer_consts(npair):
    rows = npair * SUBLANE
    sel = (lax.shift_right_logical(lax.broadcasted_iota(I32, (rows, npair), 0), 3)
           == lax.broadcasted_iota(I32, (rows, npair), 1)).astype(BF16)
    sel_t = (lax.broadcasted_iota(I32, (npair, rows), 0)
             == lax.shift_right_logical(lax.broadcasted_iota(I32, (npair, rows), 1), 3)).astype(BF16)
    diag = (lax.broadcasted_iota(I32, (SUBLANE, rows), 0)
            == (lax.broadcasted_iota(I32, (SUBLANE, rows), 1) & (SUBLANE - 1)))
    return sel, sel_t, diag


def _peer_group(load_rows, between, t0, gate_ref, hn_ref, x_ref, fn_ref, o_ref, consts, final_norm):
    sel, sel_t, diag = consts
    G = GATHER_GROUP
    rows = sel.shape[0]
    parts = []
    for j in range(G):
        between(j, 0)
        w = load_rows(j).reshape(rows, LANE)
        mu = pltpu.bitcast(lax.shift_left(w, jnp.uint32(16)), F32).astype(BF16)
        gj = _dot_nt(hn_ref[t0 + j].astype(BF16), mu)
        parts.append(jnp.where(diag, gj, 0.0))
    dg = _dot(jnp.concatenate(parts, axis=0).astype(BF16), sel)
    dots = jnp.concatenate(
        [jnp.sum(dg[j * SUBLANE:(j + 1) * SUBLANE], axis=0, keepdims=True) for j in range(G)], axis=0)
    gelu = 0.5 * dots * (1.0 + lax.erf(dots * (2.0 ** -0.5)))
    act = (gelu * gate_ref[t0:t0 + G, :]).astype(BF16)
    arep = _dot(act, sel_t)
    ys = []
    for j in range(G):
        between(j, 1)
        w = load_rows(j).reshape(rows, LANE)
        mv = pltpu.bitcast(w & jnp.uint32(0xFFFF0000), F32).astype(BF16)
        aexp = jnp.where(diag, arep[j:j + 1, :], 0.0).astype(BF16)
        y = _dot(aexp, mv) + x_ref[t0 + j]
        if final_norm:
            ssq = jnp.sum(jnp.sum(y * y, axis=1, keepdims=True), axis=0, keepdims=True)
            y = y * lax.rsqrt(ssq / (SUBLANE * LANE) + EPS) * fn_ref[...]
        ys.append(y)
    for j in range(G):
        o_ref[t0 + j] = ys[j]


def _expert_kernel(idx_ref, idxn_ref, gate_ref, hn_ref, x_ref, fn_ref, tab_ref, o_ref, buf, sem,
                   *, final_norm):
    T = GATHER_TOKENS
    G = GATHER_GROUP
    npair = idx_ref.shape[1]
    ngroups = T // G
    rows = npair * SUBLANE

    assert ngroups == GATHER_RING and GATHER_AHEAD < ngroups
    step = pl.program_id(0)
    half = npair // 2

    def issue(iref, g, j, part):
        for p in range(part * half, (part + 1) * half):
            e = iref[g * G + j, p]
            pltpu.make_async_copy(tab_ref.at[e], buf.at[g * G + j, p], sem.at[g]).start(priority=p % 2)

    def wait_group(g):
        for j in range(G):
            pltpu.make_async_copy(tab_ref.at[pl.ds(0, npair)], buf.at[g * G + j], sem.at[g]).wait()

    @pl.when(step == 0)
    def _():
        for g in range(GATHER_AHEAD):
            for j in range(G):
                issue(idx_ref, g, j, 0)
                issue(idx_ref, g, j, 1)

    consts = _peer_consts(npair)
    for g in range(ngroups):
        nxt = g + GATHER_AHEAD
        nref, ng = (idx_ref, nxt) if nxt < ngroups else (idxn_ref, nxt - ngroups)
        wait_group(g)
        _peer_group(lambda j: buf[g * G + j], lambda j, part: issue(nref, ng, j, part), g * G,
                    gate_ref, hn_ref, x_ref, fn_ref, o_ref, consts, final_norm)

    @pl.when(step == pl.num_programs(0) - 1)
    def _():
        for g in range(GATHER_AHEAD):
            wait_group(g)


def _experts(idx, gate, hn3, x3, fnorm3, table3, final_norm, n):
    tt = GATHER_TOKENS
    npair = idx.shape[1]
    tile = lambda: pl.BlockSpec((tt, SUBLANE, LANE), lambda i: (i, 0, 0))
    nsteps = n // tt
    return pl.pallas_call(
        functools.partial(_expert_kernel, final_norm=final_norm),
        grid=(nsteps,),
        in_specs=[
            pl.BlockSpec((tt, npair), lambda i: (i, 0), memory_space=pltpu.SMEM),
            pl.BlockSpec((tt, npair), lambda i: (jnp.minimum(i + 1, nsteps - 1), 0),
                         memory_space=pltpu.SMEM),
            pl.BlockSpec((tt, npair), lambda i: (i, 0)),
            tile(), tile(),
            pl.BlockSpec((SUBLANE, LANE), lambda i: (0, 0)),
            pl.BlockSpec(memory_space=pl.ANY),
        ],
        out_specs=tile(),
        out_shape=jax.ShapeDtypeStruct((n, SUBLANE, LANE), F32),
        scratch_shapes=[
            pltpu.VMEM((GATHER_RING * GATHER_GROUP, npair, SUBLANE, LANE), jnp.uint32),
            pltpu.SemaphoreType.DMA((GATHER_RING,)),
        ],
        compiler_params=_cparams(("arbitrary",)),
    )(idx, idx, gate, hn3, x3, fnorm3, table3)


def _dense_expert_kernel(rows_ref, gate_ref, hn_ref, x_ref, fn_ref, o_ref, *, final_norm):
    npair = gate_ref.shape[1]
    consts = _peer_consts(npair)
    for g in range(DENSE_TOKENS // GATHER_GROUP):
        t0 = g * GATHER_GROUP
        _peer_group(lambda j: pltpu.bitcast(rows_ref[(t0 + j) * npair:(t0 + j + 1) * npair], jnp.uint32),
                    lambda j, part: None, t0, gate_ref, hn_ref, x_ref, fn_ref, o_ref, consts, final_norm)


def _experts_dense(rows3, gate, hn3, x3, fnorm3, final_norm, first):
    tt = DENSE_TOKENS
    npair = gate.shape[1]
    n = x3.shape[0] - first
    assert first % tt == 0 and n % tt == 0 and rows3.shape[0] == n * npair
    ob = first // tt
    tile = lambda: pl.BlockSpec((tt, SUBLANE, LANE), lambda i: (i + ob, 0, 0))
    return pl.pallas_call(
        functools.partial(_dense_expert_kernel, final_norm=final_norm),
        grid=(n // tt,),
        in_specs=[
            pl.BlockSpec((tt * npair, SUBLANE, LANE), lambda i: (i, 0, 0)),
            pl.BlockSpec((tt, npair), lambda i: (i + ob, 0)),
            tile(), tile(),
            pl.BlockSpec((SUBLANE, LANE), lambda i: (0, 0)),
        ],
        out_specs=pl.BlockSpec((tt, SUBLANE, LANE), lambda i: (i, 0, 0)),
        out_shape=jax.ShapeDtypeStruct((n, SUBLANE, LANE), F32),
        compiler_params=_cparams(("arbitrary",)),
    )(rows3, gate, hn3, x3, fnorm3)


def _sc_gather(table, idx_flat):
    rows, row_shape = idx_flat.shape[0], table.shape[1:]
    nw = SC_CORES * SC_SUBCORES
    rows_per_w = rows // nw
    assert rows % (nw * SC_CHUNK) == 0
    nchunks = rows_per_w // SC_CHUNK
    mesh = plsc.VectorSubcoreMesh(core_axis_name="c", subcore_axis_name="s")

    def body(tab_hbm, idx_hbm, out_hbm, idx_v, rows_v, sem):
        wid = lax.axis_index("s") * SC_CORES + lax.axis_index("c")
        base = wid * rows_per_w

        @pl.loop(0, nchunks)
        def _(ci):
            off = pl.multiple_of(base + ci * SC_CHUNK, SC_CHUNK)
            pltpu.sync_copy(idx_hbm.at[pl.ds(off, SC_CHUNK)], idx_v)
            pltpu.async_copy(tab_hbm.at[idx_v], rows_v, sem).wait()
            pltpu.sync_copy(rows_v, out_hbm.at[pl.ds(off, SC_CHUNK)])

    return pl.kernel(
        body,
        out_type=jax.ShapeDtypeStruct((rows,) + row_shape, table.dtype),
        mesh=mesh,
        scratch_types=[pltpu.VMEM((SC_CHUNK,), I32), pltpu.VMEM((SC_CHUNK,) + row_shape, table.dtype),
                       pltpu.SemaphoreType.DMA],
    )(table, idx_flat)


def _pad_heads(w, heads, dim):
    d = w.shape[0]
    w = w.reshape(d, heads, dim)
    return jnp.pad(w, ((0, 0), (0, 0), (0, LANE - dim))).reshape(d, heads * LANE)


def _pack_table(u, v):
    ub = lax.bitcast_convert_type(u.astype(BF16), jnp.uint16).astype(jnp.uint32)
    vb = lax.bitcast_convert_type(v.astype(BF16), jnp.uint16).astype(jnp.uint32)
    return ub | (vb << 16)


def _split_w(w):
    hi = w.astype(BF16)
    return hi, (w - hi.astype(F32)).astype(BF16)


def kernel(x, mem, mem_norm, rel_bias, mix_norm, ffn_norm, final_norm, w_o, w_mem_kv, a_w_in,
           a_kv_norm, a_w_uk, a_w_uv, b_w_in, b_conv_w, b_conv_b, b_dt_bias, b_a_log, b_d_skip,
           b_out_norm, peer_w_q, peer_sub_keys, peer_u, peer_v):
    b, s, d = x.shape
    n = b * s
    depth = w_o.shape[0]
    mem_width = MEM_HEADS * MEM_HEAD_DIM
    seq_width = w_o.shape[1] - mem_width
    a_heads = seq_width // A_HEAD_DIM
    ssm_heads = seq_width // SSM_HEAD_DIM
    conv_dim = seq_width + 2 * SSM_GROUPS * SSM_STATE
    assert s % ATT_Q == 0 and s % PROJ_TOKENS == 0 and n % ROUTE_TOKENS == 0 and s >= 4 * IDX_TOPK
    assert d == SUBLANE * LANE and n % GATHER_TOKENS == 0

    wk, wv = w_mem_kv[:, :, :mem_width], w_mem_kv[:, :, mem_width:]
    w_kv_pad = jnp.concatenate(
        [jnp.stack([_pad_heads(wk[l], MEM_HEADS, MEM_HEAD_DIM) for l in range(depth)]),
         jnp.stack([_pad_heads(wv[l], MEM_HEADS, MEM_HEAD_DIM) for l in range(depth)])],
        axis=-1).astype(BF16)
    kv_all = _mem_kv(mem, mem_norm, w_kv_pad)

    x2 = x.reshape(n, d)
    for i in range(depth):
        j = i // 2
        woa = w_o[i, :seq_width].astype(BF16)
        wob = jnp.pad(w_o[i, seq_width:].reshape(MEM_HEADS, MEM_HEAD_DIM, d),
                      ((0, 0), (0, LANE - MEM_HEAD_DIM), (0, 0))).astype(BF16)
        g_mix = mix_norm[i].reshape(1, d)
        if i % 2 == 0:
            w_in = a_w_in[j]
            o0 = seq_width
            o1 = o0 + A_KV_RANK
            o2 = o1 + IDX_HEADS * IDX_DIM
            o3 = o2 + IDX_DIM
            o4 = o3 + IDX_HEADS
            wq = w_in[:, :o0].astype(BF16)
            wc = w_in[:, o0:o1].astype(BF16)
            wm = _pad_heads(w_in[:, o4:], MEM_HEADS, MEM_HEAD_DIM).astype(BF16)
            w_idx = jnp.concatenate(
                [w_in[:, o1:o2], w_in[:, o2:o3], w_in[:, o2:o3],
                 jnp.pad(w_in[:, o3:o4], ((0, 0), (0, LANE - IDX_HEADS)))], axis=1)
            wih, wil = _split_w(w_idx)
            q, c, ct, iq, ka, iw, qm = _in_a(x2, g_mix, wq, wc, wm, wih, wil,
                                             a_kv_norm[j].reshape(1, A_KV_RANK))
            bias = _bias_tiles(rel_bias)
            r3 = lambda t: t.reshape(b, s, t.shape[-1])
            seq = _dsa(r3(q), r3(iq), r3(iw), r3(ka), r3(c),
                       ct.reshape(b, s // ATT_Q, A_KV_RANK, ATT_Q),
                       a_w_uk[j].astype(BF16), jnp.swapaxes(a_w_uv[j], 1, 2).astype(BF16), bias)
        else:
            w_in = b_w_in[j]
            o0 = seq_width
            o1 = o0 + conv_dim
            o2 = o1 + ssm_heads
            wz = w_in[:, :o0].astype(BF16)
            wx = w_in[:, o0:o1].astype(BF16)
            wm = _pad_heads(w_in[:, o2:], MEM_HEADS, MEM_HEAD_DIM).astype(BF16)
            wdh, wdl = _split_w(jnp.pad(w_in[:, o1:o2], ((0, 0), (0, LANE - ssm_heads))))
            z, xbc, dt, qm = _in_b(x2, g_mix, wz, wx, wm, wdh, wdl)
            padh = lambda t: jnp.pad(t.reshape(1, ssm_heads), ((0, 0), (0, LANE - ssm_heads)))
            seq = _ssd(xbc.reshape(b, s, conv_dim), z.reshape(b, s, seq_width),
                       dt.reshape(b, s, LANE), b_conv_w[j], b_conv_b[j].reshape(1, conv_dim),
                       padh(b_dt_bias[j]), padh(b_a_log[j]),
                       jnp.repeat(b_d_skip[j], SSM_HEAD_DIM).reshape(1, seq_width),
                       b_out_norm[j].reshape(1, seq_width))
        x2 = _out_proj(x2, seq.reshape(n, seq_width), qm, kv_all[i], woa, wob, s)

        wqt = jnp.transpose(peer_w_q[i]).astype(BF16)
        hn, eidx, gate = _route(x2, ffn_norm[i].reshape(1, d), wqt, peer_sub_keys[i].astype(BF16))
        tile3 = lambda t: t.reshape(t.shape[0], SUBLANE, LANE)
        table3 = tile3(_pack_table(peer_u[i], peer_v[i]))
        n_sc = (n // SC_SHARE) // GATHER_TOKENS * GATHER_TOKENS
        n_tc = n - n_sc
        hn3, x3, fn3 = tile3(hn), tile3(x2), final_norm.reshape(SUBLANE, LANE)
        last = i == depth - 1
        rows_sc = _sc_gather(lax.bitcast_convert_type(table3, I32), eidx[n_tc:].reshape(-1))
        y_tc = _experts(eidx, gate, hn3, x3, fn3, table3, last, n_tc)
        y_sc = _experts_dense(rows_sc, gate, hn3, x3, fn3, last, n_tc)
        x2 = jnp.concatenate([y_tc, y_sc], axis=0).reshape(n, d)
    return x2.reshape(b, s, d)
```

```python
import functools
import math

import jax
import jax.numpy as jnp
from jax import lax
from jax.experimental import pallas as pl
from jax.experimental.pallas import tpu as pltpu

F32 = jnp.float32
BF16 = jnp.bfloat16
I32 = jnp.int32

EPS = 1e-6
MEM_HEADS = 4
MEM_HEAD_DIM = 64
A_HEAD_DIM = 64
A_KV_RANK = 256
IDX_HEADS = 8
IDX_DIM = 64
IDX_TOPK = 256
REL_BUCKETS = 32
REL_MAX_DIST = 128
SSM_HEAD_DIM = 64
SSM_GROUPS = 2
SSM_STATE = 128
CONV_WIDTH = 4
SSD_CHUNK = 128
PEER_HEADS = 8
PEER_KEYS = 128
PEER_TOPK = 16

LANE = 128
SUBLANE = 8
INT_MIN = -(2 ** 31)
NEG_BIG = -1e30

PROJ_TOKENS = 512
ATT_Q = 256
ATT_HEADS = 4
ROUTE_TOKENS = 256
GATHER_GROUP = 8
GATHER_RING = 4
GATHER_AHEAD = 3
GATHER_TOKENS = GATHER_GROUP * GATHER_RING
VMEM_LIMIT = 56 * 1024 * 1024


def _cparams(sem):
    return pltpu.CompilerParams(dimension_semantics=sem, vmem_limit_bytes=VMEM_LIMIT)


def _fold_rows(x, op):
    parts = [x[k * SUBLANE:(k + 1) * SUBLANE] for k in range(x.shape[0] // SUBLANE)]
    while len(parts) > 1:
        nxt = [op(parts[k], parts[k + 1]) for k in range(0, len(parts) - 1, 2)]
        if len(parts) % 2:
            nxt.append(parts[-1])
        parts = nxt
    return parts[0]


def _rms(x, g):
    return x * lax.rsqrt(jnp.mean(x * x, axis=-1, keepdims=True) + EPS) * g


def _split2(a):
    hi = a.astype(BF16)
    lo = (a - hi.astype(F32)).astype(BF16)
    return hi, lo


def _dot(a, b):
    return jnp.dot(a, b, preferred_element_type=F32)


def _dot_nt(a, b):
    return lax.dot_general(a, b, (((1,), (1,)), ((), ())), preferred_element_type=F32)


def _mem_kv_kernel(mem_ref, g_ref, w_ref, out_ref):
    y = _rms(mem_ref[0], g_ref[...])
    out_ref[0, 0] = _dot(y.astype(BF16), w_ref[0]).astype(BF16)


def _mem_kv(mem, mem_norm, w_pad):
    b, m, d = mem.shape
    depth, _, wcols = w_pad.shape
    return pl.pallas_call(
        _mem_kv_kernel,
        grid=(depth, b),
        in_specs=[
            pl.BlockSpec((1, m, d), lambda l, i: (i, 0, 0)),
            pl.BlockSpec((1, d), lambda l, i: (0, 0)),
            pl.BlockSpec((1, d, wcols), lambda l, i: (l, 0, 0)),
        ],
        out_specs=pl.BlockSpec((1, 1, m, wcols), lambda l, i: (l, i, 0, 0)),
        out_shape=jax.ShapeDtypeStruct((depth, b, m, wcols), BF16),
        compiler_params=_cparams(("arbitrary", "arbitrary")),
    )(mem, mem_norm.reshape(1, d), w_pad)


def _bias_kernel(rb_ref, out_ref):
    h = pl.program_id(0)
    max_exact = REL_BUCKETS // 2
    far = rb_ref[REL_BUCKETS - 1, h]
    krow = lax.broadcasted_iota(I32, (ATT_Q, ATT_Q), 0)
    qcol = lax.broadcasted_iota(I32, (ATT_Q, ATT_Q), 1)
    for r in range(2):
        dist = qcol - krow + ATT_Q * r
        n = jnp.maximum(dist, 0)
        nf = jnp.maximum(n, max_exact).astype(F32)
        large = max_exact + (jnp.log(nf / max_exact) / math.log(REL_MAX_DIST / max_exact)
                             * (REL_BUCKETS - max_exact)).astype(I32)
        large = jnp.minimum(large, REL_BUCKETS - 1)
        bucket = jnp.where(n < max_exact, n, large)
        acc = jnp.zeros((ATT_Q, ATT_Q), F32)
        for k in range(REL_BUCKETS):
            acc = jnp.where(bucket == k, rb_ref[k, h], acc)
        out_ref[0, r] = acc - far


def _bias_tiles(rel_bias):
    heads = rel_bias.shape[1]
    return pl.pallas_call(
        _bias_kernel,
        grid=(heads,),
        in_specs=[pl.BlockSpec(memory_space=pltpu.SMEM)],
        out_specs=pl.BlockSpec((1, 2, ATT_Q, ATT_Q), lambda h: (h, 0, 0, 0)),
        out_shape=jax.ShapeDtypeStruct((heads, 2, ATT_Q, ATT_Q), F32),
        compiler_params=_cparams(("arbitrary",)),
    )(rel_bias)


def _in_a_kernel(x_ref, g_ref, wq_ref, wc_ref, wm_ref, wih_ref, wil_ref, kvn_ref,
                 q_ref, c_ref, ct_ref, iq_ref, ka_ref, iw_ref, qm_ref):
    h = _rms(x_ref[...], g_ref[...])
    h_hi, h_lo = _split2(h)
    q_ref[...] = _dot(h_hi, wq_ref[...]).astype(BF16)
    qm_ref[...] = _dot(h_hi, wm_ref[...]).astype(BF16)
    c = _rms(_dot(h_hi, wc_ref[...]), kvn_ref[...])
    c_ref[...] = c.astype(BF16)
    for j in range(PROJ_TOKENS // ATT_Q):
        ct_ref[j] = c[j * ATT_Q:(j + 1) * ATT_Q].T.astype(BF16)
    wih = wih_ref[...]
    ii = _dot(h_hi, wih) + _dot(h_lo, wih) + _dot(h_hi, wil_ref[...])
    iq_ref[...] = ii[:, :IDX_HEADS * IDX_DIM]
    kk = ii[:, IDX_HEADS * IDX_DIM:IDX_HEADS * IDX_DIM + LANE]
    kk_hi, kk_lo = _split2(kk)
    lane = lax.broadcasted_iota(I32, kk.shape, 1)
    half = jnp.where(lane < IDX_DIM, kk_hi, kk_lo)
    ka_ref[...] = jnp.concatenate([half, half], axis=1)
    iw_ref[...] = ii[:, IDX_HEADS * IDX_DIM + LANE:]


def _in_a(x2, g, wq, wc, wm, wih, wil, kvn):
    n, d = x2.shape
    tt = PROJ_TOKENS
    full = lambda a: pl.BlockSpec(a.shape, lambda i: (0,) * a.ndim)
    row = lambda w: pl.BlockSpec((tt, w), lambda i: (i, 0))
    nblk = tt // ATT_Q
    outs = [
        jax.ShapeDtypeStruct((n, wq.shape[1]), BF16),
        jax.ShapeDtypeStruct((n, A_KV_RANK), BF16),
        jax.ShapeDtypeStruct((n // ATT_Q, A_KV_RANK, ATT_Q), BF16),
        jax.ShapeDtypeStruct((n, IDX_HEADS * IDX_DIM), F32),
        jax.ShapeDtypeStruct((n, 2 * LANE), BF16),
        jax.ShapeDtypeStruct((n, LANE), F32),
        jax.ShapeDtypeStruct((n, wm.shape[1]), BF16),
    ]
    out_specs = [
        row(wq.shape[1]), row(A_KV_RANK),
        pl.BlockSpec((nblk, A_KV_RANK, ATT_Q), lambda i: (i, 0, 0)),
        row(IDX_HEADS * IDX_DIM), row(2 * LANE), row(LANE), row(wm.shape[1]),
    ]
    return pl.pallas_call(
        _in_a_kernel,
        grid=(n // tt,),
        in_specs=[row(d), full(g), full(wq), full(wc), full(wm), full(wih), full(wil), full(kvn)],
        out_specs=out_specs,
        out_shape=outs,
        compiler_params=_cparams(("arbitrary",)),
    )(x2, g, wq, wc, wm, wih, wil, kvn)


def _dsa_kernel(q_ref, iq_ref, iw_ref, ka_ref, c_ref, ct_ref, wuk_ref, wuvt_ref, bias_ref,
                o_ref, key_s, lg_s, qbt_s, qt_s, olat_s, ot_s, *, heads):
    i = pl.program_id(1)
    nch = i + 1
    t0 = i * ATT_Q
    Q = ATT_Q
    krow = lax.broadcasted_iota(I32, (Q, Q), 0)
    qcol = lax.broadcasted_iota(I32, (Q, Q), 1)

    iqv = iq_ref[0]
    lane = lax.broadcasted_iota(I32, (Q, LANE), 1)
    for j in range(IDX_HEADS // 2):
        v = iqv[:, j * LANE:(j + 1) * LANE]
        r = pltpu.roll(v, IDX_DIM, 1)
        for hh, dup in ((2 * j, jnp.where(lane < IDX_DIM, v, r)),
                        (2 * j + 1, jnp.where(lane < IDX_DIM, r, v))):
            hi = dup.astype(BF16)
            lo = (dup - hi.astype(F32)).astype(BF16)
            qbt_s[hh, 0:LANE, :] = hi.astype(F32).T.astype(BF16)
            qbt_s[hh, LANE:2 * LANE, :] = lo.astype(F32).T.astype(BF16)
    wt = iw_ref[0].T * (IDX_HEADS ** -0.5)
    qt_s[...] = q_ref[0].astype(F32).T.astype(BF16)

    def score_chunk(c, carry):
        ka = ka_ref[0, pl.ds(pl.multiple_of(c * Q, Q), Q), :]
        acc = jnp.zeros((Q, Q), F32)
        for hh in range(IDX_HEADS):
            z = _dot(ka, qbt_s[hh])
            acc = acc + jnp.maximum(z, 0.0) * wt[hh:hh + 1, :]
        acc = acc * (IDX_DIM ** -0.5)
        bits = pltpu.bitcast(acc, I32)
        skey = jnp.where(bits < 0, bits ^ 0x7FFFFFFF, bits)
        causal = (krow + c * Q) <= (qcol + t0)
        key_s[pl.ds(pl.multiple_of(c * Q, Q), Q), :] = jnp.where(causal, skey, INT_MIN)
        return carry

    lax.fori_loop(0, nch, score_chunk, 0)

    def count_ge(thr):
        def body(c, acc):
            blk = key_s[pl.ds(pl.multiple_of(c * Q, Q), Q), :]
            return acc + _fold_rows((blk >= thr).astype(I32), jnp.add)
        acc = lax.fori_loop(0, nch, body, jnp.zeros((SUBLANE, Q), I32))
        return jnp.sum(acc, axis=0, keepdims=True)

    c0 = count_ge(jnp.zeros((1, Q), I32))
    has_k = c0 >= IDX_TOPK
    thr = jnp.where(has_k, 0, INT_MIN).astype(I32)
    cnt = jnp.where(has_k, c0, nch * Q)
    reachable = (lax.broadcasted_iota(I32, (1, Q), 1) + t0 + 1) >= IDX_TOPK

    def unsettled(cnt):
        return jnp.max(jnp.where((cnt != IDX_TOPK) & reachable, 1, 0)) > 0

    def bisect(it, state):
        thr, cnt = state
        cand = thr + lax.shift_left(jnp.int32(1), 30 - it)
        c = count_ge(cand)
        ok = c >= IDX_TOPK
        return jnp.where(ok, cand, thr), jnp.where(ok, c, cnt)

    thr, cnt = lax.fori_loop(0, 31, bisect, (thr, cnt))
    thr = jnp.maximum(thr, INT_MIN + 1)

    nbits = max(1, int(math.ceil(math.log2(ka_ref.shape[1] + 1))))

    def tie_cut(thr):
        need = IDX_TOPK - count_ge(thr + 1)

        def count_tie_below(bound):
            def body(c, acc):
                blk = key_s[pl.ds(pl.multiple_of(c * Q, Q), Q), :]
                hit = (blk == thr) & ((krow + c * Q) < bound)
                return acc + _fold_rows(hit.astype(I32), jnp.add)
            acc = lax.fori_loop(0, nch, body, jnp.zeros((SUBLANE, Q), I32))
            return jnp.sum(acc, axis=0, keepdims=True)

        def tie_bisect(it, p0):
            cand = p0 + lax.shift_left(jnp.int32(1), nbits - 1 - it)
            return jnp.where(count_tie_below(cand) < need, cand, p0)

        return lax.fori_loop(0, nbits, tie_bisect, jnp.zeros((1, Q), I32)) + 1

    pcut = lax.cond(unsettled(cnt), tie_cut, lambda thr: jnp.full((1, Q), 2 ** nbits, I32), thr)

    def mask_chunk(c, carry):
        off = pl.multiple_of(c * Q, Q)
        key = key_s[pl.ds(off, Q), :]
        sel = (key > thr) | ((key == thr) & ((krow + c * Q) < pcut))
        key_s[pl.ds(off, Q), :] = pltpu.bitcast(jnp.where(sel, 0.0, NEG_BIG).astype(F32), I32)
        return carry

    lax.fori_loop(0, nch, mask_chunk, 0)

    HB = ATT_HEADS

    def head_group(hg, carry):
        qlts = []
        for hh in range(HB):
            h = hg * HB + hh
            qh = qt_s[pl.ds(pl.multiple_of(h * A_HEAD_DIM, A_HEAD_DIM), A_HEAD_DIM), :]
            qlts.append((_dot(wuk_ref[h], qh) * (A_HEAD_DIM ** -0.5)).astype(BF16))
        qlt = jnp.concatenate(qlts, axis=1)

        def logits_chunk(c, m_acc, band):
            off = pl.multiple_of(c * Q, Q)
            lg = _dot(c_ref[0, pl.ds(off, Q), :], qlt)
            mask = pltpu.bitcast(key_s[pl.ds(off, Q), :], F32)
            new = []
            for hh in range(HB):
                lgh = lg[:, hh * Q:(hh + 1) * Q] + mask
                if band is not None:
                    lgh = lgh + bias_ref[hg * HB + hh, band]
                lg_s[pl.ds(off, Q), hh * Q:(hh + 1) * Q] = lgh
                new.append(jnp.maximum(m_acc[:, hh * Q:(hh + 1) * Q], _fold_rows(lgh, jnp.maximum)))
            return jnp.concatenate(new, axis=1)

        m_acc = jnp.full((SUBLANE, HB * Q), NEG_BIG, F32)
        m_acc = lax.fori_loop(0, nch - 2, lambda c, m: logits_chunk(c, m, None), m_acc)
        m_acc = lax.cond(nch >= 2, lambda m: logits_chunk(nch - 2, m, 1), lambda m: m, m_acc)
        m_acc = logits_chunk(nch - 1, m_acc, 0)
        m = jnp.max(m_acc, axis=0, keepdims=True)

        olat_s[...] = jnp.zeros_like(olat_s)

        def pv_chunk(c, s_acc):
            off = pl.multiple_of(c * Q, Q)
            p = jnp.exp(lg_s[pl.ds(off, Q), :] - m)
            olat_s[...] += _dot(ct_ref[0, c], p.astype(BF16))
            return s_acc + _fold_rows(p, jnp.add)

        s_acc = lax.fori_loop(0, nch, pv_chunk, jnp.zeros((SUBLANE, HB * Q), F32))
        s = jnp.sum(s_acc, axis=0, keepdims=True)
        for hh in range(HB):
            h = hg * HB + hh
            ol = olat_s[:, hh * Q:(hh + 1) * Q].astype(BF16)
            oh = _dot(wuvt_ref[h], ol) / s[:, hh * Q:(hh + 1) * Q]
            ot_s[pl.ds(pl.multiple_of(h * A_HEAD_DIM, A_HEAD_DIM), A_HEAD_DIM), :] = oh
        return carry

    lax.fori_loop(0, heads // HB, head_group, 0)
    o_ref[0] = ot_s[...].T.astype(BF16)


def _dsa(q, iq, iw, ka, c, ct, wuk, wuvt, bias):
    b, s, w = q.shape
    heads = wuk.shape[0]
    nq = s // ATT_Q
    assert heads % ATT_HEADS == 0
    once = pl.Buffered(1)
    full = lambda a: pl.BlockSpec(a.shape, lambda bi, i: (0,) * a.ndim, pipeline_mode=once)
    blk = lambda width: pl.BlockSpec((1, ATT_Q, width), lambda bi, i: (bi, i, 0))
    per_b = lambda width: pl.BlockSpec((1, s, width), lambda bi, i: (bi, 0, 0), pipeline_mode=once)
    return pl.pallas_call(
        functools.partial(_dsa_kernel, heads=heads),
        grid=(b, nq),
        in_specs=[
            blk(w), blk(iq.shape[2]), blk(iw.shape[2]),
            per_b(ka.shape[2]), per_b(c.shape[2]),
            pl.BlockSpec((1, nq, A_KV_RANK, ATT_Q), lambda bi, i: (bi, 0, 0, 0), pipeline_mode=once),
            full(wuk), full(wuvt), full(bias),
        ],
        out_specs=blk(w),
        out_shape=jax.ShapeDtypeStruct((b, s, w), BF16),
        scratch_shapes=[
            pltpu.VMEM((s, ATT_Q), I32),
            pltpu.VMEM((s, ATT_HEADS * ATT_Q), F32),
            pltpu.VMEM((IDX_HEADS, 2 * LANE, ATT_Q), BF16),
            pltpu.VMEM((w, ATT_Q), BF16),
            pltpu.VMEM((A_KV_RANK, ATT_HEADS * ATT_Q), F32),
            pltpu.VMEM((w, ATT_Q), F32),
        ],
        compiler_params=_cparams(("arbitrary", "arbitrary")),
    )(q, iq, iw, ka, c, ct, wuk, wuvt, bias)


def _out_kernel(x_ref, seq_ref, qm_ref, kv_ref, woa_ref, wob_ref, o_ref):
    acc = x_ref[...] + _dot(seq_ref[...], woa_ref[...])
    qm = qm_ref[...]
    kv = kv_ref[0]
    for h in range(MEM_HEADS):
        k = kv[:, h * LANE:(h + 1) * LANE]
        v = kv[:, (MEM_HEADS + h) * LANE:(MEM_HEADS + h + 1) * LANE]
        lg = _dot_nt(qm[:, h * LANE:(h + 1) * LANE], k) * (MEM_HEAD_DIM ** -0.5)
        p = jnp.exp(lg - jnp.max(lg, axis=-1, keepdims=True))
        oh = _dot(p.astype(BF16), v) / jnp.sum(p, axis=-1, keepdims=True)
        acc = acc + _dot(oh.astype(BF16), wob_ref[h])
    o_ref[...] = acc


def _out_proj(x2, seq2, qm2, kv, woa, wob, seq_len):
    n, d = x2.shape
    tt = PROJ_TOKENS
    per_seq = seq_len // tt
    full = lambda a: pl.BlockSpec(a.shape, lambda i: (0,) * a.ndim)
    row = lambda w: pl.BlockSpec((tt, w), lambda i: (i, 0))
    return pl.pallas_call(
        _out_kernel,
        grid=(n // tt,),
        in_specs=[row(d), row(seq2.shape[1]), row(qm2.shape[1]),
                  pl.BlockSpec((1,) + kv.shape[1:], lambda i: (i // per_seq, 0, 0)),
                  full(woa), full(wob)],
        out_specs=row(d),
        out_shape=jax.ShapeDtypeStruct((n, d), F32),
        compiler_params=_cparams(("arbitrary",)),
    )(x2, seq2, qm2, kv, woa, wob)


def _in_b_kernel(x_ref, g_ref, wz_ref, wx_ref, wm_ref, wdh_ref, wdl_ref,
                 z_ref, xbc_ref, dt_ref, qm_ref):
    h = _rms(x_ref[...], g_ref[...])
    h_hi, h_lo = _split2(h)
    z_ref[...] = _dot(h_hi, wz_ref[...]).astype(BF16)
    xbc_ref[...] = _dot(h_hi, wx_ref[...])
    qm_ref[...] = _dot(h_hi, wm_ref[...]).astype(BF16)
    wdh = wdh_ref[...]
    dt_ref[...] = _dot(h_hi, wdh) + _dot(h_lo, wdh) + _dot(h_hi, wdl_ref[...])


def _in_b(x2, g, wz, wx, wm, wdh, wdl):
    n, d = x2.shape
    tt = PROJ_TOKENS
    full = lambda a: pl.BlockSpec(a.shape, lambda i: (0,) * a.ndim)
    row = lambda w: pl.BlockSpec((tt, w), lambda i: (i, 0))
    return pl.pallas_call(
        _in_b_kernel,
        grid=(n // tt,),
        in_specs=[row(d), full(g), full(wz), full(wx), full(wm), full(wdh), full(wdl)],
        out_specs=[row(wz.shape[1]), row(wx.shape[1]), row(LANE), row(wm.shape[1])],
        out_shape=[jax.ShapeDtypeStruct((n, wz.shape[1]), BF16),
                   jax.ShapeDtypeStruct((n, wx.shape[1]), F32),
                   jax.ShapeDtypeStruct((n, LANE), F32),
                   jax.ShapeDtypeStruct((n, wm.shape[1]), BF16)],
        compiler_params=_cparams(("arbitrary",)),
    )(x2, g, wz, wx, wm, wdh, wdl)


def _ssd_kernel(xbc_ref, z_ref, dt_ref, cw_ref, cb_ref, dtb_ref, alog_ref, dsk_ref, on_ref,
                o_ref, tail_s, xpad_s, h_s, *, seq_width):
    L = SSD_CHUNK
    ci = pl.program_id(1)

    @pl.when(ci == 0)
    def _():
        tail_s[...] = jnp.zeros_like(tail_s)
        h_s[...] = jnp.zeros_like(h_s)

    xr = xbc_ref[0]
    xpad_s[0:SUBLANE, :] = tail_s[...]
    xpad_s[SUBLANE:SUBLANE + L, :] = xr
    tail_s[...] = xr[L - SUBLANE:L, :]
    conv = cb_ref[...] + jnp.zeros_like(xr)
    for j in range(CONV_WIDTH):
        off = SUBLANE - (CONV_WIDTH - 1) + j
        conv = conv + cw_ref[j:j + 1, :] * xpad_s[off:off + L, :]
    xc = conv * jax.nn.sigmoid(conv)
    gw = SSM_STATE
    bm = xc[:, seq_width:seq_width + SSM_GROUPS * gw]
    cm = xc[:, seq_width + SSM_GROUPS * gw:seq_width + 2 * SSM_GROUPS * gw]

    dt = jax.nn.softplus(dt_ref[0] + dtb_ref[...])
    a = -jnp.exp(alog_ref[...])
    da = dt * a
    row_i = lax.broadcasted_iota(I32, (L, L), 0)
    col_i = lax.broadcasted_iota(I32, (L, L), 1)
    tri = row_i >= col_i
    tri_b = tri.astype(BF16)
    d1 = da.astype(BF16)
    r1 = da - d1.astype(F32)
    d2 = r1.astype(BF16)
    d3 = (r1 - d2.astype(F32)).astype(BF16)
    acs = _dot(tri_b, d1) + _dot(tri_b, d2) + _dot(tri_b, d3)
    acs_t = acs.T
    a_last = acs[L - 1:L, :]
    e_acs = jnp.exp(acs)
    e_end = jnp.exp(a_last - acs)
    lane = lax.broadcasted_iota(I32, (L, LANE), 1)
    lo_half = lane < SSM_HEAD_DIM
    lane1 = lax.broadcasted_iota(I32, (1, LANE), 1)

    def pair_cols(m, r):
        return jnp.where(lo_half, m[:, r:r + 1], m[:, r + 1:r + 2])

    heads_per_group = (seq_width // SSM_HEAD_DIM) // SSM_GROUPS
    pairs_per_group = heads_per_group // 2
    pieces = []
    ssq = jnp.zeros((L, 1), F32)
    for g in range(SSM_GROUPS):
        bg = bm[:, g * gw:(g + 1) * gw]
        cg = cm[:, g * gw:(g + 1) * gw]
        bg_b = bg.astype(BF16)
        cg_b = cg.astype(BF16)
        bgt_b = bg.T.astype(BF16)
        cb = _dot_nt(cg_b, bg_b)
        for jp in range(pairs_per_group):
            r = g * heads_per_group + 2 * jp
            col0 = (g * pairs_per_group + jp) * LANE
            xs = xc[:, col0:col0 + LANE]
            xdt = (xs * pair_cols(dt, r)).astype(BF16)
            ys = []
            for rr in (r, r + 1):
                seg = acs[:, rr:rr + 1] - acs_t[rr:rr + 1, :]
                dec = jnp.exp(jnp.where(tri, seg, -jnp.inf))
                ys.append(_dot((cb * dec).astype(BF16), xdt))
            y = jnp.where(lo_half, ys[0], ys[1])
            hcol = jp * LANE
            h_prev = h_s[g, :, hcol:hcol + LANE]
            y = y + _dot(cg_b, h_prev.astype(BF16)) * pair_cols(e_acs, r)
            xw = (xs * pair_cols(dt * e_end, r)).astype(BF16)
            st = _dot(bgt_b, xw)
            cd = jnp.where(lane1 < SSM_HEAD_DIM, jnp.exp(a_last[:, r:r + 1]),
                           jnp.exp(a_last[:, r + 1:r + 2]))
            h_s[g, :, hcol:hcol + LANE] = h_prev * cd + st
            y = y + xs * dsk_ref[:, col0:col0 + LANE]
            zz = z_ref[0, :, col0:col0 + LANE].astype(F32)
            y = y * (zz * jax.nn.sigmoid(zz))
            ssq = ssq + jnp.sum(y * y, axis=-1, keepdims=True)
            pieces.append(y)
    scale = lax.rsqrt(ssq / seq_width + EPS)
    for k, y in enumerate(pieces):
        o_ref[0, :, k * LANE:(k + 1) * LANE] = (y * scale * on_ref[:, k * LANE:(k + 1) * LANE]).astype(BF16)


def _ssd(xbc, z, dt, cw, cb, dtb, alog, dsk, onorm):
    b, s, cdim = xbc.shape
    w = z.shape[2]
    L = SSD_CHUNK
    full = lambda a: pl.BlockSpec(a.shape, lambda bi, i: (0,) * a.ndim)
    blk = lambda width: pl.BlockSpec((1, L, width), lambda bi, i: (bi, i, 0))
    return pl.pallas_call(
        functools.partial(_ssd_kernel, seq_width=w),
        grid=(b, s // L),
        in_specs=[blk(cdim), blk(w), blk(LANE), full(cw), full(cb), full(dtb), full(alog),
                  full(dsk), full(onorm)],
        out_specs=blk(w),
        out_shape=jax.ShapeDtypeStruct((b, s, w), BF16),
        scratch_shapes=[
            pltpu.VMEM((SUBLANE, cdim), F32),
            pltpu.VMEM((SUBLANE + L, cdim), F32),
            pltpu.VMEM((SSM_GROUPS, SSM_STATE, w // SSM_GROUPS), F32),
        ],
        compiler_params=_cparams(("arbitrary", "arbitrary")),
    )(xbc, z, dt, cw, cb, dtb, alog, dsk, onorm)


def _top_rows(vals, k, val_out, idx_out, payload=None):
    rows = lax.broadcasted_iota(I32, vals.shape, 0)
    big = vals.shape[0]
    for j in range(k):
        m = jnp.max(_fold_rows(vals, jnp.maximum), axis=0, keepdims=True)
        am = jnp.min(_fold_rows(jnp.where(vals == m, rows, big), jnp.minimum), axis=0, keepdims=True)
        hit = rows == am
        val_out[j:j + 1, :] = m
        if payload is None:
            idx_out[j:j + 1, :] = am
        else:
            idx_out[j:j + 1, :] = jnp.sum(_fold_rows(jnp.where(hit, payload, 0), jnp.add),
                                          axis=0, keepdims=True)
        vals = jnp.where(hit, -jnp.inf, vals)


def _route_kernel(x_ref, g_ref, wqt_ref, sk_ref, hn_ref, e_ref, gate_ref,
                  ts_s, ti_s, bs_s, et_s, gt_s, cs_s, ci_s):
    T = ROUTE_TOKENS
    hn = _rms(x_ref[...], g_ref[...])
    hn_ref[...] = hn
    hb = hn.astype(BF16)
    qt = _dot_nt(wqt_ref[...], hb)
    half = qt.shape[0] // (PEER_HEADS * 2)
    for hd in range(PEER_HEADS):
        for side in range(2):
            r0 = (hd * 2 + side) * half
            sc = _dot(sk_ref[side], qt[r0:r0 + half].astype(BF16))
            _top_rows(sc, PEER_TOPK, ts_s.at[side], ti_s.at[side])
        off = 0
        for a in range(PEER_TOPK):
            nb = PEER_TOPK // (a + 1)
            cs_s[off:off + nb, :] = ts_s[0, a:a + 1, :] + ts_s[1, 0:nb, :]
            ci_s[off:off + nb, :] = ti_s[0, a:a + 1, :] * PEER_KEYS + ti_s[1, 0:nb, :]
            off += nb
        cs_s[off:, :] = jnp.full((cs_s.shape[0] - off, T), -jnp.inf, F32)
        ci_s[off:, :] = jnp.zeros((ci_s.shape[0] - off, T), I32)
        _top_rows(cs_s[...], PEER_TOPK, bs_s, et_s.at[pl.ds(hd * PEER_TOPK, PEER_TOPK)],
                  payload=ci_s[...])
        best = bs_s[...]
        p = jnp.exp(best - jnp.max(best, axis=0, keepdims=True))
        gt_s[hd * PEER_TOPK:(hd + 1) * PEER_TOPK, :] = p / jnp.sum(p, axis=0, keepdims=True)
    e_ref[...] = et_s[...].T
    gate_ref[...] = gt_s[...].T


def _route(x2, g, wqt, sk):
    n, d = x2.shape
    tt = ROUTE_TOKENS
    npair = PEER_HEADS * PEER_TOPK
    ncand = sum(PEER_TOPK // (a + 1) for a in range(PEER_TOPK))
    ncand = -(-ncand // SUBLANE) * SUBLANE
    full = lambda a: pl.BlockSpec(a.shape, lambda i: (0,) * a.ndim)
    row = lambda w: pl.BlockSpec((tt, w), lambda i: (i, 0))
    return pl.pallas_call(
        _route_kernel,
        grid=(n // tt,),
        in_specs=[row(d), full(g), full(wqt), full(sk)],
        out_specs=[row(d), row(npair), row(npair)],
        out_shape=[jax.ShapeDtypeStruct((n, d), F32),
                   jax.ShapeDtypeStruct((n, npair), I32),
                   jax.ShapeDtypeStruct((n, npair), F32)],
        scratch_shapes=[
            pltpu.VMEM((2, PEER_TOPK, tt), F32),
            pltpu.VMEM((2, PEER_TOPK, tt), I32),
            pltpu.VMEM((PEER_TOPK, tt), F32),
            pltpu.VMEM((npair, tt), I32),
            pltpu.VMEM((npair, tt), F32),
            pltpu.VMEM((ncand, tt), F32),
            pltpu.VMEM((ncand, tt), I32),
        ],
        compiler_params=_cparams(("arbitrary",)),
    )(x2, g, wqt, sk)


def _peer_consts(npair):
    rows = npair * SUBLANE
    sel = (lax.shift_right_logical(lax.broadcasted_iota(I32, (rows, npair), 0), 3)
           == lax.broadcasted_iota(I32, (rows, npair), 1)).astype(BF16)
    sel_t = (lax.broadcasted_iota(I32, (npair, rows), 0)
             == lax.shift_right_logical(lax.broadcasted_iota(I32, (npair, rows), 1), 3)).astype(BF16)
    diag = (lax.broadcasted_iota(I32, (SUBLANE, rows), 0)
            == (lax.broadcasted_iota(I32, (SUBLANE, rows), 1) & (SUBLANE - 1)))
    return sel, sel_t, diag


def _peer_group(load_rows, between, t0, gate_ref, hn_ref, x_ref, fn_ref, o_ref, consts, final_norm):
    sel, sel_t, diag = consts
    G = GATHER_GROUP
    rows = sel.shape[0]
    tok = lax.broadcasted_iota(I32, (G, rows), 0)
    partial = jnp.zeros((G, rows), F32)
    for j in range(G):
        between(j, 0)
        w = load_rows(j).reshape(rows, LANE)
        mu = pltpu.bitcast(lax.shift_left(w, jnp.uint32(16)), F32).astype(BF16)
        gj = _dot_nt(hn_ref[t0 + j].astype(BF16), mu)
        row = jnp.sum(jnp.where(diag, gj, 0.0), axis=0, keepdims=True)
        partial = jnp.where(tok == j, row, partial)
    dots = _dot(partial.astype(BF16), sel)
    gelu = 0.5 * dots * (1.0 + lax.erf(dots * (2.0 ** -0.5)))
    act = (gelu * gate_ref[t0:t0 + G, :]).astype(BF16)
    arep = _dot(act, sel_t)
    ys = []
    for j in range(G):
        between(j, 1)
        w = load_rows(j).reshape(rows, LANE)
        mv = pltpu.bitcast(w & jnp.uint32(0xFFFF0000), F32).astype(BF16)
        aexp = jnp.where(diag, arep[j:j + 1, :], 0.0).astype(BF16)
        y = _dot(aexp, mv) + x_ref[t0 + j]
        if final_norm:
            ssq = jnp.sum(jnp.sum(y * y, axis=1, keepdims=True), axis=0, keepdims=True)
            y = y * lax.rsqrt(ssq / (SUBLANE * LANE) + EPS) * fn_ref[...]
        ys.append(y)
    for j in range(G):
        o_ref[t0 + j] = ys[j]


def _expert_kernel(idx_ref, idxn_ref, gate_ref, hn_ref, x_ref, fn_ref, tab_ref, o_ref, buf, sem,
                   *, final_norm):
    T = GATHER_TOKENS
    G = GATHER_GROUP
    npair = idx_ref.shape[1]
    ngroups = T // G
    rows = npair * SUBLANE

    assert ngroups == GATHER_RING and GATHER_AHEAD < ngroups
    step = pl.program_id(0)
    half = npair // 2

    def issue(iref, g, j, part):
        for p in range(part * half, (part + 1) * half):
            e = iref[g * G + j, p]
            pltpu.make_async_copy(tab_ref.at[e], buf.at[g * G + j, p], sem.at[g]).start(priority=p % 2)

    def wait_group(g):
        for j in range(G):
            pltpu.make_async_copy(tab_ref.at[pl.ds(0, npair)], buf.at[g * G + j], sem.at[g]).wait()

    @pl.when(step == 0)
    def _():
        for g in range(GATHER_AHEAD):
            for j in range(G):
                issue(idx_ref, g, j, 0)
                issue(idx_ref, g, j, 1)

    consts = _peer_consts(npair)
    for g in range(ngroups):
        nxt = g + GATHER_AHEAD
        nref, ng = (idx_ref, nxt) if nxt < ngroups else (idxn_ref, nxt - ngroups)
        wait_group(g)
        _peer_group(lambda j: buf[g * G + j], lambda j, part: issue(nref, ng, j, part), g * G,
                    gate_ref, hn_ref, x_ref, fn_ref, o_ref, consts, final_norm)

    @pl.when(step == pl.num_programs(0) - 1)
    def _():
        for g in range(GATHER_AHEAD):
            wait_group(g)


def _experts(idx, gate, hn3, x3, fnorm3, table3, final_norm):
    n = x3.shape[0]
    tt = GATHER_TOKENS
    npair = idx.shape[1]
    tile = lambda: pl.BlockSpec((tt, SUBLANE, LANE), lambda i: (i, 0, 0))
    nsteps = n // tt
    return pl.pallas_call(
        functools.partial(_expert_kernel, final_norm=final_norm),
        grid=(nsteps,),
        in_specs=[
            pl.BlockSpec((tt, npair), lambda i: (i, 0), memory_space=pltpu.SMEM),
            pl.BlockSpec((tt, npair), lambda i: (jnp.minimum(i + 1, nsteps - 1), 0),
                         memory_space=pltpu.SMEM),
            pl.BlockSpec((tt, npair), lambda i: (i, 0)),
            tile(), tile(),
            pl.BlockSpec((SUBLANE, LANE), lambda i: (0, 0)),
            pl.BlockSpec(memory_space=pl.ANY),
        ],
        out_specs=tile(),
        out_shape=jax.ShapeDtypeStruct((n, SUBLANE, LANE), F32),
        scratch_shapes=[
            pltpu.VMEM((GATHER_RING * GATHER_GROUP, npair, SUBLANE, LANE), jnp.uint32),
            pltpu.SemaphoreType.DMA((GATHER_RING,)),
        ],
        compiler_params=_cparams(("arbitrary",)),
    )(idx, idx, gate, hn3, x3, fnorm3, table3)


def _pad_heads(w, heads, dim):
    d = w.shape[0]
    w = w.reshape(d, heads, dim)
    return jnp.pad(w, ((0, 0), (0, 0), (0, LANE - dim))).reshape(d, heads * LANE)


def _pack_table(u, v):
    ub = lax.bitcast_convert_type(u.astype(BF16), jnp.uint16).astype(jnp.uint32)
    vb = lax.bitcast_convert_type(v.astype(BF16), jnp.uint16).astype(jnp.uint32)
    return ub | (vb << 16)


def _split_w(w):
    hi = w.astype(BF16)
    return hi, (w - hi.astype(F32)).astype(BF16)


def kernel(x, mem, mem_norm, rel_bias, mix_norm, ffn_norm, final_norm, w_o, w_mem_kv, a_w_in,
           a_kv_norm, a_w_uk, a_w_uv, b_w_in, b_conv_w, b_conv_b, b_dt_bias, b_a_log, b_d_skip,
           b_out_norm, peer_w_q, peer_sub_keys, peer_u, peer_v):
    b, s, d = x.shape
    n = b * s
    depth = w_o.shape[0]
    mem_width = MEM_HEADS * MEM_HEAD_DIM
    seq_width = w_o.shape[1] - mem_width
    a_heads = seq_width // A_HEAD_DIM
    ssm_heads = seq_width // SSM_HEAD_DIM
    conv_dim = seq_width + 2 * SSM_GROUPS * SSM_STATE
    assert s % ATT_Q == 0 and s % PROJ_TOKENS == 0 and n % ROUTE_TOKENS == 0 and s >= 4 * IDX_TOPK
    assert d == SUBLANE * LANE and n % GATHER_TOKENS == 0

    wk, wv = w_mem_kv[:, :, :mem_width], w_mem_kv[:, :, mem_width:]
    w_kv_pad = jnp.concatenate(
        [jnp.stack([_pad_heads(wk[l], MEM_HEADS, MEM_HEAD_DIM) for l in range(depth)]),
         jnp.stack([_pad_heads(wv[l], MEM_HEADS, MEM_HEAD_DIM) for l in range(depth)])],
        axis=-1).astype(BF16)
    kv_all = _mem_kv(mem, mem_norm, w_kv_pad)

    x2 = x.reshape(n, d)
    for i in range(depth):
        j = i // 2
        woa = w_o[i, :seq_width].astype(BF16)
        wob = jnp.pad(w_o[i, seq_width:].reshape(MEM_HEADS, MEM_HEAD_DIM, d),
                      ((0, 0), (0, LANE - MEM_HEAD_DIM), (0, 0))).astype(BF16)
        g_mix = mix_norm[i].reshape(1, d)
        if i % 2 == 0:
            w_in = a_w_in[j]
            o0 = seq_width
            o1 = o0 + A_KV_RANK
            o2 = o1 + IDX_HEADS * IDX_DIM
            o3 = o2 + IDX_DIM
            o4 = o3 + IDX_HEADS
            wq = w_in[:, :o0].astype(BF16)
            wc = w_in[:, o0:o1].astype(BF16)
            wm = _pad_heads(w_in[:, o4:], MEM_HEADS, MEM_HEAD_DIM).astype(BF16)
            w_idx = jnp.concatenate(
                [w_in[:, o1:o2], w_in[:, o2:o3], w_in[:, o2:o3],
                 jnp.pad(w_in[:, o3:o4], ((0, 0), (0, LANE - IDX_HEADS)))], axis=1)
            wih, wil = _split_w(w_idx)
            q, c, ct, iq, ka, iw, qm = _in_a(x2, g_mix, wq, wc, wm, wih, wil,
                                             a_kv_norm[j].reshape(1, A_KV_RANK))
            bias = _bias_tiles(rel_bias)
            r3 = lambda t: t.reshape(b, s, t.shape[-1])
            seq = _dsa(r3(q), r3(iq), r3(iw), r3(ka), r3(c),
                       ct.reshape(b, s // ATT_Q, A_KV_RANK, ATT_Q),
                       a_w_uk[j].astype(BF16), jnp.swapaxes(a_w_uv[j], 1, 2).astype(BF16), bias)
        else:
            w_in = b_w_in[j]
            o0 = seq_width
            o1 = o0 + conv_dim
            o2 = o1 + ssm_heads
            wz = w_in[:, :o0].astype(BF16)
            wx = w_in[:, o0:o1].astype(BF16)
            wm = _pad_heads(w_in[:, o2:], MEM_HEADS, MEM_HEAD_DIM).astype(BF16)
            wdh, wdl = _split_w(jnp.pad(w_in[:, o1:o2], ((0, 0), (0, LANE - ssm_heads))))
            z, xbc, dt, qm = _in_b(x2, g_mix, wz, wx, wm, wdh, wdl)
            padh = lambda t: jnp.pad(t.reshape(1, ssm_heads), ((0, 0), (0, LANE - ssm_heads)))
            seq = _ssd(xbc.reshape(b, s, conv_dim), z.reshape(b, s, seq_width),
                       dt.reshape(b, s, LANE), b_conv_w[j], b_conv_b[j].reshape(1, conv_dim),
                       padh(b_dt_bias[j]), padh(b_a_log[j]),
                       jnp.repeat(b_d_skip[j], SSM_HEAD_DIM).reshape(1, seq_width),
                       b_out_norm[j].reshape(1, seq_width))
        x2 = _out_proj(x2, seq.reshape(n, seq_width), qm, kv_all[i], woa, wob, s)

        wqt = jnp.transpose(peer_w_q[i]).astype(BF16)
        hn, eidx, gate = _route(x2, ffn_norm[i].reshape(1, d), wqt, peer_sub_keys[i].astype(BF16))
        tile3 = lambda t: t.reshape(t.shape[0], SUBLANE, LANE)
        table3 = tile3(_pack_table(peer_u[i], peer_v[i]))
        x2 = _experts(eidx, gate, tile3(hn), tile3(x2), final_norm.reshape(SUBLANE, LANE), table3,
                      i == depth - 1).reshape(n, d)
    return x2.reshape(b, s, d)
```

```python
import functools
import math

import jax
import jax.numpy as jnp
from jax import lax
from jax.experimental import pallas as pl
from jax.experimental.pallas import tpu as pltpu

F32 = jnp.float32
BF16 = jnp.bfloat16
I32 = jnp.int32

EPS = 1e-6
MEM_HEADS = 4
MEM_HEAD_DIM = 64
A_HEAD_DIM = 64
A_KV_RANK = 256
IDX_HEADS = 8
IDX_DIM = 64
IDX_TOPK = 256
REL_BUCKETS = 32
REL_MAX_DIST = 128
SSM_HEAD_DIM = 64
SSM_GROUPS = 2
SSM_STATE = 128
CONV_WIDTH = 4
SSD_CHUNK = 128
PEER_HEADS = 8
PEER_KEYS = 128
PEER_TOPK = 16

LANE = 128
SUBLANE = 8
INT_MIN = -(2 ** 31)
NEG_BIG = -1e30

PROJ_TOKENS = 512
ATT_Q = 256
ATT_HEADS = 6
ROUTE_TOKENS = 256
GATHER_GROUP = 8
GATHER_RING = 4
GATHER_AHEAD = 3
GATHER_TOKENS = GATHER_GROUP * GATHER_RING
VMEM_LIMIT = 56 * 1024 * 1024


def _cparams(sem):
    return pltpu.CompilerParams(dimension_semantics=sem, vmem_limit_bytes=VMEM_LIMIT)


def _fold_rows(x, op):
    parts = [x[k * SUBLANE:(k + 1) * SUBLANE] for k in range(x.shape[0] // SUBLANE)]
    while len(parts) > 1:
        nxt = [op(parts[k], parts[k + 1]) for k in range(0, len(parts) - 1, 2)]
        if len(parts) % 2:
            nxt.append(parts[-1])
        parts = nxt
    return parts[0]


def _rms(x, g):
    return x * lax.rsqrt(jnp.mean(x * x, axis=-1, keepdims=True) + EPS) * g


def _split2(a):
    hi = a.astype(BF16)
    lo = (a - hi.astype(F32)).astype(BF16)
    return hi, lo


def _dot(a, b):
    return jnp.dot(a, b, preferred_element_type=F32)


def _dot_nt(a, b):
    return lax.dot_general(a, b, (((1,), (1,)), ((), ())), preferred_element_type=F32)


def _mem_kv_kernel(mem_ref, g_ref, w_ref, out_ref):
    y = _rms(mem_ref[0], g_ref[...])
    out_ref[0, 0] = _dot(y.astype(BF16), w_ref[0]).astype(BF16)


def _mem_kv(mem, mem_norm, w_pad):
    b, m, d = mem.shape
    depth, _, wcols = w_pad.shape
    return pl.pallas_call(
        _mem_kv_kernel,
        grid=(depth, b),
        in_specs=[
            pl.BlockSpec((1, m, d), lambda l, i: (i, 0, 0)),
            pl.BlockSpec((1, d), lambda l, i: (0, 0)),
            pl.BlockSpec((1, d, wcols), lambda l, i: (l, 0, 0)),
        ],
        out_specs=pl.BlockSpec((1, 1, m, wcols), lambda l, i: (l, i, 0, 0)),
        out_shape=jax.ShapeDtypeStruct((depth, b, m, wcols), BF16),
        compiler_params=_cparams(("arbitrary", "arbitrary")),
    )(mem, mem_norm.reshape(1, d), w_pad)


def _bias_kernel(rb_ref, out_ref):
    h = pl.program_id(0)
    max_exact = REL_BUCKETS // 2
    far = rb_ref[REL_BUCKETS - 1, h]
    krow = lax.broadcasted_iota(I32, (ATT_Q, ATT_Q), 0)
    qcol = lax.broadcasted_iota(I32, (ATT_Q, ATT_Q), 1)
    for r in range(2):
        dist = qcol - krow + ATT_Q * r
        n = jnp.maximum(dist, 0)
        nf = jnp.maximum(n, max_exact).astype(F32)
        large = max_exact + (jnp.log(nf / max_exact) / math.log(REL_MAX_DIST / max_exact)
                             * (REL_BUCKETS - max_exact)).astype(I32)
        large = jnp.minimum(large, REL_BUCKETS - 1)
        bucket = jnp.where(n < max_exact, n, large)
        acc = jnp.zeros((ATT_Q, ATT_Q), F32)
        for k in range(REL_BUCKETS):
            acc = jnp.where(bucket == k, rb_ref[k, h], acc)
        out_ref[0, r] = acc - far


def _bias_tiles(rel_bias):
    heads = rel_bias.shape[1]
    return pl.pallas_call(
        _bias_kernel,
        grid=(heads,),
        in_specs=[pl.BlockSpec(memory_space=pltpu.SMEM)],
        out_specs=pl.BlockSpec((1, 2, ATT_Q, ATT_Q), lambda h: (h, 0, 0, 0)),
        out_shape=jax.ShapeDtypeStruct((heads, 2, ATT_Q, ATT_Q), F32),
        compiler_params=_cparams(("arbitrary",)),
    )(rel_bias)


def _in_a_kernel(x_ref, g_ref, wq_ref, wc_ref, wm_ref, wih_ref, wil_ref, kvn_ref,
                 q_ref, c_ref, ct_ref, iq_ref, ka_ref, iw_ref, qm_ref):
    h = _rms(x_ref[...], g_ref[...])
    h_hi, h_lo = _split2(h)
    q_ref[...] = _dot(h_hi, wq_ref[...]).astype(BF16)
    qm_ref[...] = _dot(h_hi, wm_ref[...]).astype(BF16)
    c = _rms(_dot(h_hi, wc_ref[...]), kvn_ref[...])
    c_ref[...] = c.astype(BF16)
    for j in range(PROJ_TOKENS // ATT_Q):
        ct_ref[j] = c[j * ATT_Q:(j + 1) * ATT_Q].T.astype(BF16)
    wih = wih_ref[...]
    ii = _dot(h_hi, wih) + _dot(h_lo, wih) + _dot(h_hi, wil_ref[...])
    iq_ref[...] = ii[:, :IDX_HEADS * IDX_DIM]
    kk = ii[:, IDX_HEADS * IDX_DIM:IDX_HEADS * IDX_DIM + LANE]
    kk_hi, kk_lo = _split2(kk)
    lane = lax.broadcasted_iota(I32, kk.shape, 1)
    half = jnp.where(lane < IDX_DIM, kk_hi, kk_lo)
    ka_ref[...] = jnp.concatenate([half, half], axis=1)
    iw_ref[...] = ii[:, IDX_HEADS * IDX_DIM + LANE:]


def _in_a(x2, g, wq, wc, wm, wih, wil, kvn):
    n, d = x2.shape
    tt = PROJ_TOKENS
    full = lambda a: pl.BlockSpec(a.shape, lambda i: (0,) * a.ndim)
    row = lambda w: pl.BlockSpec((tt, w), lambda i: (i, 0))
    nblk = tt // ATT_Q
    outs = [
        jax.ShapeDtypeStruct((n, wq.shape[1]), BF16),
        jax.ShapeDtypeStruct((n, A_KV_RANK), BF16),
        jax.ShapeDtypeStruct((n // ATT_Q, A_KV_RANK, ATT_Q), BF16),
        jax.ShapeDtypeStruct((n, IDX_HEADS * IDX_DIM), F32),
        jax.ShapeDtypeStruct((n, 2 * LANE), BF16),
        jax.ShapeDtypeStruct((n, LANE), F32),
        jax.ShapeDtypeStruct((n, wm.shape[1]), BF16),
    ]
    out_specs = [
        row(wq.shape[1]), row(A_KV_RANK),
        pl.BlockSpec((nblk, A_KV_RANK, ATT_Q), lambda i: (i, 0, 0)),
        row(IDX_HEADS * IDX_DIM), row(2 * LANE), row(LANE), row(wm.shape[1]),
    ]
    return pl.pallas_call(
        _in_a_kernel,
        grid=(n // tt,),
        in_specs=[row(d), full(g), full(wq), full(wc), full(wm), full(wih), full(wil), full(kvn)],
        out_specs=out_specs,
        out_shape=outs,
        compiler_params=_cparams(("arbitrary",)),
    )(x2, g, wq, wc, wm, wih, wil, kvn)


def _dsa_kernel(q_ref, iq_ref, iw_ref, ka_ref, c_ref, ct_ref, wuk_ref, wuvt_ref, bias_ref,
                o_ref, key_s, lg_s, qbt_s, qt_s, olat_s, ot_s, *, heads):
    i = pl.program_id(1)
    nch = i + 1
    t0 = i * ATT_Q
    Q = ATT_Q
    krow = lax.broadcasted_iota(I32, (Q, Q), 0)
    qcol = lax.broadcasted_iota(I32, (Q, Q), 1)

    iqv = iq_ref[0]
    lane = lax.broadcasted_iota(I32, (Q, LANE), 1)
    for j in range(IDX_HEADS // 2):
        v = iqv[:, j * LANE:(j + 1) * LANE]
        r = pltpu.roll(v, IDX_DIM, 1)
        for hh, dup in ((2 * j, jnp.where(lane < IDX_DIM, v, r)),
                        (2 * j + 1, jnp.where(lane < IDX_DIM, r, v))):
            hi = dup.astype(BF16)
            lo = (dup - hi.astype(F32)).astype(BF16)
            qbt_s[hh, 0:LANE, :] = hi.astype(F32).T.astype(BF16)
            qbt_s[hh, LANE:2 * LANE, :] = lo.astype(F32).T.astype(BF16)
    wt = iw_ref[0].T * (IDX_HEADS ** -0.5)
    qt_s[...] = q_ref[0].astype(F32).T.astype(BF16)

    def score_chunk(c, carry):
        ka = ka_ref[0, pl.ds(pl.multiple_of(c * Q, Q), Q), :]
        acc = jnp.zeros((Q, Q), F32)
        for hh in range(IDX_HEADS):
            z = _dot(ka, qbt_s[hh])
            acc = acc + jnp.maximum(z, 0.0) * wt[hh:hh + 1, :]
        acc = acc * (IDX_DIM ** -0.5)
        bits = pltpu.bitcast(acc, I32)
        skey = jnp.where(bits < 0, bits ^ 0x7FFFFFFF, bits)
        causal = (krow + c * Q) <= (qcol + t0)
        key_s[pl.ds(pl.multiple_of(c * Q, Q), Q), :] = jnp.where(causal, skey, INT_MIN)
        return carry

    lax.fori_loop(0, nch, score_chunk, 0)

    def count_ge(thr):
        def body(c, acc):
            blk = key_s[pl.ds(pl.multiple_of(c * Q, Q), Q), :]
            return acc + _fold_rows((blk >= thr).astype(I32), jnp.add)
        acc = lax.fori_loop(0, nch, body, jnp.zeros((SUBLANE, Q), I32))
        return jnp.sum(acc, axis=0, keepdims=True)

    c0 = count_ge(jnp.zeros((1, Q), I32))
    has_k = c0 >= IDX_TOPK
    thr = jnp.where(has_k, 0, INT_MIN).astype(I32)
    cnt = jnp.where(has_k, c0, nch * Q)
    reachable = (lax.broadcasted_iota(I32, (1, Q), 1) + t0 + 1) >= IDX_TOPK

    def unsettled(cnt):
        return jnp.max(jnp.where((cnt != IDX_TOPK) & reachable, 1, 0)) > 0

    def bisect(it, state):
        thr, cnt = state
        cand = thr + lax.shift_left(jnp.int32(1), 30 - it)
        c = count_ge(cand)
        ok = c >= IDX_TOPK
        return jnp.where(ok, cand, thr), jnp.where(ok, c, cnt)

    thr, cnt = lax.fori_loop(0, 31, bisect, (thr, cnt))
    thr = jnp.maximum(thr, INT_MIN + 1)

    nbits = max(1, int(math.ceil(math.log2(ka_ref.shape[1] + 1))))

    def tie_cut(thr):
        need = IDX_TOPK - count_ge(thr + 1)

        def count_tie_below(bound):
            def body(c, acc):
                blk = key_s[pl.ds(pl.multiple_of(c * Q, Q), Q), :]
                hit = (blk == thr) & ((krow + c * Q) < bound)
                return acc + _fold_rows(hit.astype(I32), jnp.add)
            acc = lax.fori_loop(0, nch, body, jnp.zeros((SUBLANE, Q), I32))
            return jnp.sum(acc, axis=0, keepdims=True)

        def tie_bisect(it, p0):
            cand = p0 + lax.shift_left(jnp.int32(1), nbits - 1 - it)
            return jnp.where(count_tie_below(cand) < need, cand, p0)

        return lax.fori_loop(0, nbits, tie_bisect, jnp.zeros((1, Q), I32)) + 1

    pcut = lax.cond(unsettled(cnt), tie_cut, lambda thr: jnp.full((1, Q), 2 ** nbits, I32), thr)

    def mask_chunk(c, carry):
        off = pl.multiple_of(c * Q, Q)
        key = key_s[pl.ds(off, Q), :]
        sel = (key > thr) | ((key == thr) & ((krow + c * Q) < pcut))
        key_s[pl.ds(off, Q), :] = pltpu.bitcast(jnp.where(sel, 0.0, NEG_BIG).astype(F32), I32)
        return carry

    lax.fori_loop(0, nch, mask_chunk, 0)

    HB = ATT_HEADS

    def head_group(hg, carry):
        qlts = []
        for hh in range(HB):
            h = hg * HB + hh
            qh = qt_s[pl.ds(pl.multiple_of(h * A_HEAD_DIM, A_HEAD_DIM), A_HEAD_DIM), :]
            qlts.append((_dot(wuk_ref[h], qh) * (A_HEAD_DIM ** -0.5)).astype(BF16))
        qlt = jnp.concatenate(qlts, axis=1)

        def logits_chunk(c, m_acc, band):
            off = pl.multiple_of(c * Q, Q)
            lg = _dot(c_ref[0, pl.ds(off, Q), :], qlt)
            mask = pltpu.bitcast(key_s[pl.ds(off, Q), :], F32)
            new = []
            for hh in range(HB):
                lgh = lg[:, hh * Q:(hh + 1) * Q] + mask
                if band is not None:
                    lgh = lgh + bias_ref[hg * HB + hh, band]
                lg_s[pl.ds(off, Q), hh * Q:(hh + 1) * Q] = lgh
                new.append(jnp.maximum(m_acc[:, hh * Q:(hh + 1) * Q], _fold_rows(lgh, jnp.maximum)))
            return jnp.concatenate(new, axis=1)

        m_acc = jnp.full((SUBLANE, HB * Q), NEG_BIG, F32)
        m_acc = lax.fori_loop(0, nch - 2, lambda c, m: logits_chunk(c, m, None), m_acc)
        m_acc = lax.cond(nch >= 2, lambda m: logits_chunk(nch - 2, m, 1), lambda m: m, m_acc)
        m_acc = logits_chunk(nch - 1, m_acc, 0)
        m = jnp.max(m_acc, axis=0, keepdims=True)

        olat_s[...] = jnp.zeros_like(olat_s)

        def pv_chunk(c, s_acc):
            off = pl.multiple_of(c * Q, Q)
            p = jnp.exp(lg_s[pl.ds(off, Q), :] - m)
            olat_s[...] += _dot(ct_ref[0, c], p.astype(BF16))
            return s_acc + _fold_rows(p, jnp.add)

        s_acc = lax.fori_loop(0, nch, pv_chunk, jnp.zeros((SUBLANE, HB * Q), F32))
        s = jnp.sum(s_acc, axis=0, keepdims=True)
        for hh in range(HB):
            h = hg * HB + hh
            ol = olat_s[:, hh * Q:(hh + 1) * Q].astype(BF16)
            oh = _dot(wuvt_ref[h], ol) / s[:, hh * Q:(hh + 1) * Q]
            ot_s[pl.ds(pl.multiple_of(h * A_HEAD_DIM, A_HEAD_DIM), A_HEAD_DIM), :] = oh
        return carry

    lax.fori_loop(0, heads // HB, head_group, 0)
    o_ref[0] = ot_s[...].T.astype(BF16)


def _dsa(q, iq, iw, ka, c, ct, wuk, wuvt, bias):
    b, s, w = q.shape
    heads = wuk.shape[0]
    nq = s // ATT_Q
    assert heads % ATT_HEADS == 0
    once = pl.Buffered(1)
    full = lambda a: pl.BlockSpec(a.shape, lambda bi, i: (0,) * a.ndim, pipeline_mode=once)
    blk = lambda width: pl.BlockSpec((1, ATT_Q, width), lambda bi, i: (bi, i, 0))
    per_b = lambda width: pl.BlockSpec((1, s, width), lambda bi, i: (bi, 0, 0), pipeline_mode=once)
    return pl.pallas_call(
        functools.partial(_dsa_kernel, heads=heads),
        grid=(b, nq),
        in_specs=[
            blk(w), blk(iq.shape[2]), blk(iw.shape[2]),
            per_b(ka.shape[2]), per_b(c.shape[2]),
            pl.BlockSpec((1, nq, A_KV_RANK, ATT_Q), lambda bi, i: (bi, 0, 0, 0), pipeline_mode=once),
            full(wuk), full(wuvt), full(bias),
        ],
        out_specs=blk(w),
        out_shape=jax.ShapeDtypeStruct((b, s, w), BF16),
        scratch_shapes=[
            pltpu.VMEM((s, ATT_Q), I32),
            pltpu.VMEM((s, ATT_HEADS * ATT_Q), F32),
            pltpu.VMEM((IDX_HEADS, 2 * LANE, ATT_Q), BF16),
            pltpu.VMEM((w, ATT_Q), BF16),
            pltpu.VMEM((A_KV_RANK, ATT_HEADS * ATT_Q), F32),
            pltpu.VMEM((w, ATT_Q), F32),
        ],
        compiler_params=_cparams(("arbitrary", "arbitrary")),
    )(q, iq, iw, ka, c, ct, wuk, wuvt, bias)


def _out_kernel(x_ref, seq_ref, qm_ref, kv_ref, woa_ref, wob_ref, o_ref):
    acc = x_ref[...] + _dot(seq_ref[...], woa_ref[...])
    qm = qm_ref[...]
    kv = kv_ref[0]
    for h in range(MEM_HEADS):
        k = kv[:, h * LANE:(h + 1) * LANE]
        v = kv[:, (MEM_HEADS + h) * LANE:(MEM_HEADS + h + 1) * LANE]
        lg = _dot_nt(qm[:, h * LANE:(h + 1) * LANE], k) * (MEM_HEAD_DIM ** -0.5)
        p = jnp.exp(lg - jnp.max(lg, axis=-1, keepdims=True))
        oh = _dot(p.astype(BF16), v) / jnp.sum(p, axis=-1, keepdims=True)
        acc = acc + _dot(oh.astype(BF16), wob_ref[h])
    o_ref[...] = acc


def _out_proj(x2, seq2, qm2, kv, woa, wob, seq_len):
    n, d = x2.shape
    tt = PROJ_TOKENS
    per_seq = seq_len // tt
    full = lambda a: pl.BlockSpec(a.shape, lambda i: (0,) * a.ndim)
    row = lambda w: pl.BlockSpec((tt, w), lambda i: (i, 0))
    return pl.pallas_call(
        _out_kernel,
        grid=(n // tt,),
        in_specs=[row(d), row(seq2.shape[1]), row(qm2.shape[1]),
                  pl.BlockSpec((1,) + kv.shape[1:], lambda i: (i // per_seq, 0, 0)),
                  full(woa), full(wob)],
        out_specs=row(d),
        out_shape=jax.ShapeDtypeStruct((n, d), F32),
        compiler_params=_cparams(("arbitrary",)),
    )(x2, seq2, qm2, kv, woa, wob)


def _in_b_kernel(x_ref, g_ref, wz_ref, wx_ref, wm_ref, wdh_ref, wdl_ref,
                 z_ref, xbc_ref, dt_ref, qm_ref):
    h = _rms(x_ref[...], g_ref[...])
    h_hi, h_lo = _split2(h)
    z_ref[...] = _dot(h_hi, wz_ref[...]).astype(BF16)
    xbc_ref[...] = _dot(h_hi, wx_ref[...])
    qm_ref[...] = _dot(h_hi, wm_ref[...]).astype(BF16)
    wdh = wdh_ref[...]
    dt_ref[...] = _dot(h_hi, wdh) + _dot(h_lo, wdh) + _dot(h_hi, wdl_ref[...])


def _in_b(x2, g, wz, wx, wm, wdh, wdl):
    n, d = x2.shape
    tt = PROJ_TOKENS
    full = lambda a: pl.BlockSpec(a.shape, lambda i: (0,) * a.ndim)
    row = lambda w: pl.BlockSpec((tt, w), lambda i: (i, 0))
    return pl.pallas_call(
        _in_b_kernel,
        grid=(n // tt,),
        in_specs=[row(d), full(g), full(wz), full(wx), full(wm), full(wdh), full(wdl)],
        out_specs=[row(wz.shape[1]), row(wx.shape[1]), row(LANE), row(wm.shape[1])],
        out_shape=[jax.ShapeDtypeStruct((n, wz.shape[1]), BF16),
                   jax.ShapeDtypeStruct((n, wx.shape[1]), F32),
                   jax.ShapeDtypeStruct((n, LANE), F32),
                   jax.ShapeDtypeStruct((n, wm.shape[1]), BF16)],
        compiler_params=_cparams(("arbitrary",)),
    )(x2, g, wz, wx, wm, wdh, wdl)


def _ssd_kernel(xbc_ref, z_ref, dt_ref, cw_ref, cb_ref, dtb_ref, alog_ref, dsk_ref, on_ref,
                o_ref, tail_s, xpad_s, h_s, *, seq_width):
    L = SSD_CHUNK
    ci = pl.program_id(1)

    @pl.when(ci == 0)
    def _():
        tail_s[...] = jnp.zeros_like(tail_s)
        h_s[...] = jnp.zeros_like(h_s)

    xr = xbc_ref[0]
    xpad_s[0:SUBLANE, :] = tail_s[...]
    xpad_s[SUBLANE:SUBLANE + L, :] = xr
    tail_s[...] = xr[L - SUBLANE:L, :]
    conv = cb_ref[...] + jnp.zeros_like(xr)
    for j in range(CONV_WIDTH):
        off = SUBLANE - (CONV_WIDTH - 1) + j
        conv = conv + cw_ref[j:j + 1, :] * xpad_s[off:off + L, :]
    xc = conv * jax.nn.sigmoid(conv)
    gw = SSM_STATE
    bm = xc[:, seq_width:seq_width + SSM_GROUPS * gw]
    cm = xc[:, seq_width + SSM_GROUPS * gw:seq_width + 2 * SSM_GROUPS * gw]

    dt = jax.nn.softplus(dt_ref[0] + dtb_ref[...])
    a = -jnp.exp(alog_ref[...])
    da = dt * a
    row_i = lax.broadcasted_iota(I32, (L, L), 0)
    col_i = lax.broadcasted_iota(I32, (L, L), 1)
    tri = row_i >= col_i
    tri_b = tri.astype(BF16)
    d1 = da.astype(BF16)
    r1 = da - d1.astype(F32)
    d2 = r1.astype(BF16)
    d3 = (r1 - d2.astype(F32)).astype(BF16)
    acs = _dot(tri_b, d1) + _dot(tri_b, d2) + _dot(tri_b, d3)
    acs_t = acs.T
    a_last = acs[L - 1:L, :]
    e_acs = jnp.exp(acs)
    e_end = jnp.exp(a_last - acs)
    lane = lax.broadcasted_iota(I32, (L, LANE), 1)
    lo_half = lane < SSM_HEAD_DIM
    lane1 = lax.broadcasted_iota(I32, (1, LANE), 1)

    def pair_cols(m, r):
        return jnp.where(lo_half, m[:, r:r + 1], m[:, r + 1:r + 2])

    heads_per_group = (seq_width // SSM_HEAD_DIM) // SSM_GROUPS
    pairs_per_group = heads_per_group // 2
    pieces = []
    ssq = jnp.zeros((L, 1), F32)
    for g in range(SSM_GROUPS):
        bg = bm[:, g * gw:(g + 1) * gw]
        cg = cm[:, g * gw:(g + 1) * gw]
        bg_b = bg.astype(BF16)
        cg_b = cg.astype(BF16)
        bgt_b = bg.T.astype(BF16)
        cb = _dot_nt(cg_b, bg_b)
        for jp in range(pairs_per_group):
            r = g * heads_per_group + 2 * jp
            col0 = (g * pairs_per_group + jp) * LANE
            xs = xc[:, col0:col0 + LANE]
            xdt = (xs * pair_cols(dt, r)).astype(BF16)
            ys = []
            for rr in (r, r + 1):
                seg = acs[:, rr:rr + 1] - acs_t[rr:rr + 1, :]
                dec = jnp.exp(jnp.where(tri, seg, -jnp.inf))
                ys.append(_dot((cb * dec).astype(BF16), xdt))
            y = jnp.where(lo_half, ys[0], ys[1])
            hcol = jp * LANE
            h_prev = h_s[g, :, hcol:hcol + LANE]
            y = y + _dot(cg_b, h_prev.astype(BF16)) * pair_cols(e_acs, r)
            xw = (xs * pair_cols(dt * e_end, r)).astype(BF16)
            st = _dot(bgt_b, xw)
            cd = jnp.where(lane1 < SSM_HEAD_DIM, jnp.exp(a_last[:, r:r + 1]),
                           jnp.exp(a_last[:, r + 1:r + 2]))
            h_s[g, :, hcol:hcol + LANE] = h_prev * cd + st
            y = y + xs * dsk_ref[:, col0:col0 + LANE]
            zz = z_ref[0, :, col0:col0 + LANE].astype(F32)
            y = y * (zz * jax.nn.sigmoid(zz))
            ssq = ssq + jnp.sum(y * y, axis=-1, keepdims=True)
            pieces.append(y)
    scale = lax.rsqrt(ssq / seq_width + EPS)
    for k, y in enumerate(pieces):
        o_ref[0, :, k * LANE:(k + 1) * LANE] = (y * scale * on_ref[:, k * LANE:(k + 1) * LANE]).astype(BF16)


def _ssd(xbc, z, dt, cw, cb, dtb, alog, dsk, onorm):
    b, s, cdim = xbc.shape
    w = z.shape[2]
    L = SSD_CHUNK
    full = lambda a: pl.BlockSpec(a.shape, lambda bi, i: (0,) * a.ndim)
    blk = lambda width: pl.BlockSpec((1, L, width), lambda bi, i: (bi, i, 0))
    return pl.pallas_call(
        functools.partial(_ssd_kernel, seq_width=w),
        grid=(b, s // L),
        in_specs=[blk(cdim), blk(w), blk(LANE), full(cw), full(cb), full(dtb), full(alog),
                  full(dsk), full(onorm)],
        out_specs=blk(w),
        out_shape=jax.ShapeDtypeStruct((b, s, w), BF16),
        scratch_shapes=[
            pltpu.VMEM((SUBLANE, cdim), F32),
            pltpu.VMEM((SUBLANE + L, cdim), F32),
            pltpu.VMEM((SSM_GROUPS, SSM_STATE, w // SSM_GROUPS), F32),
        ],
        compiler_params=_cparams(("arbitrary", "arbitrary")),
    )(xbc, z, dt, cw, cb, dtb, alog, dsk, onorm)


def _top_rows(vals, k, val_out, idx_out, payload=None):
    rows = lax.broadcasted_iota(I32, vals.shape, 0)
    big = vals.shape[0]
    for j in range(k):
        m = jnp.max(_fold_rows(vals, jnp.maximum), axis=0, keepdims=True)
        am = jnp.min(_fold_rows(jnp.where(vals == m, rows, big), jnp.minimum), axis=0, keepdims=True)
        hit = rows == am
        val_out[j:j + 1, :] = m
        if payload is None:
            idx_out[j:j + 1, :] = am
        else:
            idx_out[j:j + 1, :] = jnp.sum(_fold_rows(jnp.where(hit, payload, 0), jnp.add),
                                          axis=0, keepdims=True)
        vals = jnp.where(hit, -jnp.inf, vals)


def _route_kernel(x_ref, g_ref, wqt_ref, sk_ref, hn_ref, e_ref, gate_ref,
                  ts_s, ti_s, bs_s, et_s, gt_s, cs_s, ci_s):
    T = ROUTE_TOKENS
    hn = _rms(x_ref[...], g_ref[...])
    hn_ref[...] = hn
    hb = hn.astype(BF16)
    qt = _dot_nt(wqt_ref[...], hb)
    half = qt.shape[0] // (PEER_HEADS * 2)
    for hd in range(PEER_HEADS):
        for side in range(2):
            r0 = (hd * 2 + side) * half
            sc = _dot(sk_ref[side], qt[r0:r0 + half].astype(BF16))
            _top_rows(sc, PEER_TOPK, ts_s.at[side], ti_s.at[side])
        off = 0
        for a in range(PEER_TOPK):
            nb = PEER_TOPK // (a + 1)
            cs_s[off:off + nb, :] = ts_s[0, a:a + 1, :] + ts_s[1, 0:nb, :]
            ci_s[off:off + nb, :] = ti_s[0, a:a + 1, :] * PEER_KEYS + ti_s[1, 0:nb, :]
            off += nb
        cs_s[off:, :] = jnp.full((cs_s.shape[0] - off, T), -jnp.inf, F32)
        ci_s[off:, :] = jnp.zeros((ci_s.shape[0] - off, T), I32)
        _top_rows(cs_s[...], PEER_TOPK, bs_s, et_s.at[pl.ds(hd * PEER_TOPK, PEER_TOPK)],
                  payload=ci_s[...])
        best = bs_s[...]
        p = jnp.exp(best - jnp.max(best, axis=0, keepdims=True))
        gt_s[hd * PEER_TOPK:(hd + 1) * PEER_TOPK, :] = p / jnp.sum(p, axis=0, keepdims=True)
    e_ref[...] = et_s[...].T
    gate_ref[...] = gt_s[...].T


def _route(x2, g, wqt, sk):
    n, d = x2.shape
    tt = ROUTE_TOKENS
    npair = PEER_HEADS * PEER_TOPK
    ncand = sum(PEER_TOPK // (a + 1) for a in range(PEER_TOPK))
    ncand = -(-ncand // SUBLANE) * SUBLANE
    full = lambda a: pl.BlockSpec(a.shape, lambda i: (0,) * a.ndim)
    row = lambda w: pl.BlockSpec((tt, w), lambda i: (i, 0))
    return pl.pallas_call(
        _route_kernel,
        grid=(n // tt,),
        in_specs=[row(d), full(g), full(wqt), full(sk)],
        out_specs=[row(d), row(npair), row(npair)],
        out_shape=[jax.ShapeDtypeStruct((n, d), F32),
                   jax.ShapeDtypeStruct((n, npair), I32),
                   jax.ShapeDtypeStruct((n, npair), F32)],
        scratch_shapes=[
            pltpu.VMEM((2, PEER_TOPK, tt), F32),
            pltpu.VMEM((2, PEER_TOPK, tt), I32),
            pltpu.VMEM((PEER_TOPK, tt), F32),
            pltpu.VMEM((npair, tt), I32),
            pltpu.VMEM((npair, tt), F32),
            pltpu.VMEM((ncand, tt), F32),
            pltpu.VMEM((ncand, tt), I32),
        ],
        compiler_params=_cparams(("arbitrary",)),
    )(x2, g, wqt, sk)


def _peer_consts(npair):
    rows = npair * SUBLANE
    sel = (lax.shift_right_logical(lax.broadcasted_iota(I32, (rows, npair), 0), 3)
           == lax.broadcasted_iota(I32, (rows, npair), 1)).astype(BF16)
    sel_t = (lax.broadcasted_iota(I32, (npair, rows), 0)
             == lax.shift_right_logical(lax.broadcasted_iota(I32, (npair, rows), 1), 3)).astype(BF16)
    diag = (lax.broadcasted_iota(I32, (SUBLANE, rows), 0)
            == (lax.broadcasted_iota(I32, (SUBLANE, rows), 1) & (SUBLANE - 1)))
    return sel, sel_t, diag


def _peer_group(load_rows, between, t0, gate_ref, hn_ref, x_ref, fn_ref, o_ref, consts, final_norm):
    sel, sel_t, diag = consts
    G = GATHER_GROUP
    rows = sel.shape[0]
    tok = lax.broadcasted_iota(I32, (G, rows), 0)
    partial = jnp.zeros((G, rows), F32)
    for j in range(G):
        between(j, 0)
        w = load_rows(j).reshape(rows, LANE)
        mu = pltpu.bitcast(lax.shift_left(w, jnp.uint32(16)), F32).astype(BF16)
        gj = _dot_nt(hn_ref[t0 + j].astype(BF16), mu)
        row = jnp.sum(jnp.where(diag, gj, 0.0), axis=0, keepdims=True)
        partial = jnp.where(tok == j, row, partial)
    dots = _dot(partial.astype(BF16), sel)
    for j in range(G // 2):
        between(j, 2)
    gelu = 0.5 * dots * (1.0 + lax.erf(dots * (2.0 ** -0.5)))
    act = (gelu * gate_ref[t0:t0 + G, :]).astype(BF16)
    arep = _dot(act, sel_t)
    for j in range(G // 2, G):
        between(j, 2)
    ys = []
    for j in range(G):
        between(j, 1)
        w = load_rows(j).reshape(rows, LANE)
        mv = pltpu.bitcast(w & jnp.uint32(0xFFFF0000), F32).astype(BF16)
        aexp = jnp.where(diag, arep[j:j + 1, :], 0.0).astype(BF16)
        y = _dot(aexp, mv) + x_ref[t0 + j]
        if final_norm:
            ssq = jnp.sum(jnp.sum(y * y, axis=1, keepdims=True), axis=0, keepdims=True)
            y = y * lax.rsqrt(ssq / (SUBLANE * LANE) + EPS) * fn_ref[...]
        ys.append(y)
    for j in range(G):
        o_ref[t0 + j] = ys[j]


def _expert_kernel(idx_ref, idxn_ref, gate_ref, hn_ref, x_ref, fn_ref, tab_ref, o_ref, buf, sem,
                   *, final_norm):
    T = GATHER_TOKENS
    G = GATHER_GROUP
    npair = idx_ref.shape[1]
    ngroups = T // G
    rows = npair * SUBLANE

    assert ngroups == GATHER_RING and GATHER_AHEAD < ngroups
    step = pl.program_id(0)
    cuts = (0, (3 * npair) // 8, (6 * npair) // 8, npair)

    def issue(iref, g, j, part):
        for p in range(cuts[part], cuts[part + 1]):
            e = iref[g * G + j, p]
            pltpu.make_async_copy(tab_ref.at[e], buf.at[g * G + j, p], sem.at[g]).start(priority=p % 2)

    def wait_group(g):
        for j in range(G):
            pltpu.make_async_copy(tab_ref.at[pl.ds(0, npair)], buf.at[g * G + j], sem.at[g]).wait()

    @pl.when(step == 0)
    def _():
        for g in range(GATHER_AHEAD):
            for j in range(G):
                for part in range(len(cuts) - 1):
                    issue(idx_ref, g, j, part)

    consts = _peer_consts(npair)
    for g in range(ngroups):
        nxt = g + GATHER_AHEAD
        nref, ng = (idx_ref, nxt) if nxt < ngroups else (idxn_ref, nxt - ngroups)
        wait_group(g)
        _peer_group(lambda j: buf[g * G + j], lambda j, part: issue(nref, ng, j, part), g * G,
                    gate_ref, hn_ref, x_ref, fn_ref, o_ref, consts, final_norm)

    @pl.when(step == pl.num_programs(0) - 1)
    def _():
        for g in range(GATHER_AHEAD):
            wait_group(g)


def _experts(idx, gate, hn3, x3, fnorm3, table3, final_norm):
    n = x3.shape[0]
    tt = GATHER_TOKENS
    npair = idx.shape[1]
    tile = lambda: pl.BlockSpec((tt, SUBLANE, LANE), lambda i: (i, 0, 0))
    nsteps = n // tt
    return pl.pallas_call(
        functools.partial(_expert_kernel, final_norm=final_norm),
        grid=(nsteps,),
        in_specs=[
            pl.BlockSpec((tt, npair), lambda i: (i, 0), memory_space=pltpu.SMEM),
            pl.BlockSpec((tt, npair), lambda i: (jnp.minimum(i + 1, nsteps - 1), 0),
                         memory_space=pltpu.SMEM),
            pl.BlockSpec((tt, npair), lambda i: (i, 0)),
            tile(), tile(),
            pl.BlockSpec((SUBLANE, LANE), lambda i: (0, 0)),
            pl.BlockSpec(memory_space=pl.ANY),
        ],
        out_specs=tile(),
        out_shape=jax.ShapeDtypeStruct((n, SUBLANE, LANE), F32),
        scratch_shapes=[
            pltpu.VMEM((GATHER_RING * GATHER_GROUP, npair, SUBLANE, LANE), jnp.uint32),
            pltpu.SemaphoreType.DMA((GATHER_RING,)),
        ],
        compiler_params=_cparams(("arbitrary",)),
    )(idx, idx, gate, hn3, x3, fnorm3, table3)


def _pad_heads(w, heads, dim):
    d = w.shape[0]
    w = w.reshape(d, heads, dim)
    return jnp.pad(w, ((0, 0), (0, 0), (0, LANE - dim))).reshape(d, heads * LANE)


def _pack_table(u, v):
    ub = lax.bitcast_convert_type(u.astype(BF16), jnp.uint16).astype(jnp.uint32)
    vb = lax.bitcast_convert_type(v.astype(BF16), jnp.uint16).astype(jnp.uint32)
    return ub | (vb << 16)


def _split_w(w):
    hi = w.astype(BF16)
    return hi, (w - hi.astype(F32)).astype(BF16)


def kernel(x, mem, mem_norm, rel_bias, mix_norm, ffn_norm, final_norm, w_o, w_mem_kv, a_w_in,
           a_kv_norm, a_w_uk, a_w_uv, b_w_in, b_conv_w, b_conv_b, b_dt_bias, b_a_log, b_d_skip,
           b_out_norm, peer_w_q, peer_sub_keys, peer_u, peer_v):
    b, s, d = x.shape
    n = b * s
    depth = w_o.shape[0]
    mem_width = MEM_HEADS * MEM_HEAD_DIM
    seq_width = w_o.shape[1] - mem_width
    a_heads = seq_width // A_HEAD_DIM
    ssm_heads = seq_width // SSM_HEAD_DIM
    conv_dim = seq_width + 2 * SSM_GROUPS * SSM_STATE
    assert s % ATT_Q == 0 and s % PROJ_TOKENS == 0 and n % ROUTE_TOKENS == 0 and s >= 4 * IDX_TOPK
    assert d == SUBLANE * LANE and n % GATHER_TOKENS == 0

    wk, wv = w_mem_kv[:, :, :mem_width], w_mem_kv[:, :, mem_width:]
    w_kv_pad = jnp.concatenate(
        [jnp.stack([_pad_heads(wk[l], MEM_HEADS, MEM_HEAD_DIM) for l in range(depth)]),
         jnp.stack([_pad_heads(wv[l], MEM_HEADS, MEM_HEAD_DIM) for l in range(depth)])],
        axis=-1).astype(BF16)
    kv_all = _mem_kv(mem, mem_norm, w_kv_pad)

    x2 = x.reshape(n, d)
    for i in range(depth):
        j = i // 2
        woa = w_o[i, :seq_width].astype(BF16)
        wob = jnp.pad(w_o[i, seq_width:].reshape(MEM_HEADS, MEM_HEAD_DIM, d),
                      ((0, 0), (0, LANE - MEM_HEAD_DIM), (0, 0))).astype(BF16)
        g_mix = mix_norm[i].reshape(1, d)
        if i % 2 == 0:
            w_in = a_w_in[j]
            o0 = seq_width
            o1 = o0 + A_KV_RANK
            o2 = o1 + IDX_HEADS * IDX_DIM
            o3 = o2 + IDX_DIM
            o4 = o3 + IDX_HEADS
            wq = w_in[:, :o0].astype(BF16)
            wc = w_in[:, o0:o1].astype(BF16)
            wm = _pad_heads(w_in[:, o4:], MEM_HEADS, MEM_HEAD_DIM).astype(BF16)
            w_idx = jnp.concatenate(
                [w_in[:, o1:o2], w_in[:, o2:o3], w_in[:, o2:o3],
                 jnp.pad(w_in[:, o3:o4], ((0, 0), (0, LANE - IDX_HEADS)))], axis=1)
            wih, wil = _split_w(w_idx)
            q, c, ct, iq, ka, iw, qm = _in_a(x2, g_mix, wq, wc, wm, wih, wil,
                                             a_kv_norm[j].reshape(1, A_KV_RANK))
            bias = _bias_tiles(rel_bias)
            r3 = lambda t: t.reshape(b, s, t.shape[-1])
            seq = _dsa(r3(q), r3(iq), r3(iw), r3(ka), r3(c),
                       ct.reshape(b, s // ATT_Q, A_KV_RANK, ATT_Q),
                       a_w_uk[j].astype(BF16), jnp.swapaxes(a_w_uv[j], 1, 2).astype(BF16), bias)
        else:
            w_in = b_w_in[j]
            o0 = seq_width
            o1 = o0 + conv_dim
            o2 = o1 + ssm_heads
            wz = w_in[:, :o0].astype(BF16)
            wx = w_in[:, o0:o1].astype(BF16)
            wm = _pad_heads(w_in[:, o2:], MEM_HEADS, MEM_HEAD_DIM).astype(BF16)
            wdh, wdl = _split_w(jnp.pad(w_in[:, o1:o2], ((0, 0), (0, LANE - ssm_heads))))
            z, xbc, dt, qm = _in_b(x2, g_mix, wz, wx, wm, wdh, wdl)
            padh = lambda t: jnp.pad(t.reshape(1, ssm_heads), ((0, 0), (0, LANE - ssm_heads)))
            seq = _ssd(xbc.reshape(b, s, conv_dim), z.reshape(b, s, seq_width),
                       dt.reshape(b, s, LANE), b_conv_w[j], b_conv_b[j].reshape(1, conv_dim),
                       padh(b_dt_bias[j]), padh(b_a_log[j]),
                       jnp.repeat(b_d_skip[j], SSM_HEAD_DIM).reshape(1, seq_width),
                       b_out_norm[j].reshape(1, seq_width))
        x2 = _out_proj(x2, seq.reshape(n, seq_width), qm, kv_all[i], woa, wob, s)

        wqt = jnp.transpose(peer_w_q[i]).astype(BF16)
        hn, eidx, gate = _route(x2, ffn_norm[i].reshape(1, d), wqt, peer_sub_keys[i].astype(BF16))
        tile3 = lambda t: t.reshape(t.shape[0], SUBLANE, LANE)
        table3 = tile3(_pack_table(peer_u[i], peer_v[i]))
        x2 = _experts(eidx, gate, tile3(hn), tile3(x2), final_norm.reshape(SUBLANE, LANE), table3,
                      i == depth - 1).reshape(n, d)
    return x2.reshape(b, s, d)
```

```python
import functools
import math

import jax
import jax.numpy as jnp
from jax import lax
from jax.experimental import pallas as pl
from jax.experimental.pallas import tpu as pltpu

F32 = jnp.float32
BF16 = jnp.bfloat16
I32 = jnp.int32

EPS = 1e-6
MEM_HEADS = 4
MEM_HEAD_DIM = 64
A_HEAD_DIM = 64
A_KV_RANK = 256
IDX_HEADS = 8
IDX_DIM = 64
IDX_TOPK = 256
REL_BUCKETS = 32
REL_MAX_DIST = 128
SSM_HEAD_DIM = 64
SSM_GROUPS = 2
SSM_STATE = 128
CONV_WIDTH = 4
SSD_CHUNK = 128
PEER_HEADS = 8
PEER_KEYS = 128
PEER_TOPK = 16

LANE = 128
SUBLANE = 8
INT_MIN = -(2 ** 31)
NEG_BIG = -1e30

PROJ_TOKENS = 512
ATT_Q = 256
ATT_HEADS = 6
ROUTE_TOKENS = 256
PACK_ROWS = 256
GATHER_GROUP = 8
GATHER_RING = 4
GATHER_AHEAD = 3
GATHER_TOKENS = GATHER_GROUP * GATHER_RING
VMEM_LIMIT = 56 * 1024 * 1024


def _cparams(sem):
    return pltpu.CompilerParams(dimension_semantics=sem, vmem_limit_bytes=VMEM_LIMIT)


def _fold_rows(x, op):
    parts = [x[k * SUBLANE:(k + 1) * SUBLANE] for k in range(x.shape[0] // SUBLANE)]
    while len(parts) > 1:
        nxt = [op(parts[k], parts[k + 1]) for k in range(0, len(parts) - 1, 2)]
        if len(parts) % 2:
            nxt.append(parts[-1])
        parts = nxt
    return parts[0]


def _rms(x, g):
    return x * lax.rsqrt(jnp.mean(x * x, axis=-1, keepdims=True) + EPS) * g


def _split2(a):
    hi = a.astype(BF16)
    lo = (a - hi.astype(F32)).astype(BF16)
    return hi, lo


def _dot(a, b):
    return jnp.dot(a, b, preferred_element_type=F32)


def _dot_nt(a, b):
    return lax.dot_general(a, b, (((1,), (1,)), ((), ())), preferred_element_type=F32)


def _mem_kv_kernel(mem_ref, g_ref, w_ref, out_ref):
    y = _rms(mem_ref[0], g_ref[...])
    out_ref[0, 0] = _dot(y.astype(BF16), w_ref[0]).astype(BF16)


def _mem_kv(mem, mem_norm, w_pad):
    b, m, d = mem.shape
    depth, _, wcols = w_pad.shape
    return pl.pallas_call(
        _mem_kv_kernel,
        grid=(depth, b),
        in_specs=[
            pl.BlockSpec((1, m, d), lambda l, i: (i, 0, 0)),
            pl.BlockSpec((1, d), lambda l, i: (0, 0)),
            pl.BlockSpec((1, d, wcols), lambda l, i: (l, 0, 0)),
        ],
        out_specs=pl.BlockSpec((1, 1, m, wcols), lambda l, i: (l, i, 0, 0)),
        out_shape=jax.ShapeDtypeStruct((depth, b, m, wcols), BF16),
        compiler_params=_cparams(("arbitrary", "arbitrary")),
    )(mem, mem_norm.reshape(1, d), w_pad)


def _bias_kernel(rb_ref, out_ref):
    h = pl.program_id(0)
    max_exact = REL_BUCKETS // 2
    far = rb_ref[REL_BUCKETS - 1, h]
    krow = lax.broadcasted_iota(I32, (ATT_Q, ATT_Q), 0)
    qcol = lax.broadcasted_iota(I32, (ATT_Q, ATT_Q), 1)
    for r in range(2):
        dist = qcol - krow + ATT_Q * r
        n = jnp.maximum(dist, 0)
        nf = jnp.maximum(n, max_exact).astype(F32)
        large = max_exact + (jnp.log(nf / max_exact) / math.log(REL_MAX_DIST / max_exact)
                             * (REL_BUCKETS - max_exact)).astype(I32)
        large = jnp.minimum(large, REL_BUCKETS - 1)
        bucket = jnp.where(n < max_exact, n, large)
        acc = jnp.zeros((ATT_Q, ATT_Q), F32)
        for k in range(REL_BUCKETS):
            acc = jnp.where(bucket == k, rb_ref[k, h], acc)
        out_ref[0, r] = acc - far


def _bias_tiles(rel_bias):
    heads = rel_bias.shape[1]
    return pl.pallas_call(
        _bias_kernel,
        grid=(heads,),
        in_specs=[pl.BlockSpec(memory_space=pltpu.SMEM)],
        out_specs=pl.BlockSpec((1, 2, ATT_Q, ATT_Q), lambda h: (h, 0, 0, 0)),
        out_shape=jax.ShapeDtypeStruct((heads, 2, ATT_Q, ATT_Q), F32),
        compiler_params=_cparams(("arbitrary",)),
    )(rel_bias)


def _in_a_kernel(x_ref, g_ref, wq_ref, wc_ref, wm_ref, wih_ref, wil_ref, kvn_ref,
                 q_ref, c_ref, ct_ref, iq_ref, ka_ref, iw_ref, qm_ref):
    h = _rms(x_ref[...], g_ref[...])
    h_hi, h_lo = _split2(h)
    q_ref[...] = _dot(h_hi, wq_ref[...]).astype(BF16)
    qm_ref[...] = _dot(h_hi, wm_ref[...]).astype(BF16)
    c = _rms(_dot(h_hi, wc_ref[...]), kvn_ref[...])
    c_ref[...] = c.astype(BF16)
    for j in range(PROJ_TOKENS // ATT_Q):
        ct_ref[j] = c[j * ATT_Q:(j + 1) * ATT_Q].T.astype(BF16)
    wih = wih_ref[...]
    ii = _dot(h_hi, wih) + _dot(h_lo, wih) + _dot(h_hi, wil_ref[...])
    iq_ref[...] = ii[:, :IDX_HEADS * IDX_DIM]
    kk = ii[:, IDX_HEADS * IDX_DIM:IDX_HEADS * IDX_DIM + LANE]
    kk_hi, kk_lo = _split2(kk)
    lane = lax.broadcasted_iota(I32, kk.shape, 1)
    half = jnp.where(lane < IDX_DIM, kk_hi, kk_lo)
    ka_ref[...] = jnp.concatenate([half, half], axis=1)
    iw_ref[...] = ii[:, IDX_HEADS * IDX_DIM + LANE:]


def _in_a(x2, g, wq, wc, wm, wih, wil, kvn):
    n, d = x2.shape
    tt = PROJ_TOKENS
    full = lambda a: pl.BlockSpec(a.shape, lambda i: (0,) * a.ndim)
    row = lambda w: pl.BlockSpec((tt, w), lambda i: (i, 0))
    nblk = tt // ATT_Q
    outs = [
        jax.ShapeDtypeStruct((n, wq.shape[1]), BF16),
        jax.ShapeDtypeStruct((n, A_KV_RANK), BF16),
        jax.ShapeDtypeStruct((n // ATT_Q, A_KV_RANK, ATT_Q), BF16),
        jax.ShapeDtypeStruct((n, IDX_HEADS * IDX_DIM), F32),
        jax.ShapeDtypeStruct((n, 2 * LANE), BF16),
        jax.ShapeDtypeStruct((n, LANE), F32),
        jax.ShapeDtypeStruct((n, wm.shape[1]), BF16),
    ]
    out_specs = [
        row(wq.shape[1]), row(A_KV_RANK),
        pl.BlockSpec((nblk, A_KV_RANK, ATT_Q), lambda i: (i, 0, 0)),
        row(IDX_HEADS * IDX_DIM), row(2 * LANE), row(LANE), row(wm.shape[1]),
    ]
    return pl.pallas_call(
        _in_a_kernel,
        grid=(n // tt,),
        in_specs=[row(d), full(g), full(wq), full(wc), full(wm), full(wih), full(wil), full(kvn)],
        out_specs=out_specs,
        out_shape=outs,
        compiler_params=_cparams(("arbitrary",)),
    )(x2, g, wq, wc, wm, wih, wil, kvn)


def _dsa_kernel(q_ref, iq_ref, iw_ref, ka_ref, c_ref, ct_ref, wuk_ref, wuvt_ref, bias_ref,
                o_ref, key_s, lg_s, qbt_s, qt_s, olat_s, ot_s, *, heads):
    i = pl.program_id(1)
    nch = i + 1
    t0 = i * ATT_Q
    Q = ATT_Q
    krow = lax.broadcasted_iota(I32, (Q, Q), 0)
    qcol = lax.broadcasted_iota(I32, (Q, Q), 1)

    iqv = iq_ref[0]
    lane = lax.broadcasted_iota(I32, (Q, LANE), 1)
    for j in range(IDX_HEADS // 2):
        v = iqv[:, j * LANE:(j + 1) * LANE]
        r = pltpu.roll(v, IDX_DIM, 1)
        for hh, dup in ((2 * j, jnp.where(lane < IDX_DIM, v, r)),
                        (2 * j + 1, jnp.where(lane < IDX_DIM, r, v))):
            hi = dup.astype(BF16)
            lo = (dup - hi.astype(F32)).astype(BF16)
            qbt_s[hh, 0:LANE, :] = hi.astype(F32).T.astype(BF16)
            qbt_s[hh, LANE:2 * LANE, :] = lo.astype(F32).T.astype(BF16)
    wt = iw_ref[0].T * (IDX_HEADS ** -0.5)
    qt_s[...] = q_ref[0].astype(F32).T.astype(BF16)

    def score_chunk(c, carry):
        ka = ka_ref[0, pl.ds(pl.multiple_of(c * Q, Q), Q), :]
        acc = jnp.zeros((Q, Q), F32)
        for hh in range(IDX_HEADS):
            z = _dot(ka, qbt_s[hh])
            acc = acc + jnp.maximum(z, 0.0) * wt[hh:hh + 1, :]
        acc = acc * (IDX_DIM ** -0.5)
        bits = pltpu.bitcast(acc, I32)
        skey = jnp.where(bits < 0, bits ^ 0x7FFFFFFF, bits)
        causal = (krow + c * Q) <= (qcol + t0)
        key_s[pl.ds(pl.multiple_of(c * Q, Q), Q), :] = jnp.where(causal, skey, INT_MIN)
        return carry

    lax.fori_loop(0, nch, score_chunk, 0)

    def count_ge(thr):
        def body(c, acc):
            blk = key_s[pl.ds(pl.multiple_of(c * Q, Q), Q), :]
            return acc + _fold_rows((blk >= thr).astype(I32), jnp.add)
        acc = lax.fori_loop(0, nch, body, jnp.zeros((SUBLANE, Q), I32))
        return jnp.sum(acc, axis=0, keepdims=True)

    c0 = count_ge(jnp.zeros((1, Q), I32))
    has_k = c0 >= IDX_TOPK
    thr = jnp.where(has_k, 0, INT_MIN).astype(I32)
    cnt = jnp.where(has_k, c0, nch * Q)
    reachable = (lax.broadcasted_iota(I32, (1, Q), 1) + t0 + 1) >= IDX_TOPK

    def unsettled(cnt):
        return jnp.max(jnp.where((cnt != IDX_TOPK) & reachable, 1, 0)) > 0

    def bisect(it, state):
        thr, cnt = state
        cand = thr + lax.shift_left(jnp.int32(1), 30 - it)
        c = count_ge(cand)
        ok = c >= IDX_TOPK
        return jnp.where(ok, cand, thr), jnp.where(ok, c, cnt)

    thr, cnt = lax.fori_loop(0, 31, bisect, (thr, cnt))
    thr = jnp.maximum(thr, INT_MIN + 1)

    nbits = max(1, int(math.ceil(math.log2(ka_ref.shape[1] + 1))))

    def tie_cut(thr):
        need = IDX_TOPK - count_ge(thr + 1)

        def count_tie_below(bound):
            def body(c, acc):
                blk = key_s[pl.ds(pl.multiple_of(c * Q, Q), Q), :]
                hit = (blk == thr) & ((krow + c * Q) < bound)
                return acc + _fold_rows(hit.astype(I32), jnp.add)
            acc = lax.fori_loop(0, nch, body, jnp.zeros((SUBLANE, Q), I32))
            return jnp.sum(acc, axis=0, keepdims=True)

        def tie_bisect(it, p0):
            cand = p0 + lax.shift_left(jnp.int32(1), nbits - 1 - it)
            return jnp.where(count_tie_below(cand) < need, cand, p0)

        return lax.fori_loop(0, nbits, tie_bisect, jnp.zeros((1, Q), I32)) + 1

    pcut = lax.cond(unsettled(cnt), tie_cut, lambda thr: jnp.full((1, Q), 2 ** nbits, I32), thr)

    def mask_chunk(c, carry):
        off = pl.multiple_of(c * Q, Q)
        key = key_s[pl.ds(off, Q), :]
        sel = (key > thr) | ((key == thr) & ((krow + c * Q) < pcut))
        key_s[pl.ds(off, Q), :] = pltpu.bitcast(jnp.where(sel, 0.0, NEG_BIG).astype(F32), I32)
        return carry

    lax.fori_loop(0, nch, mask_chunk, 0)

    HB = ATT_HEADS

    def head_group(hg, carry):
        qlts = []
        for hh in range(HB):
            h = hg * HB + hh
            qh = qt_s[pl.ds(pl.multiple_of(h * A_HEAD_DIM, A_HEAD_DIM), A_HEAD_DIM), :]
            qlts.append((_dot(wuk_ref[h], qh) * (A_HEAD_DIM ** -0.5)).astype(BF16))
        qlt = jnp.concatenate(qlts, axis=1)

        def logits_chunk(c, m_acc, band):
            off = pl.multiple_of(c * Q, Q)
            lg = _dot(c_ref[0, pl.ds(off, Q), :], qlt)
            mask = pltpu.bitcast(key_s[pl.ds(off, Q), :], F32)
            new = []
            for hh in range(HB):
                lgh = lg[:, hh * Q:(hh + 1) * Q] + mask
                if band is not None:
                    lgh = lgh + bias_ref[hg * HB + hh, band]
                lg_s[pl.ds(off, Q), hh * Q:(hh + 1) * Q] = lgh
                new.append(jnp.maximum(m_acc[:, hh * Q:(hh + 1) * Q], _fold_rows(lgh, jnp.maximum)))
            return jnp.concatenate(new, axis=1)

        m_acc = jnp.full((SUBLANE, HB * Q), NEG_BIG, F32)
        m_acc = lax.fori_loop(0, nch - 2, lambda c, m: logits_chunk(c, m, None), m_acc)
        m_acc = lax.cond(nch >= 2, lambda m: logits_chunk(nch - 2, m, 1), lambda m: m, m_acc)
        m_acc = logits_chunk(nch - 1, m_acc, 0)
        m = jnp.max(m_acc, axis=0, keepdims=True)

        olat_s[...] = jnp.zeros_like(olat_s)

        def pv_chunk(c, s_acc):
            off = pl.multiple_of(c * Q, Q)
            p = jnp.exp(lg_s[pl.ds(off, Q), :] - m)
            olat_s[...] += _dot(ct_ref[0, c], p.astype(BF16))
            return s_acc + _fold_rows(p, jnp.add)

        s_acc = lax.fori_loop(0, nch, pv_chunk, jnp.zeros((SUBLANE, HB * Q), F32))
        s = jnp.sum(s_acc, axis=0, keepdims=True)
        for hh in range(HB):
            h = hg * HB + hh
            ol = olat_s[:, hh * Q:(hh + 1) * Q].astype(BF16)
            oh = _dot(wuvt_ref[h], ol) / s[:, hh * Q:(hh + 1) * Q]
            ot_s[pl.ds(pl.multiple_of(h * A_HEAD_DIM, A_HEAD_DIM), A_HEAD_DIM), :] = oh
        return carry

    lax.fori_loop(0, heads // HB, head_group, 0)
    o_ref[0] = ot_s[...].T.astype(BF16)


def _dsa(q, iq, iw, ka, c, ct, wuk, wuvt, bias):
    b, s, w = q.shape
    heads = wuk.shape[0]
    nq = s // ATT_Q
    assert heads % ATT_HEADS == 0
    once = pl.Buffered(1)
    full = lambda a: pl.BlockSpec(a.shape, lambda bi, i: (0,) * a.ndim, pipeline_mode=once)
    blk = lambda width: pl.BlockSpec((1, ATT_Q, width), lambda bi, i: (bi, i, 0))
    per_b = lambda width: pl.BlockSpec((1, s, width), lambda bi, i: (bi, 0, 0), pipeline_mode=once)
    return pl.pallas_call(
        functools.partial(_dsa_kernel, heads=heads),
        grid=(b, nq),
        in_specs=[
            blk(w), blk(iq.shape[2]), blk(iw.shape[2]),
            per_b(ka.shape[2]), per_b(c.shape[2]),
            pl.BlockSpec((1, nq, A_KV_RANK, ATT_Q), lambda bi, i: (bi, 0, 0, 0), pipeline_mode=once),
            full(wuk), full(wuvt), full(bias),
        ],
        out_specs=blk(w),
        out_shape=jax.ShapeDtypeStruct((b, s, w), BF16),
        scratch_shapes=[
            pltpu.VMEM((s, ATT_Q), I32),
            pltpu.VMEM((s, ATT_HEADS * ATT_Q), F32),
            pltpu.VMEM((IDX_HEADS, 2 * LANE, ATT_Q), BF16),
            pltpu.VMEM((w, ATT_Q), BF16),
            pltpu.VMEM((A_KV_RANK, ATT_HEADS * ATT_Q), F32),
            pltpu.VMEM((w, ATT_Q), F32),
        ],
        compiler_params=_cparams(("arbitrary", "arbitrary")),
    )(q, iq, iw, ka, c, ct, wuk, wuvt, bias)


def _out_kernel(x_ref, seq_ref, qm_ref, kv_ref, woa_ref, wob_ref, o_ref, o3_ref):
    acc = x_ref[...] + _dot(seq_ref[...], woa_ref[...])
    qm = qm_ref[...]
    kv = kv_ref[0]
    for h in range(MEM_HEADS):
        k = kv[:, h * LANE:(h + 1) * LANE]
        v = kv[:, (MEM_HEADS + h) * LANE:(MEM_HEADS + h + 1) * LANE]
        lg = _dot_nt(qm[:, h * LANE:(h + 1) * LANE], k) * (MEM_HEAD_DIM ** -0.5)
        p = jnp.exp(lg - jnp.max(lg, axis=-1, keepdims=True))
        oh = _dot(p.astype(BF16), v) / jnp.sum(p, axis=-1, keepdims=True)
        acc = acc + _dot(oh.astype(BF16), wob_ref[h])
    o_ref[...] = acc
    o3_ref[...] = acc.reshape(o3_ref.shape)


def _out_proj(x2, seq2, qm2, kv, woa, wob, seq_len):
    n, d = x2.shape
    tt = PROJ_TOKENS
    per_seq = seq_len // tt
    full = lambda a: pl.BlockSpec(a.shape, lambda i: (0,) * a.ndim)
    row = lambda w: pl.BlockSpec((tt, w), lambda i: (i, 0))
    return pl.pallas_call(
        _out_kernel,
        grid=(n // tt,),
        in_specs=[row(d), row(seq2.shape[1]), row(qm2.shape[1]),
                  pl.BlockSpec((1,) + kv.shape[1:], lambda i: (i // per_seq, 0, 0)),
                  full(woa), full(wob)],
        out_specs=[row(d), pl.BlockSpec((tt, SUBLANE, LANE), lambda i: (i, 0, 0))],
        out_shape=[jax.ShapeDtypeStruct((n, d), F32),
                   jax.ShapeDtypeStruct((n, SUBLANE, LANE), F32)],
        compiler_params=_cparams(("arbitrary",)),
    )(x2, seq2, qm2, kv, woa, wob)


def _in_b_kernel(x_ref, g_ref, wz_ref, wx_ref, wm_ref, wdh_ref, wdl_ref,
                 z_ref, xbc_ref, dt_ref, qm_ref):
    h = _rms(x_ref[...], g_ref[...])
    h_hi, h_lo = _split2(h)
    z_ref[...] = _dot(h_hi, wz_ref[...]).astype(BF16)
    xbc_ref[...] = _dot(h_hi, wx_ref[...])
    qm_ref[...] = _dot(h_hi, wm_ref[...]).astype(BF16)
    wdh = wdh_ref[...]
    dt_ref[...] = _dot(h_hi, wdh) + _dot(h_lo, wdh) + _dot(h_hi, wdl_ref[...])


def _in_b(x2, g, wz, wx, wm, wdh, wdl):
    n, d = x2.shape
    tt = PROJ_TOKENS
    full = lambda a: pl.BlockSpec(a.shape, lambda i: (0,) * a.ndim)
    row = lambda w: pl.BlockSpec((tt, w), lambda i: (i, 0))
    return pl.pallas_call(
        _in_b_kernel,
        grid=(n // tt,),
        in_specs=[row(d), full(g), full(wz), full(wx), full(wm), full(wdh), full(wdl)],
        out_specs=[row(wz.shape[1]), row(wx.shape[1]), row(LANE), row(wm.shape[1])],
        out_shape=[jax.ShapeDtypeStruct((n, wz.shape[1]), BF16),
                   jax.ShapeDtypeStruct((n, wx.shape[1]), F32),
                   jax.ShapeDtypeStruct((n, LANE), F32),
                   jax.ShapeDtypeStruct((n, wm.shape[1]), BF16)],
        compiler_params=_cparams(("arbitrary",)),
    )(x2, g, wz, wx, wm, wdh, wdl)


def _ssd_kernel(xbc_ref, z_ref, dt_ref, cw_ref, cb_ref, dtb_ref, alog_ref, dsk_ref, on_ref,
                o_ref, tail_s, xpad_s, h_s, *, seq_width):
    L = SSD_CHUNK
    ci = pl.program_id(1)

    @pl.when(ci == 0)
    def _():
        tail_s[...] = jnp.zeros_like(tail_s)
        h_s[...] = jnp.zeros_like(h_s)

    xr = xbc_ref[0]
    xpad_s[0:SUBLANE, :] = tail_s[...]
    xpad_s[SUBLANE:SUBLANE + L, :] = xr
    tail_s[...] = xr[L - SUBLANE:L, :]
    conv = cb_ref[...] + jnp.zeros_like(xr)
    for j in range(CONV_WIDTH):
        off = SUBLANE - (CONV_WIDTH - 1) + j
        conv = conv + cw_ref[j:j + 1, :] * xpad_s[off:off + L, :]
    xc = conv * jax.nn.sigmoid(conv)
    gw = SSM_STATE
    bm = xc[:, seq_width:seq_width + SSM_GROUPS * gw]
    cm = xc[:, seq_width + SSM_GROUPS * gw:seq_width + 2 * SSM_GROUPS * gw]

    dt = jax.nn.softplus(dt_ref[0] + dtb_ref[...])
    a = -jnp.exp(alog_ref[...])
    da = dt * a
    row_i = lax.broadcasted_iota(I32, (L, L), 0)
    col_i = lax.broadcasted_iota(I32, (L, L), 1)
    tri = row_i >= col_i
    tri_b = tri.astype(BF16)
    d1 = da.astype(BF16)
    r1 = da - d1.astype(F32)
    d2 = r1.astype(BF16)
    d3 = (r1 - d2.astype(F32)).astype(BF16)
    acs = _dot(tri_b, d1) + _dot(tri_b, d2) + _dot(tri_b, d3)
    acs_t = acs.T
    a_last = acs[L - 1:L, :]
    e_acs = jnp.exp(acs)
    e_end = jnp.exp(a_last - acs)
    lane = lax.broadcasted_iota(I32, (L, LANE), 1)
    lo_half = lane < SSM_HEAD_DIM
    lane1 = lax.broadcasted_iota(I32, (1, LANE), 1)

    def pair_cols(m, r):
        return jnp.where(lo_half, m[:, r:r + 1], m[:, r + 1:r + 2])

    heads_per_group = (seq_width // SSM_HEAD_DIM) // SSM_GROUPS
    pairs_per_group = heads_per_group // 2
    pieces = []
    ssq = jnp.zeros((L, 1), F32)
    for g in range(SSM_GROUPS):
        bg = bm[:, g * gw:(g + 1) * gw]
        cg = cm[:, g * gw:(g + 1) * gw]
        bg_b = bg.astype(BF16)
        cg_b = cg.astype(BF16)
        bgt_b = bg.T.astype(BF16)
        cb = _dot_nt(cg_b, bg_b)
        for jp in range(pairs_per_group):
            r = g * heads_per_group + 2 * jp
            col0 = (g * pairs_per_group + jp) * LANE
            xs = xc[:, col0:col0 + LANE]
            xdt = (xs * pair_cols(dt, r)).astype(BF16)
            ys = []
            for rr in (r, r + 1):
                seg = acs[:, rr:rr + 1] - acs_t[rr:rr + 1, :]
                dec = jnp.exp(jnp.where(tri, seg, -jnp.inf))
                ys.append(_dot((cb * dec).astype(BF16), xdt))
            y = jnp.where(lo_half, ys[0], ys[1])
            hcol = jp * LANE
            h_prev = h_s[g, :, hcol:hcol + LANE]
            y = y + _dot(cg_b, h_prev.astype(BF16)) * pair_cols(e_acs, r)
            xw = (xs * pair_cols(dt * e_end, r)).astype(BF16)
            st = _dot(bgt_b, xw)
            cd = jnp.where(lane1 < SSM_HEAD_DIM, jnp.exp(a_last[:, r:r + 1]),
                           jnp.exp(a_last[:, r + 1:r + 2]))
            h_s[g, :, hcol:hcol + LANE] = h_prev * cd + st
            y = y + xs * dsk_ref[:, col0:col0 + LANE]
            zz = z_ref[0, :, col0:col0 + LANE].astype(F32)
            y = y * (zz * jax.nn.sigmoid(zz))
            ssq = ssq + jnp.sum(y * y, axis=-1, keepdims=True)
            pieces.append(y)
    scale = lax.rsqrt(ssq / seq_width + EPS)
    for k, y in enumerate(pieces):
        o_ref[0, :, k * LANE:(k + 1) * LANE] = (y * scale * on_ref[:, k * LANE:(k + 1) * LANE]).astype(BF16)


def _ssd(xbc, z, dt, cw, cb, dtb, alog, dsk, onorm):
    b, s, cdim = xbc.shape
    w = z.shape[2]
    L = SSD_CHUNK
    full = lambda a: pl.BlockSpec(a.shape, lambda bi, i: (0,) * a.ndim)
    blk = lambda width: pl.BlockSpec((1, L, width), lambda bi, i: (bi, i, 0))
    return pl.pallas_call(
        functools.partial(_ssd_kernel, seq_width=w),
        grid=(b, s // L),
        in_specs=[blk(cdim), blk(w), blk(LANE), full(cw), full(cb), full(dtb), full(alog),
                  full(dsk), full(onorm)],
        out_specs=blk(w),
        out_shape=jax.ShapeDtypeStruct((b, s, w), BF16),
        scratch_shapes=[
            pltpu.VMEM((SUBLANE, cdim), F32),
            pltpu.VMEM((SUBLANE + L, cdim), F32),
            pltpu.VMEM((SSM_GROUPS, SSM_STATE, w // SSM_GROUPS), F32),
        ],
        compiler_params=_cparams(("arbitrary", "arbitrary")),
    )(xbc, z, dt, cw, cb, dtb, alog, dsk, onorm)


def _top_rows(vals, k, val_out, idx_out, payload=None):
    rows = lax.broadcasted_iota(I32, vals.shape, 0)
    big = vals.shape[0]
    for j in range(k):
        m = jnp.max(_fold_rows(vals, jnp.maximum), axis=0, keepdims=True)
        am = jnp.min(_fold_rows(jnp.where(vals == m, rows, big), jnp.minimum), axis=0, keepdims=True)
        hit = rows == am
        val_out[j:j + 1, :] = m
        if payload is None:
            idx_out[j:j + 1, :] = am
        else:
            idx_out[j:j + 1, :] = jnp.sum(_fold_rows(jnp.where(hit, payload, 0), jnp.add),
                                          axis=0, keepdims=True)
        vals = jnp.where(hit, -jnp.inf, vals)


def _route_kernel(x_ref, g_ref, wqt_ref, sk_ref, hn_ref, e_ref, gate_ref,
                  ts_s, ti_s, bs_s, et_s, gt_s, cs_s, ci_s):
    T = ROUTE_TOKENS
    hn = _rms(x_ref[...], g_ref[...])
    hn_ref[...] = hn.reshape(hn_ref.shape)
    hb = hn.astype(BF16)
    qt = _dot_nt(wqt_ref[...], hb)
    half = qt.shape[0] // (PEER_HEADS * 2)
    for hd in range(PEER_HEADS):
        for side in range(2):
            r0 = (hd * 2 + side) * half
            sc = _dot(sk_ref[side], qt[r0:r0 + half].astype(BF16))
            _top_rows(sc, PEER_TOPK, ts_s.at[side], ti_s.at[side])
        off = 0
        for a in range(PEER_TOPK):
            nb = PEER_TOPK // (a + 1)
            cs_s[off:off + nb, :] = ts_s[0, a:a + 1, :] + ts_s[1, 0:nb, :]
            ci_s[off:off + nb, :] = ti_s[0, a:a + 1, :] * PEER_KEYS + ti_s[1, 0:nb, :]
            off += nb
        cs_s[off:, :] = jnp.full((cs_s.shape[0] - off, T), -jnp.inf, F32)
        ci_s[off:, :] = jnp.zeros((ci_s.shape[0] - off, T), I32)
        _top_rows(cs_s[...], PEER_TOPK, bs_s, et_s.at[pl.ds(hd * PEER_TOPK, PEER_TOPK)],
                  payload=ci_s[...])
        best = bs_s[...]
        p = jnp.exp(best - jnp.max(best, axis=0, keepdims=True))
        gt_s[hd * PEER_TOPK:(hd + 1) * PEER_TOPK, :] = p / jnp.sum(p, axis=0, keepdims=True)
    e_ref[...] = et_s[...].T
    gate_ref[...] = gt_s[...].T


def _route(x2, g, wqt, sk):
    n, d = x2.shape
    tt = ROUTE_TOKENS
    npair = PEER_HEADS * PEER_TOPK
    ncand = sum(PEER_TOPK // (a + 1) for a in range(PEER_TOPK))
    ncand = -(-ncand // SUBLANE) * SUBLANE
    full = lambda a: pl.BlockSpec(a.shape, lambda i: (0,) * a.ndim)
    row = lambda w: pl.BlockSpec((tt, w), lambda i: (i, 0))
    return pl.pallas_call(
        _route_kernel,
        grid=(n // tt,),
        in_specs=[row(d), full(g), full(wqt), full(sk)],
        out_specs=[pl.BlockSpec((tt, SUBLANE, LANE), lambda i: (i, 0, 0)), row(npair), row(npair)],
        out_shape=[jax.ShapeDtypeStruct((n, SUBLANE, LANE), F32),
                   jax.ShapeDtypeStruct((n, npair), I32),
                   jax.ShapeDtypeStruct((n, npair), F32)],
        scratch_shapes=[
            pltpu.VMEM((2, PEER_TOPK, tt), F32),
            pltpu.VMEM((2, PEER_TOPK, tt), I32),
            pltpu.VMEM((PEER_TOPK, tt), F32),
            pltpu.VMEM((npair, tt), I32),
            pltpu.VMEM((npair, tt), F32),
            pltpu.VMEM((ncand, tt), F32),
            pltpu.VMEM((ncand, tt), I32),
        ],
        compiler_params=_cparams(("arbitrary",)),
    )(x2, g, wqt, sk)


def _peer_consts(npair):
    rows = npair * SUBLANE
    sel = (lax.shift_right_logical(lax.broadcasted_iota(I32, (rows, npair), 0), 3)
           == lax.broadcasted_iota(I32, (rows, npair), 1)).astype(BF16)
    sel_t = (lax.broadcasted_iota(I32, (npair, rows), 0)
             == lax.shift_right_logical(lax.broadcasted_iota(I32, (npair, rows), 1), 3)).astype(BF16)
    diag = (lax.broadcasted_iota(I32, (SUBLANE, rows), 0)
            == (lax.broadcasted_iota(I32, (SUBLANE, rows), 1) & (SUBLANE - 1)))
    return sel, sel_t, diag


def _peer_group(load_rows, between, t0, gate_ref, hn_ref, x_ref, fn_ref, o_ref, consts, final_norm):
    sel, sel_t, diag = consts
    G = GATHER_GROUP
    rows = sel.shape[0]
    tok = lax.broadcasted_iota(I32, (G, rows), 0)
    partial = jnp.zeros((G, rows), F32)
    for j in range(G):
        between(j, 0)
        w = load_rows(j).reshape(rows, LANE)
        mu = pltpu.bitcast(lax.shift_left(w, jnp.uint32(16)), F32).astype(BF16)
        gj = _dot_nt(hn_ref[t0 + j].astype(BF16), mu)
        row = jnp.sum(jnp.where(diag, gj, 0.0), axis=0, keepdims=True)
        partial = jnp.where(tok == j, row, partial)
    dots = _dot(partial.astype(BF16), sel)
    for j in range(G // 2):
        between(j, 2)
    gelu = 0.5 * dots * (1.0 + lax.erf(dots * (2.0 ** -0.5)))
    act = (gelu * gate_ref[t0:t0 + G, :]).astype(BF16)
    arep = _dot(act, sel_t)
    for j in range(G // 2, G):
        between(j, 2)
    ys = []
    for j in range(G):
        between(j, 1)
        w = load_rows(j).reshape(rows, LANE)
        mv = pltpu.bitcast(w & jnp.uint32(0xFFFF0000), F32).astype(BF16)
        aexp = jnp.where(diag, arep[j:j + 1, :], 0.0).astype(BF16)
        y = _dot(aexp, mv) + x_ref[t0 + j]
        if final_norm:
            ssq = jnp.sum(jnp.sum(y * y, axis=1, keepdims=True), axis=0, keepdims=True)
            y = y * lax.rsqrt(ssq / (SUBLANE * LANE) + EPS) * fn_ref[...]
        ys.append(y)
    for j in range(G):
        o_ref[t0 + j] = ys[j]


def _expert_kernel(idx_ref, idxn_ref, gate_ref, hn_ref, x_ref, fn_ref, tab_ref, o_ref, buf, sem,
                   *, final_norm):
    T = GATHER_TOKENS
    G = GATHER_GROUP
    npair = idx_ref.shape[1]
    ngroups = T // G
    rows = npair * SUBLANE

    assert ngroups == GATHER_RING and GATHER_AHEAD < ngroups
    step = pl.program_id(0)
    cuts = (0, (3 * npair) // 8, (6 * npair) // 8, npair)

    def issue(iref, g, j, part):
        for p in range(cuts[part], cuts[part + 1]):
            e = iref[g * G + j, p]
            pltpu.make_async_copy(tab_ref.at[e], buf.at[g * G + j, p], sem.at[g]).start(priority=p % 2)

    def wait_group(g):
        for j in range(G):
            pltpu.make_async_copy(tab_ref.at[pl.ds(0, npair)], buf.at[g * G + j], sem.at[g]).wait()

    @pl.when(step == 0)
    def _():
        for g in range(GATHER_AHEAD):
            for j in range(G):
                for part in range(len(cuts) - 1):
                    issue(idx_ref, g, j, part)

    consts = _peer_consts(npair)
    for g in range(ngroups):
        nxt = g + GATHER_AHEAD
        nref, ng = (idx_ref, nxt) if nxt < ngroups else (idxn_ref, nxt - ngroups)
        wait_group(g)
        _peer_group(lambda j: buf[g * G + j], lambda j, part: issue(nref, ng, j, part), g * G,
                    gate_ref, hn_ref, x_ref, fn_ref, o_ref, consts, final_norm)

    @pl.when(step == pl.num_programs(0) - 1)
    def _():
        for g in range(GATHER_AHEAD):
            wait_group(g)


def _experts(idx, gate, hn3, x3, fnorm3, table3, final_norm):
    n = x3.shape[0]
    tt = GATHER_TOKENS
    npair = idx.shape[1]
    tile = lambda: pl.BlockSpec((tt, SUBLANE, LANE), lambda i: (i, 0, 0))
    nsteps = n // tt
    return pl.pallas_call(
        functools.partial(_expert_kernel, final_norm=final_norm),
        grid=(nsteps,),
        in_specs=[
            pl.BlockSpec((tt, npair), lambda i: (i, 0), memory_space=pltpu.SMEM),
            pl.BlockSpec((tt, npair), lambda i: (jnp.minimum(i + 1, nsteps - 1), 0),
                         memory_space=pltpu.SMEM),
            pl.BlockSpec((tt, npair), lambda i: (i, 0)),
            tile(), tile(),
            pl.BlockSpec((SUBLANE, LANE), lambda i: (0, 0)),
            pl.BlockSpec(memory_space=pl.ANY),
        ],
        out_specs=tile(),
        out_shape=jax.ShapeDtypeStruct((n, SUBLANE, LANE), F32),
        scratch_shapes=[
            pltpu.VMEM((GATHER_RING * GATHER_GROUP, npair, SUBLANE, LANE), jnp.uint32),
            pltpu.SemaphoreType.DMA((GATHER_RING,)),
        ],
        compiler_params=_cparams(("arbitrary",)),
    )(idx, idx, gate, hn3, x3, fnorm3, table3)


def _pad_heads(w, heads, dim):
    d = w.shape[0]
    w = w.reshape(d, heads, dim)
    return jnp.pad(w, ((0, 0), (0, 0), (0, LANE - dim))).reshape(d, heads * LANE)


def _pack_kernel(u_ref, v_ref, o_ref):
    ub = pltpu.bitcast(u_ref[...].astype(BF16).astype(F32), jnp.uint32)
    vb = pltpu.bitcast(v_ref[...].astype(BF16).astype(F32), jnp.uint32)
    w = lax.shift_right_logical(ub, jnp.uint32(16)) | (vb & jnp.uint32(0xFFFF0000))
    o_ref[...] = w.reshape(o_ref.shape)


def _pack_table(u, v):
    e, d = u.shape
    tb = PACK_ROWS
    row = pl.BlockSpec((tb, d), lambda i: (i, 0))
    return pl.pallas_call(
        _pack_kernel,
        grid=(e // tb,),
        in_specs=[row, row],
        out_specs=pl.BlockSpec((tb, SUBLANE, LANE), lambda i: (i, 0, 0)),
        out_shape=jax.ShapeDtypeStruct((e, SUBLANE, LANE), jnp.uint32),
        compiler_params=_cparams(("arbitrary",)),
    )(u, v)


def _split_w(w):
    hi = w.astype(BF16)
    return hi, (w - hi.astype(F32)).astype(BF16)


def kernel(x, mem, mem_norm, rel_bias, mix_norm, ffn_norm, final_norm, w_o, w_mem_kv, a_w_in,
           a_kv_norm, a_w_uk, a_w_uv, b_w_in, b_conv_w, b_conv_b, b_dt_bias, b_a_log, b_d_skip,
           b_out_norm, peer_w_q, peer_sub_keys, peer_u, peer_v):
    b, s, d = x.shape
    n = b * s
    depth = w_o.shape[0]
    mem_width = MEM_HEADS * MEM_HEAD_DIM
    seq_width = w_o.shape[1] - mem_width
    a_heads = seq_width // A_HEAD_DIM
    ssm_heads = seq_width // SSM_HEAD_DIM
    conv_dim = seq_width + 2 * SSM_GROUPS * SSM_STATE
    assert s % ATT_Q == 0 and s % PROJ_TOKENS == 0 and n % ROUTE_TOKENS == 0 and s >= 4 * IDX_TOPK
    assert d == SUBLANE * LANE and n % GATHER_TOKENS == 0

    wk, wv = w_mem_kv[:, :, :mem_width], w_mem_kv[:, :, mem_width:]
    w_kv_pad = jnp.concatenate(
        [jnp.stack([_pad_heads(wk[l], MEM_HEADS, MEM_HEAD_DIM) for l in range(depth)]),
         jnp.stack([_pad_heads(wv[l], MEM_HEADS, MEM_HEAD_DIM) for l in range(depth)])],
        axis=-1).astype(BF16)
    kv_all = _mem_kv(mem, mem_norm, w_kv_pad)

    x2 = x.reshape(n, d)
    for i in range(depth):
        j = i // 2
        woa = w_o[i, :seq_width].astype(BF16)
        wob = jnp.pad(w_o[i, seq_width:].reshape(MEM_HEADS, MEM_HEAD_DIM, d),
                      ((0, 0), (0, LANE - MEM_HEAD_DIM), (0, 0))).astype(BF16)
        g_mix = mix_norm[i].reshape(1, d)
        if i % 2 == 0:
            w_in = a_w_in[j]
            o0 = seq_width
            o1 = o0 + A_KV_RANK
            o2 = o1 + IDX_HEADS * IDX_DIM
            o3 = o2 + IDX_DIM
            o4 = o3 + IDX_HEADS
            wq = w_in[:, :o0].astype(BF16)
            wc = w_in[:, o0:o1].astype(BF16)
            wm = _pad_heads(w_in[:, o4:], MEM_HEADS, MEM_HEAD_DIM).astype(BF16)
            w_idx = jnp.concatenate(
                [w_in[:, o1:o2], w_in[:, o2:o3], w_in[:, o2:o3],
                 jnp.pad(w_in[:, o3:o4], ((0, 0), (0, LANE - IDX_HEADS)))], axis=1)
            wih, wil = _split_w(w_idx)
            q, c, ct, iq, ka, iw, qm = _in_a(x2, g_mix, wq, wc, wm, wih, wil,
                                             a_kv_norm[j].reshape(1, A_KV_RANK))
            bias = _bias_tiles(rel_bias)
            r3 = lambda t: t.reshape(b, s, t.shape[-1])
            seq = _dsa(r3(q), r3(iq), r3(iw), r3(ka), r3(c),
                       ct.reshape(b, s // ATT_Q, A_KV_RANK, ATT_Q),
                       a_w_uk[j].astype(BF16), jnp.swapaxes(a_w_uv[j], 1, 2).astype(BF16), bias)
        else:
            w_in = b_w_in[j]
            o0 = seq_width
            o1 = o0 + conv_dim
            o2 = o1 + ssm_heads
            wz = w_in[:, :o0].astype(BF16)
            wx = w_in[:, o0:o1].astype(BF16)
            wm = _pad_heads(w_in[:, o2:], MEM_HEADS, MEM_HEAD_DIM).astype(BF16)
            wdh, wdl = _split_w(jnp.pad(w_in[:, o1:o2], ((0, 0), (0, LANE - ssm_heads))))
            z, xbc, dt, qm = _in_b(x2, g_mix, wz, wx, wm, wdh, wdl)
            padh = lambda t: jnp.pad(t.reshape(1, ssm_heads), ((0, 0), (0, LANE - ssm_heads)))
            seq = _ssd(xbc.reshape(b, s, conv_dim), z.reshape(b, s, seq_width),
                       dt.reshape(b, s, LANE), b_conv_w[j], b_conv_b[j].reshape(1, conv_dim),
                       padh(b_dt_bias[j]), padh(b_a_log[j]),
                       jnp.repeat(b_d_skip[j], SSM_HEAD_DIM).reshape(1, seq_width),
                       b_out_norm[j].reshape(1, seq_width))
        x2, x3 = _out_proj(x2, seq.reshape(n, seq_width), qm, kv_all[i], woa, wob, s)

        wqt = jnp.transpose(peer_w_q[i]).astype(BF16)
        hn3, eidx, gate = _route(x2, ffn_norm[i].reshape(1, d), wqt, peer_sub_keys[i].astype(BF16))
        table3 = _pack_table(peer_u[i], peer_v[i])
        x2 = _experts(eidx, gate, hn3, x3, final_norm.reshape(SUBLANE, LANE), table3,
                      i == depth - 1).reshape(n, d)
    return x2.reshape(b, s, d)
```

```python
import functools
import math

import jax
import jax.numpy as jnp
from jax import lax
from jax.experimental import pallas as pl
from jax.experimental.pallas import tpu as pltpu

F32 = jnp.float32
BF16 = jnp.bfloat16
I32 = jnp.int32

EPS = 1e-6
MEM_HEADS = 4
MEM_HEAD_DIM = 64
A_HEAD_DIM = 64
A_KV_RANK = 256
IDX_HEADS = 8
IDX_DIM = 64
IDX_TOPK = 256
REL_BUCKETS = 32
REL_MAX_DIST = 128
SSM_HEAD_DIM = 64
SSM_GROUPS = 2
SSM_STATE = 128
CONV_WIDTH = 4
SSD_CHUNK = 128
PEER_HEADS = 8
PEER_KEYS = 128
PEER_TOPK = 16

LANE = 128
SUBLANE = 8
INT_MIN = -(2 ** 31)
NEG_BIG = -1e30

PROJ_TOKENS = 512
ATT_Q = 256
ATT_HEADS = 6
ROUTE_TOKENS = 256
PACK_ROWS = 256
GATHER_GROUP = 8
GATHER_RING = 4
GATHER_AHEAD = 3
GATHER_TOKENS = GATHER_GROUP * GATHER_RING
VMEM_LIMIT = 56 * 1024 * 1024


def _cparams(sem):
    return pltpu.CompilerParams(dimension_semantics=sem, vmem_limit_bytes=VMEM_LIMIT)


def _fold_rows(x, op):
    parts = [x[k * SUBLANE:(k + 1) * SUBLANE] for k in range(x.shape[0] // SUBLANE)]
    while len(parts) > 1:
        nxt = [op(parts[k], parts[k + 1]) for k in range(0, len(parts) - 1, 2)]
        if len(parts) % 2:
            nxt.append(parts[-1])
        parts = nxt
    return parts[0]


def _rms(x, g):
    return x * lax.rsqrt(jnp.mean(x * x, axis=-1, keepdims=True) + EPS) * g


def _split2(a):
    hi = a.astype(BF16)
    lo = (a - hi.astype(F32)).astype(BF16)
    return hi, lo


def _dot(a, b):
    return jnp.dot(a, b, preferred_element_type=F32)


def _dot_nt(a, b):
    return lax.dot_general(a, b, (((1,), (1,)), ((), ())), preferred_element_type=F32)


def _mem_kv_kernel(mem_ref, g_ref, w_ref, out_ref):
    y = _rms(mem_ref[0], g_ref[...])
    out_ref[0, 0] = _dot(y.astype(BF16), w_ref[0]).astype(BF16)


def _mem_kv(mem, mem_norm, w_pad):
    b, m, d = mem.shape
    depth, _, wcols = w_pad.shape
    return pl.pallas_call(
        _mem_kv_kernel,
        grid=(depth, b),
        in_specs=[
            pl.BlockSpec((1, m, d), lambda l, i: (i, 0, 0)),
            pl.BlockSpec((1, d), lambda l, i: (0, 0)),
            pl.BlockSpec((1, d, wcols), lambda l, i: (l, 0, 0)),
        ],
        out_specs=pl.BlockSpec((1, 1, m, wcols), lambda l, i: (l, i, 0, 0)),
        out_shape=jax.ShapeDtypeStruct((depth, b, m, wcols), BF16),
        compiler_params=_cparams(("arbitrary", "arbitrary")),
    )(mem, mem_norm.reshape(1, d), w_pad)


def _bias_kernel(rb_ref, out_ref):
    h = pl.program_id(0)
    max_exact = REL_BUCKETS // 2
    far = rb_ref[REL_BUCKETS - 1, h]
    krow = lax.broadcasted_iota(I32, (ATT_Q, ATT_Q), 0)
    qcol = lax.broadcasted_iota(I32, (ATT_Q, ATT_Q), 1)
    for r in range(2):
        dist = qcol - krow + ATT_Q * r
        n = jnp.maximum(dist, 0)
        nf = jnp.maximum(n, max_exact).astype(F32)
        large = max_exact + (jnp.log(nf / max_exact) / math.log(REL_MAX_DIST / max_exact)
                             * (REL_BUCKETS - max_exact)).astype(I32)
        large = jnp.minimum(large, REL_BUCKETS - 1)
        bucket = jnp.where(n < max_exact, n, large)
        acc = jnp.zeros((ATT_Q, ATT_Q), F32)
        for k in range(REL_BUCKETS):
            acc = jnp.where(bucket == k, rb_ref[k, h], acc)
        out_ref[0, r] = acc - far


def _bias_tiles(rel_bias):
    heads = rel_bias.shape[1]
    return pl.pallas_call(
        _bias_kernel,
        grid=(heads,),
        in_specs=[pl.BlockSpec(memory_space=pltpu.SMEM)],
        out_specs=pl.BlockSpec((1, 2, ATT_Q, ATT_Q), lambda h: (h, 0, 0, 0)),
        out_shape=jax.ShapeDtypeStruct((heads, 2, ATT_Q, ATT_Q), F32),
        compiler_params=_cparams(("arbitrary",)),
    )(rel_bias)


def _in_a_kernel(x_ref, g_ref, wq_ref, wc_ref, wm_ref, wih_ref, wil_ref, kvn_ref,
                 q_ref, c_ref, ct_ref, iq_ref, ka_ref, iw_ref, qm_ref):
    h = _rms(x_ref[...], g_ref[...])
    h_hi, h_lo = _split2(h)
    q_ref[...] = _dot(h_hi, wq_ref[...]).astype(BF16)
    qm_ref[...] = _dot(h_hi, wm_ref[...]).astype(BF16)
    c = _rms(_dot(h_hi, wc_ref[...]), kvn_ref[...])
    c_ref[...] = c.astype(BF16)
    for j in range(PROJ_TOKENS // ATT_Q):
        ct_ref[j] = c[j * ATT_Q:(j + 1) * ATT_Q].T.astype(BF16)
    wih = wih_ref[...]
    ii = _dot(h_hi, wih) + _dot(h_lo, wih) + _dot(h_hi, wil_ref[...])
    iq_ref[...] = ii[:, :IDX_HEADS * IDX_DIM]
    kk = ii[:, IDX_HEADS * IDX_DIM:IDX_HEADS * IDX_DIM + LANE]
    kk_hi, kk_lo = _split2(kk)
    lane = lax.broadcasted_iota(I32, kk.shape, 1)
    half = jnp.where(lane < IDX_DIM, kk_hi, kk_lo)
    ka_ref[...] = jnp.concatenate([half, half], axis=1)
    iw_ref[...] = ii[:, IDX_HEADS * IDX_DIM + LANE:]


def _in_a(x2, g, wq, wc, wm, wih, wil, kvn):
    n, d = x2.shape
    tt = PROJ_TOKENS
    full = lambda a: pl.BlockSpec(a.shape, lambda i: (0,) * a.ndim)
    row = lambda w: pl.BlockSpec((tt, w), lambda i: (i, 0))
    nblk = tt // ATT_Q
    outs = [
        jax.ShapeDtypeStruct((n, wq.shape[1]), BF16),
        jax.ShapeDtypeStruct((n, A_KV_RANK), BF16),
        jax.ShapeDtypeStruct((n // ATT_Q, A_KV_RANK, ATT_Q), BF16),
        jax.ShapeDtypeStruct((n, IDX_HEADS * IDX_DIM), F32),
        jax.ShapeDtypeStruct((n, 2 * LANE), BF16),
        jax.ShapeDtypeStruct((n, LANE), F32),
        jax.ShapeDtypeStruct((n, wm.shape[1]), BF16),
    ]
    out_specs = [
        row(wq.shape[1]), row(A_KV_RANK),
        pl.BlockSpec((nblk, A_KV_RANK, ATT_Q), lambda i: (i, 0, 0)),
        row(IDX_HEADS * IDX_DIM), row(2 * LANE), row(LANE), row(wm.shape[1]),
    ]
    return pl.pallas_call(
        _in_a_kernel,
        grid=(n // tt,),
        in_specs=[row(d), full(g), full(wq), full(wc), full(wm), full(wih), full(wil), full(kvn)],
        out_specs=out_specs,
        out_shape=outs,
        compiler_params=_cparams(("arbitrary",)),
    )(x2, g, wq, wc, wm, wih, wil, kvn)


def _dsa_kernel(q_ref, iq_ref, iw_ref, ka_ref, c_ref, ct_ref, wuk_ref, wuvt_ref, bias_ref,
                o_ref, key_s, lg_s, qbt_s, qt_s, olat_s, ot_s, *, heads):
    i = pl.program_id(1)
    nch = i + 1
    t0 = i * ATT_Q
    Q = ATT_Q
    krow = lax.broadcasted_iota(I32, (Q, Q), 0)
    qcol = lax.broadcasted_iota(I32, (Q, Q), 1)

    iqv = iq_ref[0]
    lane = lax.broadcasted_iota(I32, (Q, LANE), 1)
    for j in range(IDX_HEADS // 2):
        v = iqv[:, j * LANE:(j + 1) * LANE]
        r = pltpu.roll(v, IDX_DIM, 1)
        for hh, dup in ((2 * j, jnp.where(lane < IDX_DIM, v, r)),
                        (2 * j + 1, jnp.where(lane < IDX_DIM, r, v))):
            hi = dup.astype(BF16)
            lo = (dup - hi.astype(F32)).astype(BF16)
            qbt_s[hh, 0:LANE, :] = hi.astype(F32).T.astype(BF16)
            qbt_s[hh, LANE:2 * LANE, :] = lo.astype(F32).T.astype(BF16)
    wt = iw_ref[0].T * (IDX_HEADS ** -0.5)
    qt_s[...] = q_ref[0].astype(F32).T.astype(BF16)

    def score_chunk(c, carry):
        ka = ka_ref[0, pl.ds(pl.multiple_of(c * Q, Q), Q), :]
        acc = jnp.zeros((Q, Q), F32)
        for hh in range(IDX_HEADS):
            z = _dot(ka, qbt_s[hh])
            acc = acc + jnp.maximum(z, 0.0) * wt[hh:hh + 1, :]
        acc = acc * (IDX_DIM ** -0.5)
        bits = pltpu.bitcast(acc, I32)
        skey = jnp.where(bits < 0, bits ^ 0x7FFFFFFF, bits)
        causal = (krow + c * Q) <= (qcol + t0)
        key_s[pl.ds(pl.multiple_of(c * Q, Q), Q), :] = jnp.where(causal, skey, INT_MIN)
        return carry

    lax.fori_loop(0, nch, score_chunk, 0)

    def count_ge(thr):
        def body(c, acc):
            blk = key_s[pl.ds(pl.multiple_of(c * Q, Q), Q), :]
            return acc + _fold_rows((blk >= thr).astype(I32), jnp.add)
        acc = lax.fori_loop(0, nch, body, jnp.zeros((SUBLANE, Q), I32))
        return jnp.sum(acc, axis=0, keepdims=True)

    c0 = count_ge(jnp.zeros((1, Q), I32))
    has_k = c0 >= IDX_TOPK
    thr = jnp.where(has_k, 0, INT_MIN).astype(I32)
    cnt = jnp.where(has_k, c0, nch * Q)
    reachable = (lax.broadcasted_iota(I32, (1, Q), 1) + t0 + 1) >= IDX_TOPK

    def unsettled(cnt):
        return jnp.max(jnp.where((cnt != IDX_TOPK) & reachable, 1, 0)) > 0

    def bisect(it, state):
        thr, cnt = state
        cand = thr + lax.shift_left(jnp.int32(1), 30 - it)
        c = count_ge(cand)
        ok = c >= IDX_TOPK
        return jnp.where(ok, cand, thr), jnp.where(ok, c, cnt)

    thr, cnt = lax.fori_loop(0, 31, bisect, (thr, cnt))
    thr = jnp.maximum(thr, INT_MIN + 1)

    nbits = max(1, int(math.ceil(math.log2(ka_ref.shape[1] + 1))))

    def tie_cut(thr):
        need = IDX_TOPK - count_ge(thr + 1)

        def count_tie_below(bound):
            def body(c, acc):
                blk = key_s[pl.ds(pl.multiple_of(c * Q, Q), Q), :]
                hit = (blk == thr) & ((krow + c * Q) < bound)
                return acc + _fold_rows(hit.astype(I32), jnp.add)
            acc = lax.fori_loop(0, nch, body, jnp.zeros((SUBLANE, Q), I32))
            return jnp.sum(acc, axis=0, keepdims=True)

        def tie_bisect(it, p0):
            cand = p0 + lax.shift_left(jnp.int32(1), nbits - 1 - it)
            return jnp.where(count_tie_below(cand) < need, cand, p0)

        return lax.fori_loop(0, nbits, tie_bisect, jnp.zeros((1, Q), I32)) + 1

    pcut = lax.cond(unsettled(cnt), tie_cut, lambda thr: jnp.full((1, Q), 2 ** nbits, I32), thr)

    def mask_chunk(c, carry):
        off = pl.multiple_of(c * Q, Q)
        key = key_s[pl.ds(off, Q), :]
        sel = (key > thr) | ((key == thr) & ((krow + c * Q) < pcut))
        key_s[pl.ds(off, Q), :] = pltpu.bitcast(jnp.where(sel, 0.0, NEG_BIG).astype(F32), I32)
        return carry

    lax.fori_loop(0, nch, mask_chunk, 0)

    HB = ATT_HEADS

    def head_group(hg, carry):
        qlts = []
        for hh in range(HB):
            h = hg * HB + hh
            qh = qt_s[pl.ds(pl.multiple_of(h * A_HEAD_DIM, A_HEAD_DIM), A_HEAD_DIM), :]
            qlts.append((_dot(wuk_ref[h], qh) * (A_HEAD_DIM ** -0.5)).astype(BF16))
        qlt = jnp.concatenate(qlts, axis=1)

        def logits_chunk(c, m_acc, band):
            off = pl.multiple_of(c * Q, Q)
            lg = _dot(c_ref[0, pl.ds(off, Q), :], qlt)
            mask = pltpu.bitcast(key_s[pl.ds(off, Q), :], F32)
            new = []
            for hh in range(HB):
                lgh = lg[:, hh * Q:(hh + 1) * Q] + mask
                if band is not None:
                    lgh = lgh + bias_ref[hg * HB + hh, band]
                lg_s[pl.ds(off, Q), hh * Q:(hh + 1) * Q] = lgh
                new.append(jnp.maximum(m_acc[:, hh * Q:(hh + 1) * Q], _fold_rows(lgh, jnp.maximum)))
            return jnp.concatenate(new, axis=1)

        m_acc = jnp.full((SUBLANE, HB * Q), NEG_BIG, F32)
        m_acc = lax.fori_loop(0, nch - 2, lambda c, m: logits_chunk(c, m, None), m_acc)
        m_acc = lax.cond(nch >= 2, lambda m: logits_chunk(nch - 2, m, 1), lambda m: m, m_acc)
        m_acc = logits_chunk(nch - 1, m_acc, 0)
        m = jnp.max(m_acc, axis=0, keepdims=True)

        olat_s[...] = jnp.zeros_like(olat_s)

        def pv_chunk(c, s_acc):
            off = pl.multiple_of(c * Q, Q)
            p = jnp.exp(lg_s[pl.ds(off, Q), :] - m)
            olat_s[...] += _dot(ct_ref[0, c], p.astype(BF16))
            return s_acc + _fold_rows(p, jnp.add)

        s_acc = lax.fori_loop(0, nch, pv_chunk, jnp.zeros((SUBLANE, HB * Q), F32))
        s = jnp.sum(s_acc, axis=0, keepdims=True)
        for hh in range(HB):
            h = hg * HB + hh
            ol = olat_s[:, hh * Q:(hh + 1) * Q].astype(BF16)
            oh = _dot(wuvt_ref[h], ol) / s[:, hh * Q:(hh + 1) * Q]
            ot_s[pl.ds(pl.multiple_of(h * A_HEAD_DIM, A_HEAD_DIM), A_HEAD_DIM), :] = oh
        return carry

    lax.fori_loop(0, heads // HB, head_group, 0)
    o_ref[0] = ot_s[...].T.astype(BF16)


def _dsa(q, iq, iw, ka, c, ct, wuk, wuvt, bias):
    b, s, w = q.shape
    heads = wuk.shape[0]
    nq = s // ATT_Q
    assert heads % ATT_HEADS == 0
    once = pl.Buffered(1)
    full = lambda a: pl.BlockSpec(a.shape, lambda bi, i: (0,) * a.ndim, pipeline_mode=once)
    blk = lambda width: pl.BlockSpec((1, ATT_Q, width), lambda bi, i: (bi, i, 0))
    per_b = lambda width: pl.BlockSpec((1, s, width), lambda bi, i: (bi, 0, 0), pipeline_mode=once)
    return pl.pallas_call(
        functools.partial(_dsa_kernel, heads=heads),
        grid=(b, nq),
        in_specs=[
            blk(w), blk(iq.shape[2]), blk(iw.shape[2]),
            per_b(ka.shape[2]), per_b(c.shape[2]),
            pl.BlockSpec((1, nq, A_KV_RANK, ATT_Q), lambda bi, i: (bi, 0, 0, 0), pipeline_mode=once),
            full(wuk), full(wuvt), full(bias),
        ],
        out_specs=blk(w),
        out_shape=jax.ShapeDtypeStruct((b, s, w), BF16),
        scratch_shapes=[
            pltpu.VMEM((s, ATT_Q), I32),
            pltpu.VMEM((s, ATT_HEADS * ATT_Q), F32),
            pltpu.VMEM((IDX_HEADS, 2 * LANE, ATT_Q), BF16),
            pltpu.VMEM((w, ATT_Q), BF16),
            pltpu.VMEM((A_KV_RANK, ATT_HEADS * ATT_Q), F32),
            pltpu.VMEM((w, ATT_Q), F32),
        ],
        compiler_params=_cparams(("arbitrary", "arbitrary")),
    )(q, iq, iw, ka, c, ct, wuk, wuvt, bias)


def _out_kernel(x_ref, seq_ref, qm_ref, kv_ref, woa_ref, wob_ref, o_ref, o3_ref):
    acc = x_ref[...] + _dot(seq_ref[...], woa_ref[...])
    qm = qm_ref[...]
    kv = kv_ref[0]
    for h in range(MEM_HEADS):
        k = kv[:, h * LANE:(h + 1) * LANE]
        v = kv[:, (MEM_HEADS + h) * LANE:(MEM_HEADS + h + 1) * LANE]
        lg = _dot_nt(qm[:, h * LANE:(h + 1) * LANE], k) * (MEM_HEAD_DIM ** -0.5)
        p = jnp.exp(lg - jnp.max(lg, axis=-1, keepdims=True))
        oh = _dot(p.astype(BF16), v) / jnp.sum(p, axis=-1, keepdims=True)
        acc = acc + _dot(oh.astype(BF16), wob_ref[h])
    o_ref[...] = acc
    o3_ref[...] = acc.reshape(o3_ref.shape)


def _out_proj(x2, seq2, qm2, kv, woa, wob, seq_len):
    n, d = x2.shape
    tt = PROJ_TOKENS
    per_seq = seq_len // tt
    full = lambda a: pl.BlockSpec(a.shape, lambda i: (0,) * a.ndim)
    row = lambda w: pl.BlockSpec((tt, w), lambda i: (i, 0))
    return pl.pallas_call(
        _out_kernel,
        grid=(n // tt,),
        in_specs=[row(d), row(seq2.shape[1]), row(qm2.shape[1]),
                  pl.BlockSpec((1,) + kv.shape[1:], lambda i: (i // per_seq, 0, 0)),
                  full(woa), full(wob)],
        out_specs=[row(d), pl.BlockSpec((tt, SUBLANE, LANE), lambda i: (i, 0, 0))],
        out_shape=[jax.ShapeDtypeStruct((n, d), F32),
                   jax.ShapeDtypeStruct((n, SUBLANE, LANE), F32)],
        compiler_params=_cparams(("arbitrary",)),
    )(x2, seq2, qm2, kv, woa, wob)


def _in_b_kernel(x_ref, g_ref, wz_ref, wx_ref, wm_ref, wdh_ref, wdl_ref,
                 z_ref, xbc_ref, dt_ref, qm_ref):
    h = _rms(x_ref[...], g_ref[...])
    h_hi, h_lo = _split2(h)
    z_ref[...] = _dot(h_hi, wz_ref[...]).astype(BF16)
    xbc_ref[...] = _dot(h_hi, wx_ref[...])
    qm_ref[...] = _dot(h_hi, wm_ref[...]).astype(BF16)
    wdh = wdh_ref[...]
    dt_ref[...] = _dot(h_hi, wdh) + _dot(h_lo, wdh) + _dot(h_hi, wdl_ref[...])


def _in_b(x2, g, wz, wx, wm, wdh, wdl):
    n, d = x2.shape
    tt = PROJ_TOKENS
    full = lambda a: pl.BlockSpec(a.shape, lambda i: (0,) * a.ndim)
    row = lambda w: pl.BlockSpec((tt, w), lambda i: (i, 0))
    return pl.pallas_call(
        _in_b_kernel,
        grid=(n // tt,),
        in_specs=[row(d), full(g), full(wz), full(wx), full(wm), full(wdh), full(wdl)],
        out_specs=[row(wz.shape[1]), row(wx.shape[1]), row(LANE), row(wm.shape[1])],
        out_shape=[jax.ShapeDtypeStruct((n, wz.shape[1]), BF16),
                   jax.ShapeDtypeStruct((n, wx.shape[1]), F32),
                   jax.ShapeDtypeStruct((n, LANE), F32),
                   jax.ShapeDtypeStruct((n, wm.shape[1]), BF16)],
        compiler_params=_cparams(("arbitrary",)),
    )(x2, g, wz, wx, wm, wdh, wdl)


def _ssd_kernel(xbc_ref, z_ref, dt_ref, cw_ref, cb_ref, dtb_ref, alog_ref, dsk_ref, on_ref,
                o_ref, tail_s, xpad_s, h_s, *, seq_width):
    L = SSD_CHUNK
    ci = pl.program_id(1)

    @pl.when(ci == 0)
    def _():
        tail_s[...] = jnp.zeros_like(tail_s)
        h_s[...] = jnp.zeros_like(h_s)

    xr = xbc_ref[0]
    xpad_s[0:SUBLANE, :] = tail_s[...]
    xpad_s[SUBLANE:SUBLANE + L, :] = xr
    tail_s[...] = xr[L - SUBLANE:L, :]
    conv = cb_ref[...] + jnp.zeros_like(xr)
    for j in range(CONV_WIDTH):
        off = SUBLANE - (CONV_WIDTH - 1) + j
        conv = conv + cw_ref[j:j + 1, :] * xpad_s[off:off + L, :]
    xc = conv * jax.nn.sigmoid(conv)
    gw = SSM_STATE
    bm = xc[:, seq_width:seq_width + SSM_GROUPS * gw]
    cm = xc[:, seq_width + SSM_GROUPS * gw:seq_width + 2 * SSM_GROUPS * gw]

    dt = jax.nn.softplus(dt_ref[0] + dtb_ref[...])
    a = -jnp.exp(alog_ref[...])
    da = dt * a
    row_i = lax.broadcasted_iota(I32, (L, L), 0)
    col_i = lax.broadcasted_iota(I32, (L, L), 1)
    tri = row_i >= col_i
    tri_b = tri.astype(BF16)
    d1 = da.astype(BF16)
    r1 = da - d1.astype(F32)
    d2 = r1.astype(BF16)
    d3 = (r1 - d2.astype(F32)).astype(BF16)
    acs = _dot(tri_b, d1) + _dot(tri_b, d2) + _dot(tri_b, d3)
    acs_t = acs.T
    a_last = acs[L - 1:L, :]
    e_acs = jnp.exp(acs)
    e_end = jnp.exp(a_last - acs)
    lane = lax.broadcasted_iota(I32, (L, LANE), 1)
    lo_half = lane < SSM_HEAD_DIM
    lane1 = lax.broadcasted_iota(I32, (1, LANE), 1)

    def pair_cols(m, r):
        return jnp.where(lo_half, m[:, r:r + 1], m[:, r + 1:r + 2])

    heads_per_group = (seq_width // SSM_HEAD_DIM) // SSM_GROUPS
    pairs_per_group = heads_per_group // 2
    pieces = []
    ssq = jnp.zeros((L, 1), F32)
    for g in range(SSM_GROUPS):
        bg = bm[:, g * gw:(g + 1) * gw]
        cg = cm[:, g * gw:(g + 1) * gw]
        bg_b = bg.astype(BF16)
        cg_b = cg.astype(BF16)
        bgt_b = bg.T.astype(BF16)
        cb = _dot_nt(cg_b, bg_b)
        for jp in range(pairs_per_group):
            r = g * heads_per_group + 2 * jp
            col0 = (g * pairs_per_group + jp) * LANE
            xs = xc[:, col0:col0 + LANE]
            xdt = (xs * pair_cols(dt, r)).astype(BF16)
            ys = []
            for rr in (r, r + 1):
                seg = acs[:, rr:rr + 1] - acs_t[rr:rr + 1, :]
                dec = jnp.exp(jnp.where(tri, seg, -jnp.inf))
                ys.append(_dot((cb * dec).astype(BF16), xdt))
            y = jnp.where(lo_half, ys[0], ys[1])
            hcol = jp * LANE
            h_prev = h_s[g, :, hcol:hcol + LANE]
            y = y + _dot(cg_b, h_prev.astype(BF16)) * pair_cols(e_acs, r)
            xw = (xs * pair_cols(dt * e_end, r)).astype(BF16)
            st = _dot(bgt_b, xw)
            cd = jnp.where(lane1 < SSM_HEAD_DIM, jnp.exp(a_last[:, r:r + 1]),
                           jnp.exp(a_last[:, r + 1:r + 2]))
            h_s[g, :, hcol:hcol + LANE] = h_prev * cd + st
            y = y + xs * dsk_ref[:, col0:col0 + LANE]
            zz = z_ref[0, :, col0:col0 + LANE].astype(F32)
            y = y * (zz * jax.nn.sigmoid(zz))
            ssq = ssq + jnp.sum(y * y, axis=-1, keepdims=True)
            pieces.append(y)
    scale = lax.rsqrt(ssq / seq_width + EPS)
    for k, y in enumerate(pieces):
        o_ref[0, :, k * LANE:(k + 1) * LANE] = (y * scale * on_ref[:, k * LANE:(k + 1) * LANE]).astype(BF16)


def _ssd(xbc, z, dt, cw, cb, dtb, alog, dsk, onorm):
    b, s, cdim = xbc.shape
    w = z.shape[2]
    L = SSD_CHUNK
    full = lambda a: pl.BlockSpec(a.shape, lambda bi, i: (0,) * a.ndim)
    blk = lambda width: pl.BlockSpec((1, L, width), lambda bi, i: (bi, i, 0))
    return pl.pallas_call(
        functools.partial(_ssd_kernel, seq_width=w),
        grid=(b, s // L),
        in_specs=[blk(cdim), blk(w), blk(LANE), full(cw), full(cb), full(dtb), full(alog),
                  full(dsk), full(onorm)],
        out_specs=blk(w),
        out_shape=jax.ShapeDtypeStruct((b, s, w), BF16),
        scratch_shapes=[
            pltpu.VMEM((SUBLANE, cdim), F32),
            pltpu.VMEM((SUBLANE + L, cdim), F32),
            pltpu.VMEM((SSM_GROUPS, SSM_STATE, w // SSM_GROUPS), F32),
        ],
        compiler_params=_cparams(("arbitrary", "arbitrary")),
    )(xbc, z, dt, cw, cb, dtb, alog, dsk, onorm)


def _top_rows(vals, k, val_out, idx_out, payload=None):
    rows = lax.broadcasted_iota(I32, vals.shape, 0)
    big = vals.shape[0]
    for j in range(k):
        m = jnp.max(_fold_rows(vals, jnp.maximum), axis=0, keepdims=True)
        am = jnp.min(_fold_rows(jnp.where(vals == m, rows, big), jnp.minimum), axis=0, keepdims=True)
        hit = rows == am
        val_out[j:j + 1, :] = m
        if payload is None:
            idx_out[j:j + 1, :] = am
        else:
            idx_out[j:j + 1, :] = jnp.sum(_fold_rows(jnp.where(hit, payload, 0), jnp.add),
                                          axis=0, keepdims=True)
        vals = jnp.where(hit, -jnp.inf, vals)


def _route_kernel(x_ref, g_ref, wqt_ref, sk_ref, hn_ref, e_ref, gate_ref,
                  ts_s, ti_s, bs_s, et_s, gt_s, cs_s, ci_s):
    T = ROUTE_TOKENS
    hn = _rms(x_ref[...], g_ref[...])
    hn_ref[...] = hn.reshape(hn_ref.shape)
    hb = hn.astype(BF16)
    qt = _dot_nt(wqt_ref[...], hb)
    half = qt.shape[0] // (PEER_HEADS * 2)
    for hd in range(PEER_HEADS):
        for side in range(2):
            r0 = (hd * 2 + side) * half
            sc = _dot(sk_ref[side], qt[r0:r0 + half].astype(BF16))
            _top_rows(sc, PEER_TOPK, ts_s.at[side], ti_s.at[side])
        off = 0
        for a in range(PEER_TOPK):
            nb = PEER_TOPK // (a + 1)
            cs_s[off:off + nb, :] = ts_s[0, a:a + 1, :] + ts_s[1, 0:nb, :]
            ci_s[off:off + nb, :] = ti_s[0, a:a + 1, :] * PEER_KEYS + ti_s[1, 0:nb, :]
            off += nb
        cs_s[off:, :] = jnp.full((cs_s.shape[0] - off, T), -jnp.inf, F32)
        ci_s[off:, :] = jnp.zeros((ci_s.shape[0] - off, T), I32)
        _top_rows(cs_s[...], PEER_TOPK, bs_s, et_s.at[pl.ds(hd * PEER_TOPK, PEER_TOPK)],
                  payload=ci_s[...])
        best = bs_s[...]
        p = jnp.exp(best - jnp.max(best, axis=0, keepdims=True))
        gt_s[hd * PEER_TOPK:(hd + 1) * PEER_TOPK, :] = p / jnp.sum(p, axis=0, keepdims=True)
    e_ref[...] = et_s[...].T
    gate_ref[...] = gt_s[...].T


def _route(x2, g, wqt, sk):
    n, d = x2.shape
    tt = ROUTE_TOKENS
    npair = PEER_HEADS * PEER_TOPK
    ncand = sum(PEER_TOPK // (a + 1) for a in range(PEER_TOPK))
    ncand = -(-ncand // SUBLANE) * SUBLANE
    full = lambda a: pl.BlockSpec(a.shape, lambda i: (0,) * a.ndim)
    row = lambda w: pl.BlockSpec((tt, w), lambda i: (i, 0))
    return pl.pallas_call(
        _route_kernel,
        grid=(n // tt,),
        in_specs=[row(d), full(g), full(wqt), full(sk)],
        out_specs=[pl.BlockSpec((tt, SUBLANE, LANE), lambda i: (i, 0, 0)), row(npair), row(npair)],
        out_shape=[jax.ShapeDtypeStruct((n, SUBLANE, LANE), F32),
                   jax.ShapeDtypeStruct((n, npair), I32),
                   jax.ShapeDtypeStruct((n, npair), F32)],
        scratch_shapes=[
            pltpu.VMEM((2, PEER_TOPK, tt), F32),
            pltpu.VMEM((2, PEER_TOPK, tt), I32),
            pltpu.VMEM((PEER_TOPK, tt), F32),
            pltpu.VMEM((npair, tt), I32),
            pltpu.VMEM((npair, tt), F32),
            pltpu.VMEM((ncand, tt), F32),
            pltpu.VMEM((ncand, tt), I32),
        ],
        compiler_params=_cparams(("arbitrary",)),
    )(x2, g, wqt, sk)


def _peer_consts(npair):
    rows = npair * SUBLANE
    sel = (lax.shift_right_logical(lax.broadcasted_iota(I32, (rows, npair), 0), 3)
           == lax.broadcasted_iota(I32, (rows, npair), 1)).astype(BF16)
    sel_t = (lax.broadcasted_iota(I32, (npair, rows), 0)
             == lax.shift_right_logical(lax.broadcasted_iota(I32, (npair, rows), 1), 3)).astype(BF16)
    diag = (lax.broadcasted_iota(I32, (SUBLANE, rows), 0)
            == (lax.broadcasted_iota(I32, (SUBLANE, rows), 1) & (SUBLANE - 1)))
    return sel, sel_t, diag


def _peer_group(load_rows, between, t0, gate_ref, hn_ref, x_ref, fn_ref, o_ref, consts, final_norm):
    sel, sel_t, diag = consts
    G = GATHER_GROUP
    rows = sel.shape[0]
    tok = lax.broadcasted_iota(I32, (G, rows), 0)
    partial = jnp.zeros((G, rows), F32)
    for j in range(G):
        between(j, 0)
        w = load_rows(j).reshape(rows, LANE)
        mu = pltpu.bitcast(lax.shift_left(w, jnp.uint32(16)), F32).astype(BF16)
        gj = _dot_nt(hn_ref[t0 + j].astype(BF16), mu)
        row = jnp.sum(jnp.where(diag, gj, 0.0), axis=0, keepdims=True)
        partial = jnp.where(tok == j, row, partial)
    dots = _dot(partial.astype(BF16), sel)
    for j in range(G // 2):
        between(j, 2)
    gelu = 0.5 * dots * (1.0 + lax.erf(dots * (2.0 ** -0.5)))
    act = (gelu * gate_ref[t0:t0 + G, :]).astype(BF16)
    arep = _dot(act, sel_t)
    for j in range(G // 2, G):
        between(j, 2)
    ys = []
    for j in range(G):
        between(j, 1)
        w = load_rows(j).reshape(rows, LANE)
        mv = pltpu.bitcast(w & jnp.uint32(0xFFFF0000), F32).astype(BF16)
        aexp = jnp.where(diag, arep[j:j + 1, :], 0.0).astype(BF16)
        y = _dot(aexp, mv) + x_ref[t0 + j]
        if final_norm:
            ssq = jnp.sum(jnp.sum(y * y, axis=1, keepdims=True), axis=0, keepdims=True)
            y = y * lax.rsqrt(ssq / (SUBLANE * LANE) + EPS) * fn_ref[...]
        ys.append(y)
    o_ref[t0:t0 + G, :] = jnp.concatenate(ys, axis=0).reshape(G, SUBLANE * LANE)


def _expert_kernel(idx_ref, idxn_ref, gate_ref, hn_ref, x_ref, fn_ref, tab_ref, o_ref, buf, sem,
                   *, final_norm):
    T = GATHER_TOKENS
    G = GATHER_GROUP
    npair = idx_ref.shape[1]
    ngroups = T // G
    rows = npair * SUBLANE

    assert ngroups == GATHER_RING and GATHER_AHEAD < ngroups
    step = pl.program_id(0)
    cuts = (0, (3 * npair) // 8, (6 * npair) // 8, npair)

    def issue(iref, g, j, part):
        for p in range(cuts[part], cuts[part + 1]):
            e = iref[g * G + j, p]
            pltpu.make_async_copy(tab_ref.at[e], buf.at[g * G + j, p], sem.at[g]).start(priority=p % 2)

    def wait_group(g):
        for j in range(G):
            pltpu.make_async_copy(tab_ref.at[pl.ds(0, npair)], buf.at[g * G + j], sem.at[g]).wait()

    @pl.when(step == 0)
    def _():
        for g in range(GATHER_AHEAD):
            for j in range(G):
                for part in range(len(cuts) - 1):
                    issue(idx_ref, g, j, part)

    consts = _peer_consts(npair)
    for g in range(ngroups):
        nxt = g + GATHER_AHEAD
        nref, ng = (idx_ref, nxt) if nxt < ngroups else (idxn_ref, nxt - ngroups)
        wait_group(g)
        _peer_group(lambda j: buf[g * G + j], lambda j, part: issue(nref, ng, j, part), g * G,
                    gate_ref, hn_ref, x_ref, fn_ref, o_ref, consts, final_norm)

    @pl.when(step == pl.num_programs(0) - 1)
    def _():
        for g in range(GATHER_AHEAD):
            wait_group(g)


def _experts(idx, gate, hn3, x3, fnorm3, table3, final_norm):
    n = x3.shape[0]
    tt = GATHER_TOKENS
    npair = idx.shape[1]
    tile = lambda: pl.BlockSpec((tt, SUBLANE, LANE), lambda i: (i, 0, 0))
    nsteps = n // tt
    return pl.pallas_call(
        functools.partial(_expert_kernel, final_norm=final_norm),
        grid=(nsteps,),
        in_specs=[
            pl.BlockSpec((tt, npair), lambda i: (i, 0), memory_space=pltpu.SMEM),
            pl.BlockSpec((tt, npair), lambda i: (jnp.minimum(i + 1, nsteps - 1), 0),
                         memory_space=pltpu.SMEM),
            pl.BlockSpec((tt, npair), lambda i: (i, 0)),
            tile(), tile(),
            pl.BlockSpec((SUBLANE, LANE), lambda i: (0, 0)),
            pl.BlockSpec(memory_space=pl.ANY),
        ],
        out_specs=pl.BlockSpec((tt, SUBLANE * LANE), lambda i: (i, 0)),
        out_shape=jax.ShapeDtypeStruct((n, SUBLANE * LANE), F32),
        scratch_shapes=[
            pltpu.VMEM((GATHER_RING * GATHER_GROUP, npair, SUBLANE, LANE), jnp.uint32),
            pltpu.SemaphoreType.DMA((GATHER_RING,)),
        ],
        compiler_params=_cparams(("arbitrary",)),
    )(idx, idx, gate, hn3, x3, fnorm3, table3)


def _pad_heads(w, heads, dim):
    d = w.shape[0]
    w = w.reshape(d, heads, dim)
    return jnp.pad(w, ((0, 0), (0, 0), (0, LANE - dim))).reshape(d, heads * LANE)


def _pack_kernel(u_ref, v_ref, o_ref):
    ub = pltpu.bitcast(u_ref[...].astype(BF16).astype(F32), jnp.uint32)
    vb = pltpu.bitcast(v_ref[...].astype(BF16).astype(F32), jnp.uint32)
    w = lax.shift_right_logical(ub, jnp.uint32(16)) | (vb & jnp.uint32(0xFFFF0000))
    o_ref[...] = w.reshape(o_ref.shape)


def _pack_table(u_all, v_all, layer):
    _, e, d = u_all.shape
    tb = PACK_ROWS
    row = pl.BlockSpec((None, tb, d), lambda i: (layer, i, 0))
    return pl.pallas_call(
        _pack_kernel,
        grid=(e // tb,),
        in_specs=[row, row],
        out_specs=pl.BlockSpec((tb, SUBLANE, LANE), lambda i: (i, 0, 0)),
        out_shape=jax.ShapeDtypeStruct((e, SUBLANE, LANE), jnp.uint32),
        compiler_params=_cparams(("arbitrary",)),
    )(u_all, v_all)


def _split_w(w):
    hi = w.astype(BF16)
    return hi, (w - hi.astype(F32)).astype(BF16)


def kernel(x, mem, mem_norm, rel_bias, mix_norm, ffn_norm, final_norm, w_o, w_mem_kv, a_w_in,
           a_kv_norm, a_w_uk, a_w_uv, b_w_in, b_conv_w, b_conv_b, b_dt_bias, b_a_log, b_d_skip,
           b_out_norm, peer_w_q, peer_sub_keys, peer_u, peer_v):
    b, s, d = x.shape
    n = b * s
    depth = w_o.shape[0]
    mem_width = MEM_HEADS * MEM_HEAD_DIM
    seq_width = w_o.shape[1] - mem_width
    a_heads = seq_width // A_HEAD_DIM
    ssm_heads = seq_width // SSM_HEAD_DIM
    conv_dim = seq_width + 2 * SSM_GROUPS * SSM_STATE
    assert s % ATT_Q == 0 and s % PROJ_TOKENS == 0 and n % ROUTE_TOKENS == 0 and s >= 4 * IDX_TOPK
    assert d == SUBLANE * LANE and n % GATHER_TOKENS == 0

    wk, wv = w_mem_kv[:, :, :mem_width], w_mem_kv[:, :, mem_width:]
    w_kv_pad = jnp.concatenate(
        [jnp.stack([_pad_heads(wk[l], MEM_HEADS, MEM_HEAD_DIM) for l in range(depth)]),
         jnp.stack([_pad_heads(wv[l], MEM_HEADS, MEM_HEAD_DIM) for l in range(depth)])],
        axis=-1).astype(BF16)
    kv_all = _mem_kv(mem, mem_norm, w_kv_pad)

    x2 = x.reshape(n, d)
    for i in range(depth):
        j = i // 2
        woa = w_o[i, :seq_width].astype(BF16)
        wob = jnp.pad(w_o[i, seq_width:].reshape(MEM_HEADS, MEM_HEAD_DIM, d),
                      ((0, 0), (0, LANE - MEM_HEAD_DIM), (0, 0))).astype(BF16)
        g_mix = mix_norm[i].reshape(1, d)
        if i % 2 == 0:
            w_in = a_w_in[j]
            o0 = seq_width
            o1 = o0 + A_KV_RANK
            o2 = o1 + IDX_HEADS * IDX_DIM
            o3 = o2 + IDX_DIM
            o4 = o3 + IDX_HEADS
            wq = w_in[:, :o0].astype(BF16)
            wc = w_in[:, o0:o1].astype(BF16)
            wm = _pad_heads(w_in[:, o4:], MEM_HEADS, MEM_HEAD_DIM).astype(BF16)
            w_idx = jnp.concatenate(
                [w_in[:, o1:o2], w_in[:, o2:o3], w_in[:, o2:o3],
                 jnp.pad(w_in[:, o3:o4], ((0, 0), (0, LANE - IDX_HEADS)))], axis=1)
            wih, wil = _split_w(w_idx)
            q, c, ct, iq, ka, iw, qm = _in_a(x2, g_mix, wq, wc, wm, wih, wil,
                                             a_kv_norm[j].reshape(1, A_KV_RANK))
            bias = _bias_tiles(rel_bias)
            r3 = lambda t: t.reshape(b, s, t.shape[-1])
            seq = _dsa(r3(q), r3(iq), r3(iw), r3(ka), r3(c),
                       ct.reshape(b, s // ATT_Q, A_KV_RANK, ATT_Q),
                       a_w_uk[j].astype(BF16), jnp.swapaxes(a_w_uv[j], 1, 2).astype(BF16), bias)
        else:
            w_in = b_w_in[j]
            o0 = seq_width
            o1 = o0 + conv_dim
            o2 = o1 + ssm_heads
            wz = w_in[:, :o0].astype(BF16)
            wx = w_in[:, o0:o1].astype(BF16)
            wm = _pad_heads(w_in[:, o2:], MEM_HEADS, MEM_HEAD_DIM).astype(BF16)
            wdh, wdl = _split_w(jnp.pad(w_in[:, o1:o2], ((0, 0), (0, LANE - ssm_heads))))
            z, xbc, dt, qm = _in_b(x2, g_mix, wz, wx, wm, wdh, wdl)
            padh = lambda t: jnp.pad(t.reshape(1, ssm_heads), ((0, 0), (0, LANE - ssm_heads)))
            seq = _ssd(xbc.reshape(b, s, conv_dim), z.reshape(b, s, seq_width),
                       dt.reshape(b, s, LANE), b_conv_w[j], b_conv_b[j].reshape(1, conv_dim),
                       padh(b_dt_bias[j]), padh(b_a_log[j]),
                       jnp.repeat(b_d_skip[j], SSM_HEAD_DIM).reshape(1, seq_width),
                       b_out_norm[j].reshape(1, seq_width))
        x2, x3 = _out_proj(x2, seq.reshape(n, seq_width), qm, kv_all[i], woa, wob, s)

        wqt = jnp.transpose(peer_w_q[i]).astype(BF16)
        hn3, eidx, gate = _route(x2, ffn_norm[i].reshape(1, d), wqt, peer_sub_keys[i].astype(BF16))
        table3 = _pack_table(peer_u, peer_v, i)
        x2 = _experts(eidx, gate, hn3, x3, final_norm.reshape(SUBLANE, LANE), table3, i == depth - 1)
    return x2.reshape(b, s, d)
```
